```python
import math
import jax, jax.numpy as jnp
from jax import lax
import numpy as np

D_MODEL = 2048
BATCH = 4
SEQ = 4096
DEPTH = 1

CHUNK = 64
PLE_DIM = 256
EPS = 1e-6
A_HEADS = 8
A_HEAD_DIM = 128
A_WIDTH = A_HEADS * A_HEAD_DIM
IDX_HEADS = 16
IDX_DIM = 64
TOPK_MAX = 256
R_HEADS = 8
R_KEY_DIM = 128
R_VAL_DIM = 128
R_WIDTH = R_HEADS * R_VAL_DIM
ROPE_BASE = 10000.0
N_BUCKETS = 32
MAX_DISTANCE = 128

SPLITS = (A_WIDTH, A_WIDTH, A_WIDTH, A_WIDTH,
          IDX_HEADS * IDX_DIM, IDX_DIM, IDX_HEADS,
          R_HEADS * R_KEY_DIM, R_HEADS * R_KEY_DIM,
          R_WIDTH, R_WIDTH,
          D_MODEL, D_MODEL)
IN_WIDTH = int(sum(SPLITS))
SPLIT_POINTS = tuple(int(s) for s in np.cumsum(SPLITS)[:-1])

kernel_name = 'hybrid_dsa_retention_gated_block'


def rmsnorm(x, g):
    xf = x.astype(jnp.float32)
    y = xf * lax.rsqrt(jnp.mean(xf * xf, axis=-1, keepdims=True) + EPS) * g.astype(jnp.float32)
    return y.astype(x.dtype)


def plain_layernorm(x):
    xf = x.astype(jnp.float32)
    mu = jnp.mean(xf, axis=-1, keepdims=True)
    var = jnp.mean((xf - mu) ** 2, axis=-1, keepdims=True)
    return ((xf - mu) * lax.rsqrt(var + EPS)).astype(x.dtype)


def rope(x, positions):
    half = x.shape[-1] // 2
    inv_freq = ROPE_BASE ** (-jnp.arange(half, dtype=jnp.float32) / half)
    ang = positions[:, :, None].astype(jnp.float32) * inv_freq
    cos = jnp.cos(ang)[:, :, None, :]
    sin = jnp.sin(ang)[:, :, None, :]
    xf = x.astype(jnp.float32)
    x1, x2 = xf[..., :half], xf[..., half:]
    return jnp.concatenate([x1 * cos - x2 * sin, x1 * sin + x2 * cos], axis=-1).astype(x.dtype)


def t5_bucket(rel):
    half = N_BUCKETS // 2
    max_exact = half // 2
    ret = jnp.where(rel > 0, half, 0)
    n = jnp.abs(rel)
    nf = jnp.maximum(n, 1).astype(jnp.float32)
    large = max_exact + (jnp.log(nf / max_exact) / math.log(MAX_DISTANCE / max_exact)
                         * (half - max_exact)).astype(jnp.int32)
    large = jnp.minimum(large, half - 1)
    return ret + jnp.where(n < max_exact, n, large)


def dsa_mixer(q, k, v, qi, ki, w, positions, rel_table):
    B, L = q.shape[0], q.shape[1]
    nc = L // CHUNK
    topk = min(TOPK_MAX, L // 4)
    scale = A_HEAD_DIM ** -0.5
    gather = jax.vmap(lambda t, i: t[i])

    def blocks(t):
        return t.reshape((B, nc, CHUNK) + t.shape[2:]).swapaxes(0, 1)

    def one_chunk(args):
        c, q_c, qi_c, w_c, pos_c = args
        limit = (c + 1) * CHUNK
        score = jnp.einsum('bqhd,bsd->bqsh', qi_c, ki)
        index = jnp.einsum('bqsh,bqh->bqs', jax.nn.relu(score), w_c).astype(jnp.float32)
        admissible = jnp.arange(L) < limit
        index = jnp.where(admissible[None, None, :], index, -jnp.inf)
        _, sel = lax.top_k(index, topk)
        sel_ok = sel < limit
        k_sel = gather(k, sel)
        v_sel = gather(v, sel)
        pos_sel = gather(positions, sel)
        bias = rel_table.astype(jnp.float32)[t5_bucket(pos_sel - pos_c[:, :, None])]
        logits = jnp.einsum('bqhd,bqkhd->bqkh', q_c, k_sel).astype(jnp.float32) * scale + bias
        logits = jnp.where(sel_ok[..., None], logits, -jnp.inf)
        probs = jax.nn.softmax(logits, axis=2).astype(v.dtype)
        return jnp.einsum('bqkh,bqkhd->bqhd', probs, v_sel)

    out = lax.map(one_chunk, (jnp.arange(nc), blocks(q), blocks(qi), blocks(w), blocks(positions)))
    return out.swapaxes(0, 1).reshape(B, L, A_WIDTH)


def retention_mixer(q, k, v, gn_gain):
    B, L = q.shape[0], q.shape[1]
    nc = L // CHUNK
    dt = v.dtype
    qf, kf, vf = q.astype(jnp.float32), k.astype(jnp.float32), v.astype(jnp.float32)
    log_g = jnp.log(1.0 - 2.0 ** (-5.0 - jnp.arange(R_HEADS, dtype=jnp.float32)))
    pos = jnp.arange(CHUNK, dtype=jnp.float32)
    dist = jnp.abs(pos[:, None] - pos[None, :])
    intra_decay = jnp.exp(log_g[:, None, None] * dist)
    to_end = jnp.exp(log_g[None, :] * (CHUNK - 1.0 - pos)[:, None])
    from_start = jnp.exp(log_g[None, :] * (pos + 1.0)[:, None])
    chunk_decay = jnp.exp(log_g * CHUNK)
    qc = qf.reshape(B, nc, CHUNK, R_HEADS, R_KEY_DIM)
    kc = kf.reshape(B, nc, CHUNK, R_HEADS, R_KEY_DIM)
    vc = vf.reshape(B, nc, CHUNK, R_HEADS, R_VAL_DIM)
    scores = jnp.einsum('bnqhd,bnkhd->bnhqk', qc, kc) * intra_decay
    intra = jnp.einsum('bnhqk,bnkhe->bnqhe', scores, vc)
    kv = jnp.einsum('bnkhd,bnkhe->nbhde', kc * to_end[:, :, None], vc)

    def step(state, kv_c):
        return state * chunk_decay[None, :, None, None] + kv_c, state

    _, prev = lax.scan(step, jnp.zeros(kv.shape[1:], jnp.float32), kv)
    cross = jnp.einsum('bnqhd,nbhde->bnqhe', qc * from_start[:, :, None], prev)
    y = (intra + cross).reshape(B, L, R_HEADS, R_VAL_DIM)
    mu = jnp.mean(y, axis=-1, keepdims=True)
    var = jnp.mean((y - mu) ** 2, axis=-1, keepdims=True)
    y = ((y - mu) * lax.rsqrt(var + EPS)).reshape(B, L, R_WIDTH) * gn_gain.astype(jnp.float32)
    return y.astype(dt)


def hybrid_layer(x, p_i, positions, w_in, norm_gain, w_a_out, w_b_out, w_o, ret_gn_gain,
                 w_ple, w_ple_gate, rel_bias):
    B, L = x.shape[0], x.shape[1]
    h = rmsnorm(x, norm_gain)
    proj = h @ w_in
    (aq, ak, av, az, iq, ik, iw, rq, rk, rv, rz, ga, gb) = jnp.split(proj, SPLIT_POINTS, axis=-1)
    aq = aq.reshape(B, L, A_HEADS, A_HEAD_DIM)
    ak = ak.reshape(B, L, A_HEADS, A_HEAD_DIM)
    av = av.reshape(B, L, A_HEADS, A_HEAD_DIM)
    iq = iq.reshape(B, L, IDX_HEADS, IDX_DIM) * (IDX_DIM ** -0.5)
    ik = plain_layernorm(ik)
    iw = iw * (IDX_HEADS ** -0.5)
    a_out = dsa_mixer(aq, ak, av, iq, ik, iw, positions, rel_bias) * jax.nn.silu(az)
    rq = rope(rq.reshape(B, L, R_HEADS, R_KEY_DIM), positions)
    rk = rope(rk.reshape(B, L, R_HEADS, R_KEY_DIM), positions) * (R_KEY_DIM ** -0.5)
    rv = rv.reshape(B, L, R_HEADS, R_VAL_DIM)
    b_out = retention_mixer(rq, rk, rv, ret_gn_gain) * jax.nn.silu(rz)
    merged = jax.nn.sigmoid(ga) * (a_out @ w_a_out) + jax.nn.sigmoid(gb) * (b_out @ w_b_out)
    r = x + merged @ w_o
    return r + (p_i @ w_ple) * jax.nn.sigmoid(r @ w_ple_gate)


def setup_inputs(seed: int = 0) -> dict:
    key = jax.random.key(seed)
    ks = jax.random.split(key, 14)
    f32 = jnp.float32
    x = jax.random.normal(ks[0], (BATCH, SEQ, D_MODEL), f32)
    p = jax.random.normal(ks[1], (DEPTH, BATCH, SEQ, PLE_DIM), f32)
    offsets = jax.random.randint(ks[2], (BATCH, 1), 0, 64) * CHUNK
    positions = (offsets + jnp.arange(SEQ)[None, :]).astype(jnp.int32)
    w_in = jax.random.normal(ks[3], (DEPTH, D_MODEL, IN_WIDTH), f32) * D_MODEL ** -0.5
    norm_gain = 1.0 + 0.02 * jax.random.normal(ks[4], (DEPTH, D_MODEL), f32)
    w_a_out = jax.random.normal(ks[5], (DEPTH, A_WIDTH, D_MODEL), f32) * A_WIDTH ** -0.5
    w_b_out = jax.random.normal(ks[6], (DEPTH, R_WIDTH, D_MODEL), f32) * R_WIDTH ** -0.5
    w_o = jax.random.normal(ks[7], (DEPTH, D_MODEL, D_MODEL), f32) * D_MODEL ** -0.5
    ret_gn_gain = 1.0 + 0.02 * jax.random.normal(ks[8], (DEPTH, R_WIDTH), f32)
    w_ple = jax.random.normal(ks[9], (DEPTH, PLE_DIM, D_MODEL), f32) * PLE_DIM ** -0.5
    w_ple_gate = jax.random.normal(ks[10], (DEPTH, D_MODEL, D_MODEL), f32) * D_MODEL ** -0.5
    rel_bias = 0.5 * jax.random.normal(ks[11], (N_BUCKETS, A_HEADS), f32)
    final_gain = 1.0 + 0.02 * jax.random.normal(ks[12], (D_MODEL,), f32)
    return {'x': x, 'p': p, 'positions': positions, 'w_in': w_in, 'norm_gain': norm_gain,
            'w_a_out': w_a_out, 'w_b_out': w_b_out, 'w_o': w_o, 'ret_gn_gain': ret_gn_gain,
            'w_ple': w_ple, 'w_ple_gate': w_ple_gate, 'rel_bias': rel_bias,
            'final_gain': final_gain}


def reference(x, p, positions, w_in, norm_gain, w_a_out, w_b_out, w_o, ret_gn_gain,
              w_ple, w_ple_gate, rel_bias, final_gain):
    for i in range(DEPTH):
        x = hybrid_layer(x, p[i], positions, w_in[i], norm_gain[i], w_a_out[i], w_b_out[i],
                         w_o[i], ret_gn_gain[i], w_ple[i], w_ple_gate[i], rel_bias)
    return rmsnorm(x, final_gain)
```

```python
import functools
import math

import jax
import jax.numpy as jnp
from jax import lax
from jax.experimental import pallas as pl
from jax.experimental.pallas import tpu as pltpu

D_MODEL = 2048
BATCH = 4
SEQ = 4096
TOKENS = BATCH * SEQ
CHUNK = 64
PLE_DIM = 256
EPS = 1e-6
A_HEADS = 8
A_HEAD_DIM = 128
A_WIDTH = A_HEADS * A_HEAD_DIM
IDX_HEADS = 16
IDX_DIM = 64
TOPK = min(256, SEQ // 4)
R_HEADS = 8
R_KEY_DIM = 128
R_VAL_DIM = 128
R_WIDTH = R_HEADS * R_VAL_DIM
ROPE_BASE = 10000.0
N_BUCKETS = 32
MAX_DISTANCE = 128

COL_IQ = 4 * A_WIDTH
COL_IK = COL_IQ + IDX_HEADS * IDX_DIM
COL_IW = COL_IK + IDX_DIM
COL_R = COL_IW + IDX_HEADS

LANES = 128
QB = 256
KB = 256
NQB = SEQ // QB
NKB = SEQ // KB
RB = 512
NEG = -1e30
INT_MIN = -(2 ** 31)
VMEM_LIMIT = 56 * 1024 * 1024

F32 = jnp.float32
BF16 = jnp.bfloat16
NT_DIMS = (((1,), (1,)), ((), ()))


def _rmsnorm_kernel(x_ref, g_ref, o_ref):
    x = x_ref[...]
    ms = jnp.mean(x * x, axis=-1, keepdims=True)
    o_ref[...] = (x * lax.rsqrt(ms + EPS) * g_ref[...]).astype(o_ref.dtype)


def _rmsnorm(x2d, gain, tm=512):
    return pl.pallas_call(
        _rmsnorm_kernel,
        grid=(TOKENS // tm,),
        in_specs=[pl.BlockSpec((tm, D_MODEL), lambda i: (i, 0)),
                  pl.BlockSpec((1, D_MODEL), lambda i: (0, 0))],
        out_specs=pl.BlockSpec((tm, D_MODEL), lambda i: (i, 0)),
        out_shape=jax.ShapeDtypeStruct((TOKENS, D_MODEL), BF16),
        compiler_params=pltpu.CompilerParams(dimension_semantics=("arbitrary",),
                                             vmem_limit_bytes=VMEM_LIMIT),
        name="rmsnorm",
    )(x2d, gain.reshape(1, D_MODEL))


def _proj_kernel(h_ref, w_ref, o_ref, *, scale):
    acc = jnp.dot(h_ref[...], w_ref[...], preferred_element_type=F32)
    if scale != 1.0:
        acc = acc * scale
    o_ref[...] = acc.astype(o_ref.dtype)


def _proj(h, w, col0, ncols, out_dtype, scale=1.0, tm=1024, tn=512, name="proj"):
    c0 = col0 // tn
    assert col0 % tn == 0 and ncols % tn == 0
    return pl.pallas_call(
        functools.partial(_proj_kernel, scale=scale),
        grid=(TOKENS // tm, ncols // tn),
        in_specs=[pl.BlockSpec((tm, D_MODEL), lambda i, j: (i, 0)),
                  pl.BlockSpec((D_MODEL, tn), lambda i, j: (0, j + c0))],
        out_specs=pl.BlockSpec((tm, tn), lambda i, j: (i, j)),
        out_shape=jax.ShapeDtypeStruct((TOKENS, ncols), out_dtype),
        compiler_params=pltpu.CompilerParams(dimension_semantics=("arbitrary", "arbitrary"),
                                             vmem_limit_bytes=VMEM_LIMIT),
        name=name,
    )(h, w)


def _proj_idx_kernel(h_ref, w_ref, ik_ref, iw_ref):
    acc = jnp.dot(h_ref[...], w_ref[...], preferred_element_type=F32)
    ik = acc[:, :LANES]
    mu = jnp.mean(ik, axis=-1, keepdims=True)
    var = jnp.mean((ik - mu) ** 2, axis=-1, keepdims=True)
    ik_ref[...] = ((ik - mu) * lax.rsqrt(var + EPS)).astype(ik_ref.dtype)
    iw_ref[...] = acc[:, LANES:] * (IDX_HEADS ** -0.5)


def _proj_idx(h, w_idx, tm=1024):
    return pl.pallas_call(
        _proj_idx_kernel,
        grid=(TOKENS // tm,),
        in_specs=[pl.BlockSpec((tm, D_MODEL), lambda i: (i, 0)),
                  pl.BlockSpec((D_MODEL, 2 * LANES), lambda i: (0, 0))],
        out_specs=[pl.BlockSpec((tm, LANES), lambda i: (i, 0)),
                   pl.BlockSpec((tm, LANES), lambda i: (i, 0))],
        out_shape=[jax.ShapeDtypeStruct((TOKENS, LANES), BF16),
                   jax.ShapeDtypeStruct((TOKENS, LANES), F32)],
        compiler_params=pltpu.CompilerParams(dimension_semantics=("arbitrary",),
                                             vmem_limit_bytes=VMEM_LIMIT),
        name="proj_idx",
    )(h, w_idx)


def _proj_rope_kernel(h_ref, w_ref, cos_ref, sin_ref, o_ref, *, tn, k_tile0, k_scale):
    acc = jnp.dot(h_ref[...], w_ref[...], preferred_element_type=F32)
    cos = cos_ref[...]
    sin = sin_ref[...]
    scale = jnp.where(pl.program_id(1) >= k_tile0, k_scale, 1.0).astype(F32)
    for g in range(tn // LANES):
        xg = acc[:, g * LANES:(g + 1) * LANES]
        rot = xg * cos + pltpu.roll(xg, LANES // 2, 1) * sin
        o_ref[:, g * LANES:(g + 1) * LANES] = (rot * scale).astype(o_ref.dtype)


def _proj_rope(h, w, col0, cos2, sin2, tm=1024, tn=512):
    ncols = 2 * R_HEADS * R_KEY_DIM
    c0 = col0 // tn
    assert col0 % tn == 0
    kern = functools.partial(_proj_rope_kernel, tn=tn, k_tile0=(R_HEADS * R_KEY_DIM) // tn,
                             k_scale=R_KEY_DIM ** -0.5)
    return pl.pallas_call(
        kern,
        grid=(TOKENS // tm, ncols // tn),
        in_specs=[pl.BlockSpec((tm, D_MODEL), lambda i, j: (i, 0)),
                  pl.BlockSpec((D_MODEL, tn), lambda i, j: (0, j + c0)),
                  pl.BlockSpec((tm, LANES), lambda i, j: (i, 0)),
                  pl.BlockSpec((tm, LANES), lambda i, j: (i, 0))],
        out_specs=pl.BlockSpec((tm, tn), lambda i, j: (i, j)),
        out_shape=jax.ShapeDtypeStruct((TOKENS, ncols), BF16),
        compiler_params=pltpu.CompilerParams(dimension_semantics=("arbitrary", "arbitrary"),
                                             vmem_limit_bytes=VMEM_LIMIT),
        name="proj_rope",
    )(h, w, cos2, sin2)


def _to_key(a):
    bits = pltpu.bitcast(a, jnp.int32)
    return jnp.where(bits < 0, bits ^ jnp.int32(0x7FFFFFFF), bits)


def _dsa_kernel(pmin_ref, pmax_ref,
                q_ref, k_ref, v_ref, iq_ref, ikd_ref, iw_ref, posq_ref, posk_ref, az_ref,
                tab_ref, far_ref, o_ref,
                keys_scr, iqm_scr, mb_scr, acc_scr, m_scr, l_scr):
    b = pl.program_id(0)
    qi = pl.program_id(1)
    nkb = qi + 1

    lane = lax.broadcasted_iota(jnp.int32, (QB, LANES), 1)
    for p in range(IDX_HEADS // 2):
        pair = iq_ref[:, p * LANES:(p + 1) * LANES].astype(F32)
        iqm_scr[2 * p] = jnp.where(lane < IDX_DIM, pair, 0.0).astype(BF16)
        iqm_scr[2 * p + 1] = jnp.where(lane >= IDX_DIM, pair, 0.0).astype(BF16)
    iw = iw_ref[...]

    def idx_tile(kb):
        s0 = pl.multiple_of(kb * KB, KB)
        kid = ikd_ref[pl.ds(s0, KB), :]
        acc = jnp.zeros((QB, KB), F32)
        for h in range(IDX_HEADS):
            sc = lax.dot_general(iqm_scr[h], kid, NT_DIMS, preferred_element_type=F32)
            acc = acc + jnp.maximum(sc, 0.0) * iw[:, h:h + 1]
        return _to_key(acc)

    def idx_body(kb, c):
        keys_scr[kb] = idx_tile(kb)
        return c

    lax.fori_loop(0, qi, idx_body, 0)
    rr = lax.broadcasted_iota(jnp.int32, (QB, KB), 0)
    cc = lax.broadcasted_iota(jnp.int32, (QB, KB), 1)
    keys_scr[qi] = jnp.where((cc // CHUNK) <= (rr // CHUNK), idx_tile(qi), jnp.int32(INT_MIN))

    def count_where(pred_fn):
        def body(kb, acc):
            part = jnp.where(pred_fn(kb, keys_scr[kb]), 1.0, 0.0)
            return acc + part[:, :LANES] + part[:, LANES:]
        acc = lax.fori_loop(0, nkb, body, jnp.zeros((QB, LANES), F32))
        return jnp.sum(acc, axis=1, keepdims=True)

    def bisect_body(it, ans):
        bit = jnp.left_shift(jnp.int32(1), 31 - it)
        cand = ans | bit
        cand_s = cand ^ jnp.int32(INT_MIN)
        cnt = count_where(lambda kb, kk: kk >= cand_s)
        return jnp.where(cnt >= TOPK, cand, ans)

    ans0 = jnp.where(qi > 0, 0, 1) * jnp.ones((QB, 1), jnp.int32)
    ans = lax.fori_loop(0, jnp.where(qi > 0, 32, 0), bisect_body, ans0)
    thr = ans ^ jnp.int32(INT_MIN)

    cnt_ge = count_where(lambda kb, kk: kk >= thr)
    has_tie = jnp.logical_and(qi > 0, jnp.max(cnt_ge) > TOPK)

    @pl.when(has_tie)
    def _():
        cnt_gt = count_where(lambda kb, kk: kk > thr)
        need = TOPK - cnt_gt

        def col_of(kb):
            return kb * KB + cc

        def jb_body(it, j0):
            cand = j0 | jnp.left_shift(jnp.int32(1), 11 - it)
            f = count_where(lambda kb, kk: jnp.logical_and(kk == thr, col_of(kb) < cand))
            return jnp.where(f < need, cand, j0)

        j0 = lax.fori_loop(0, 12, jb_body, jnp.zeros((QB, 1), jnp.int32))
        jstar = j0 + 1

        def fix_body(kb, c):
            kk = keys_scr[kb]
            drop = jnp.logical_and(kk == thr, col_of(kb) >= jstar)
            keys_scr[kb] = jnp.where(drop, kk - 1, kk)
            return c

        lax.fori_loop(0, nkb, fix_body, 0)

    m_scr[...] = jnp.full(m_scr.shape, NEG, F32)
    l_scr[...] = jnp.zeros(l_scr.shape, F32)
    acc_scr[...] = jnp.zeros(acc_scr.shape, F32)
    posq = posq_ref[...]
    pmin_q = pmin_ref[b, qi]
    scale = A_HEAD_DIM ** -0.5
    half = N_BUCKETS // 2
    max_exact = half // 2

    def att_body(kb, c):
        s0 = pl.multiple_of(kb * KB, KB)
        madd = jnp.where(keys_scr[kb] >= thr, 0.0, NEG)
        far = (pmin_q - pmax_ref[b, kb]) >= MAX_DISTANCE

        @pl.when(far)
        def _():
            for h in range(A_HEADS):
                mb_scr[h] = madd + far_ref[h]

        @pl.when(jnp.logical_not(far))
        def _():
            rel = posk_ref[kb] - posq
            n = jnp.abs(rel)
            nf = jnp.maximum(n, 1).astype(F32)
            large = max_exact + (jnp.log(nf / max_exact) / math.log(MAX_DISTANCE / max_exact)
                                 * (half - max_exact)).astype(jnp.int32)
            large = jnp.minimum(large, half - 1)
            bucket = jnp.where(rel > 0, half, 0) + jnp.where(n < max_exact, n, large)
            for h in range(A_HEADS):
                row = jnp.broadcast_to(tab_ref[h:h + 1, :], (QB, LANES))
                bias = jnp.concatenate(
                    [jnp.take_along_axis(row, bucket[:, g * LANES:(g + 1) * LANES], axis=1)
                     for g in range(KB // LANES)], axis=1)
                mb_scr[h] = madd + bias

        for h in range(A_HEADS):
            hs = slice(h * A_HEAD_DIM, (h + 1) * A_HEAD_DIM)
            qh = q_ref[:, hs]
            kh = k_ref[pl.ds(s0, KB), hs]
            vh = v_ref[pl.ds(s0, KB), hs]
            s = lax.dot_general(qh, kh, NT_DIMS, preferred_element_type=F32) * scale + mb_scr[h]
            m_old = m_scr[h]
            m_new = jnp.maximum(m_old, jnp.max(s, axis=1, keepdims=True))
            alpha = jnp.exp(m_old - m_new)
            p = jnp.exp(s - m_new)
            l_scr[h] = alpha * l_scr[h] + jnp.sum(p, axis=1, keepdims=True)
            acc_scr[:, hs] = alpha * acc_scr[:, hs] + jnp.dot(p.astype(BF16), vh,
                                                              preferred_element_type=F32)
            m_scr[h] = m_new
        return c

    lax.fori_loop(0, nkb, att_body, 0)

    for h in range(A_HEADS):
        hs = slice(h * A_HEAD_DIM, (h + 1) * A_HEAD_DIM)
        z = az_ref[:, hs]
        o_ref[:, hs] = (acc_scr[:, hs] / l_scr[h] * (z * jax.nn.sigmoid(z))).astype(o_ref.dtype)


def _dsa(qkv, iq, ikd, iw, az, positions, rel_bias):
    qkv3 = qkv.reshape(BATCH, SEQ, 3 * A_WIDTH)
    iq3 = iq.reshape(BATCH, SEQ, A_WIDTH)
    ikd3 = ikd.reshape(BATCH, SEQ, LANES)
    iw3 = iw.reshape(BATCH, SEQ, LANES)
    az3 = az.reshape(BATCH, SEQ, A_WIDTH)
    posq = positions.reshape(BATCH, SEQ, 1)
    posk = positions.reshape(BATCH, NKB, 1, KB)
    pblk = positions.reshape(BATCH, NKB, KB)
    pmin = jnp.min(pblk, axis=-1)
    pmax = jnp.max(pblk, axis=-1)
    tab = jnp.zeros((A_HEADS, LANES), F32).at[:, :N_BUCKETS].set(rel_bias.astype(F32).T)
    far = rel_bias[N_BUCKETS // 2 - 1, :].astype(F32)

    grid_spec = pltpu.PrefetchScalarGridSpec(
        num_scalar_prefetch=2,
        grid=(BATCH, NQB),
        in_specs=[
            pl.BlockSpec((None, QB, A_WIDTH), lambda b, i, *_: (b, i, 0)),
            pl.BlockSpec((None, SEQ, A_WIDTH), lambda b, i, *_: (b, 0, 1),
                         pipeline_mode=pl.Buffered(1)),
            pl.BlockSpec((None, SEQ, A_WIDTH), lambda b, i, *_: (b, 0, 2),
                         pipeline_mode=pl.Buffered(1)),
            pl.BlockSpec((None, QB, A_WIDTH), lambda b, i, *_: (b, i, 0)),
            pl.BlockSpec((None, SEQ, LANES), lambda b, i, *_: (b, 0, 0)),
            pl.BlockSpec((None, QB, LANES), lambda b, i, *_: (b, i, 0)),
            pl.BlockSpec((None, QB, 1), lambda b, i, *_: (b, i, 0)),
            pl.BlockSpec((None, NKB, 1, KB), lambda b, i, *_: (b, 0, 0, 0)),
            pl.BlockSpec((None, QB, A_WIDTH), lambda b, i, *_: (b, i, 0)),
            pl.BlockSpec((A_HEADS, LANES), lambda b, i, *_: (0, 0)),
            pl.BlockSpec(memory_space=pltpu.SMEM),
        ],
        out_specs=pl.BlockSpec((None, QB, A_WIDTH), lambda b, i, *_: (b, i, 0)),
        scratch_shapes=[
            pltpu.VMEM((NKB, QB, KB), jnp.int32),
            pltpu.VMEM((IDX_HEADS, QB, LANES), BF16),
            pltpu.VMEM((A_HEADS, QB, KB), F32),
            pltpu.VMEM((QB, A_WIDTH), F32),
            pltpu.VMEM((A_HEADS, QB, 1), F32),
            pltpu.VMEM((A_HEADS, QB, 1), F32),
        ],
    )
    out = pl.pallas_call(
        _dsa_kernel,
        grid_spec=grid_spec,
        out_shape=jax.ShapeDtypeStruct((BATCH, SEQ, A_WIDTH), BF16),
        compiler_params=pltpu.CompilerParams(dimension_semantics=("arbitrary", "arbitrary"),
                                             vmem_limit_bytes=VMEM_LIMIT),
        name="dsa",
    )(pmin, pmax, qkv3, qkv3, qkv3, iq3, ikd3, iw3, posq, posk, az3, tab, far)
    return out.reshape(TOKENS, A_WIDTH)


def _ret_kernel(cdec_ref, q_ref, k_ref, v_ref, z_ref, gain_ref, dec_ref, te_ref, fs_ref, o_ref,
                state_scr):
    @pl.when(pl.program_id(1) == 0)
    def _():
        state_scr[...] = jnp.zeros(state_scr.shape, F32)

    def chunk_body(c, carry):
        r0 = pl.multiple_of(c * CHUNK, CHUNK)
        for h in range(R_HEADS):
            hs = slice(h * R_KEY_DIM, (h + 1) * R_KEY_DIM)
            q = q_ref[pl.ds(r0, CHUNK), hs]
            k = k_ref[pl.ds(r0, CHUNK), hs]
            v = v_ref[pl.ds(r0, CHUNK), hs]
            sc = lax.dot_general(q, k, NT_DIMS, preferred_element_type=F32) * dec_ref[h]
            intra = jnp.dot(sc.astype(BF16), v, preferred_element_type=F32)
            state = state_scr[h]
            qs = (q.astype(F32) * fs_ref[h]).astype(BF16)
            cross = jnp.dot(qs, state.astype(BF16), preferred_element_type=F32)
            ke = (k.astype(F32) * te_ref[h]).T.astype(BF16)
            kv = jnp.dot(ke, v, preferred_element_type=F32)
            state_scr[h] = state * cdec_ref[h] + kv
            y = intra + cross
            mu = jnp.mean(y, axis=-1, keepdims=True)
            var = jnp.mean((y - mu) ** 2, axis=-1, keepdims=True)
            yn = (y - mu) * lax.rsqrt(var + EPS) * gain_ref[:, hs]
            z = z_ref[pl.ds(r0, CHUNK), hs]
            o_ref[pl.ds(r0, CHUNK), hs] = (yn * (z * jax.nn.sigmoid(z))).astype(o_ref.dtype)
        return carry

    lax.fori_loop(0, RB // CHUNK, chunk_body, 0)


def _retention(rqk, rv, gates, gn_gain):
    log_g = jnp.log(1.0 - 2.0 ** (-5.0 - jnp.arange(R_HEADS, dtype=F32)))
    pos = jnp.arange(CHUNK, dtype=F32)
    dist = jnp.abs(pos[:, None] - pos[None, :])
    intra_decay = jnp.exp(log_g[:, None, None] * dist)
    to_end = jnp.exp(log_g[:, None] * (CHUNK - 1.0 - pos)[None, :])
    from_start = jnp.exp(log_g[:, None] * (pos + 1.0)[None, :])
    chunk_decay = jnp.exp(log_g * CHUNK)
    te = jnp.broadcast_to(to_end[:, :, None], (R_HEADS, CHUNK, R_KEY_DIM))
    fs = jnp.broadcast_to(from_start[:, :, None], (R_HEADS, CHUNK, R_KEY_DIM))

    rqk3 = rqk.reshape(BATCH, SEQ, 2 * R_WIDTH)
    rv3 = rv.reshape(BATCH, SEQ, R_WIDTH)
    g3 = gates.reshape(BATCH, SEQ, gates.shape[-1])
    out = pl.pallas_call(
        _ret_kernel,
        grid=(BATCH, SEQ // RB),
        in_specs=[
            pl.BlockSpec(memory_space=pltpu.SMEM),
            pl.BlockSpec((None, RB, R_WIDTH), lambda b, i: (b, i, 0)),
            pl.BlockSpec((None, RB, R_WIDTH), lambda b, i: (b, i, 1)),
            pl.BlockSpec((None, RB, R_WIDTH), lambda b, i: (b, i, 0)),
            pl.BlockSpec((None, RB, R_WIDTH), lambda b, i: (b, i, 0)),
            pl.BlockSpec((1, R_WIDTH), lambda b, i: (0, 0)),
            pl.BlockSpec((R_HEADS, CHUNK, CHUNK), lambda b, i: (0, 0, 0)),
            pl.BlockSpec((R_HEADS, CHUNK, R_KEY_DIM), lambda b, i: (0, 0, 0)),
            pl.BlockSpec((R_HEADS, CHUNK, R_KEY_DIM), lambda b, i: (0, 0, 0)),
        ],
        out_specs=pl.BlockSpec((None, RB, R_WIDTH), lambda b, i: (b, i, 0)),
        out_shape=jax.ShapeDtypeStruct((BATCH, SEQ, R_WIDTH), BF16),
        scratch_shapes=[pltpu.VMEM((R_HEADS, R_KEY_DIM, R_VAL_DIM), F32)],
        compiler_params=pltpu.CompilerParams(dimension_semantics=("arbitrary", "arbitrary"),
                                             vmem_limit_bytes=VMEM_LIMIT),
        name="retention",
    )(chunk_decay, rqk3, rqk3, rv3, g3, gn_gain.reshape(1, R_WIDTH), intra_decay, te, fs)
    return out.reshape(TOKENS, R_WIDTH)


def _merge_kernel(a_ref, wa_ref, b_ref, wb_ref, ga_ref, gb_ref, o_ref):
    ta = jnp.dot(a_ref[...], wa_ref[...], preferred_element_type=F32)
    tb = jnp.dot(b_ref[...], wb_ref[...], preferred_element_type=F32)
    o_ref[...] = (jax.nn.sigmoid(ga_ref[...]) * ta + jax.nn.sigmoid(gb_ref[...]) * tb).astype(o_ref.dtype)


def _merge(a_out, wa, b_out, wb, gates, tm=1024, tn=512):
    ga0 = R_WIDTH // tn
    gb0 = (R_WIDTH + D_MODEL) // tn
    return pl.pallas_call(
        _merge_kernel,
        grid=(TOKENS // tm, D_MODEL // tn),
        in_specs=[pl.BlockSpec((tm, A_WIDTH), lambda i, j: (i, 0)),
                  pl.BlockSpec((A_WIDTH, tn), lambda i, j: (0, j)),
                  pl.BlockSpec((tm, R_WIDTH), lambda i, j: (i, 0)),
                  pl.BlockSpec((R_WIDTH, tn), lambda i, j: (0, j)),
                  pl.BlockSpec((tm, tn), lambda i, j: (i, j + ga0)),
                  pl.BlockSpec((tm, tn), lambda i, j: (i, j + gb0))],
        out_specs=pl.BlockSpec((tm, tn), lambda i, j: (i, j)),
        out_shape=jax.ShapeDtypeStruct((TOKENS, D_MODEL), BF16),
        compiler_params=pltpu.CompilerParams(dimension_semantics=("arbitrary", "arbitrary"),
                                             vmem_limit_bytes=VMEM_LIMIT),
        name="merge",
    )(a_out, wa, b_out, wb, gates, gates)


def _resid_kernel(m_ref, w_ref, x_ref, o_ref):
    o_ref[...] = x_ref[...] + jnp.dot(m_ref[...], w_ref[...], preferred_element_type=F32)


def _resid(merged, w_o, x2d, tm=1024, tn=512):
    return pl.pallas_call(
        _resid_kernel,
        grid=(TOKENS // tm, D_MODEL // tn),
        in_specs=[pl.BlockSpec((tm, D_MODEL), lambda i, j: (i, 0)),
                  pl.BlockSpec((D_MODEL, tn), lambda i, j: (0, j)),
                  pl.BlockSpec((tm, tn), lambda i, j: (i, j))],
        out_specs=pl.BlockSpec((tm, tn), lambda i, j: (i, j)),
        out_shape=jax.ShapeDtypeStruct((TOKENS, D_MODEL), F32),
        compiler_params=pltpu.CompilerParams(dimension_semantics=("arbitrary", "arbitrary"),
                                             vmem_limit_bytes=VMEM_LIMIT),
        name="resid",
    )(merged, w_o, x2d)


def _final_kernel(r_ref, p_ref, wp_ref, wg_ref, fg_ref, o_ref):
    r = r_ref[...]
    u = jnp.dot(p_ref[...].astype(BF16), wp_ref[...], preferred_element_type=F32)
    g = jnp.dot(r.astype(BF16), wg_ref[...], preferred_element_type=F32)
    y = r + u * jax.nn.sigmoid(g)
    ms = jnp.mean(y * y, axis=-1, keepdims=True)
    o_ref[...] = y * lax.rsqrt(ms + EPS) * fg_ref[...]


def _final(r, p2d, w_ple, w_gate, final_gain, tm=256):
    return pl.pallas_call(
        _final_kernel,
        grid=(TOKENS // tm,),
        in_specs=[pl.BlockSpec((tm, D_MODEL), lambda i: (i, 0)),
                  pl.BlockSpec((tm, PLE_DIM), lambda i: (i, 0)),
                  pl.BlockSpec((PLE_DIM, D_MODEL), lambda i: (0, 0), pipeline_mode=pl.Buffered(1)),
                  pl.BlockSpec((D_MODEL, D_MODEL), lambda i: (0, 0), pipeline_mode=pl.Buffered(1)),
                  pl.BlockSpec((1, D_MODEL), lambda i: (0, 0))],
        out_specs=pl.BlockSpec((tm, D_MODEL), lambda i: (i, 0)),
        out_shape=jax.ShapeDtypeStruct((TOKENS, D_MODEL), F32),
        compiler_params=pltpu.CompilerParams(dimension_semantics=("arbitrary",),
                                             vmem_limit_bytes=VMEM_LIMIT),
        name="final",
    )(r, p2d, w_ple, w_gate, final_gain.reshape(1, D_MODEL))


def _layer(x2d, p2d, positions, w_in, norm_gain, w_a_out, w_b_out, w_o, ret_gn_gain, w_ple,
           w_ple_gate, rel_bias):
    w_head = w_in[:, :COL_IK].astype(BF16)
    w_ik = w_in[:, COL_IK:COL_IW]
    w_iw = w_in[:, COL_IW:COL_R]
    w_idx = jnp.concatenate(
        [w_ik, w_ik, w_iw, jnp.zeros((D_MODEL, LANES - IDX_HEADS), w_in.dtype)], axis=1).astype(BF16)
    w_rest = w_in[:, COL_R:].astype(BF16)

    half = R_KEY_DIM // 2
    inv_freq = ROPE_BASE ** (-jnp.arange(half, dtype=F32) / half)
    ang = positions[:, :, None].astype(F32) * inv_freq
    cos = jnp.cos(ang)
    sin = jnp.sin(ang)
    cos2 = jnp.concatenate([cos, cos], axis=-1).reshape(TOKENS, R_KEY_DIM)
    sin2 = jnp.concatenate([-sin, sin], axis=-1).reshape(TOKENS, R_KEY_DIM)

    h = _rmsnorm(x2d, norm_gain)
    qkv = _proj(h, w_head, 0, 3 * A_WIDTH, BF16, name="proj_qkv")
    az = _proj(h, w_head, 3 * A_WIDTH, A_WIDTH, F32, name="proj_az")
    iq = _proj(h, w_head, COL_IQ, IDX_HEADS * IDX_DIM, BF16, scale=IDX_DIM ** -0.5, name="proj_iq")
    ikd, iw = _proj_idx(h, w_idx)
    rqk = _proj_rope(h, w_rest, 0, cos2, sin2)
    rv = _proj(h, w_rest, 2 * R_WIDTH, R_WIDTH, BF16, name="proj_rv")
    gates = _proj(h, w_rest, 3 * R_WIDTH, R_WIDTH + 2 * D_MODEL, F32, name="proj_gates")

    a_out = _dsa(qkv, iq, ikd, iw, az, positions, rel_bias)
    b_out = _retention(rqk, rv, gates, ret_gn_gain)

    merged = _merge(a_out, w_a_out.astype(BF16), b_out, w_b_out.astype(BF16), gates)
    r = _resid(merged, w_o.astype(BF16), x2d)
    return r, (p2d, w_ple.astype(BF16), w_ple_gate.astype(BF16))


def kernel(x, p, positions, w_in, norm_gain, w_a_out, w_b_out, w_o, ret_gn_gain, w_ple, w_ple_gate,
           rel_bias, final_gain):
    assert x.shape == (BATCH, SEQ, D_MODEL) and w_in.shape[0] == 1
    x2d = x.reshape(TOKENS, D_MODEL)
    r, (p2d, wp, wg) = _layer(x2d, p[0].reshape(TOKENS, PLE_DIM), positions, w_in[0], norm_gain[0],
                              w_a_out[0], w_b_out[0], w_o[0], ret_gn_gain[0], w_ple[0],
                              w_ple_gate[0], rel_bias)
    out = _final(r, p2d, wp, wg, final_gain)
    return out.reshape(BATCH, SEQ, D_MODEL)
```

```python
import functools
import math

import jax
import jax.numpy as jnp
from jax import lax
from jax.experimental import pallas as pl
from jax.experimental.pallas import tpu as pltpu

D_MODEL = 2048
BATCH = 4
SEQ = 4096
TOKENS = BATCH * SEQ
CHUNK = 64
PLE_DIM = 256
EPS = 1e-6
A_HEADS = 8
A_HEAD_DIM = 128
A_WIDTH = A_HEADS * A_HEAD_DIM
IDX_HEADS = 16
IDX_DIM = 64
TOPK = min(256, SEQ // 4)
R_HEADS = 8
R_KEY_DIM = 128
R_VAL_DIM = 128
R_WIDTH = R_HEADS * R_VAL_DIM
ROPE_BASE = 10000.0
N_BUCKETS = 32
MAX_DISTANCE = 128

COL_IQ = 4 * A_WIDTH
COL_IK = COL_IQ + IDX_HEADS * IDX_DIM
COL_IW = COL_IK + IDX_DIM
COL_R = COL_IW + IDX_HEADS

LANES = 128
QB = 256
KB = 256
NQB = SEQ // QB
NKB = SEQ // KB
RB = 512
NEG = -1e30
INT_MIN = -(2 ** 31)
VMEM_LIMIT = 56 * 1024 * 1024

F32 = jnp.float32
BF16 = jnp.bfloat16
NT_DIMS = (((1,), (1,)), ((), ()))


def _rmsnorm_kernel(x_ref, g_ref, o_ref):
    x = x_ref[...]
    ms = jnp.mean(x * x, axis=-1, keepdims=True)
    o_ref[...] = (x * lax.rsqrt(ms + EPS) * g_ref[...]).astype(o_ref.dtype)


def _rmsnorm(x2d, gain, tm=512):
    return pl.pallas_call(
        _rmsnorm_kernel,
        grid=(TOKENS // tm,),
        in_specs=[pl.BlockSpec((tm, D_MODEL), lambda i: (i, 0)),
                  pl.BlockSpec((1, D_MODEL), lambda i: (0, 0))],
        out_specs=pl.BlockSpec((tm, D_MODEL), lambda i: (i, 0)),
        out_shape=jax.ShapeDtypeStruct((TOKENS, D_MODEL), BF16),
        compiler_params=pltpu.CompilerParams(dimension_semantics=("arbitrary",),
                                             vmem_limit_bytes=VMEM_LIMIT),
        name="rmsnorm",
    )(x2d, gain.reshape(1, D_MODEL))


def _proj_kernel(h_ref, w_ref, o_ref, *, scale):
    acc = jnp.dot(h_ref[...], w_ref[...], preferred_element_type=F32)
    if scale != 1.0:
        acc = acc * scale
    o_ref[...] = acc.astype(o_ref.dtype)


def _proj(h, w, col0, ncols, out_dtype, scale=1.0, tm=1024, tn=512, name="proj"):
    c0 = col0 // tn
    assert col0 % tn == 0 and ncols % tn == 0
    return pl.pallas_call(
        functools.partial(_proj_kernel, scale=scale),
        grid=(TOKENS // tm, ncols // tn),
        in_specs=[pl.BlockSpec((tm, D_MODEL), lambda i, j: (i, 0)),
                  pl.BlockSpec((D_MODEL, tn), lambda i, j: (0, j + c0))],
        out_specs=pl.BlockSpec((tm, tn), lambda i, j: (i, j)),
        out_shape=jax.ShapeDtypeStruct((TOKENS, ncols), out_dtype),
        compiler_params=pltpu.CompilerParams(dimension_semantics=("arbitrary", "arbitrary"),
                                             vmem_limit_bytes=VMEM_LIMIT),
        name=name,
    )(h, w)


def _proj_idx_kernel(h_ref, w_ref, ik_ref, iw_ref):
    acc = jnp.dot(h_ref[...], w_ref[...], preferred_element_type=F32)
    ik = acc[:, :LANES]
    mu = jnp.mean(ik, axis=-1, keepdims=True)
    var = jnp.mean((ik - mu) ** 2, axis=-1, keepdims=True)
    ik_ref[...] = ((ik - mu) * lax.rsqrt(var + EPS)).astype(ik_ref.dtype)
    iw_ref[...] = acc[:, LANES:] * (IDX_HEADS ** -0.5)


def _proj_idx(h, w_idx, tm=1024):
    return pl.pallas_call(
        _proj_idx_kernel,
        grid=(TOKENS // tm,),
        in_specs=[pl.BlockSpec((tm, D_MODEL), lambda i: (i, 0)),
                  pl.BlockSpec((D_MODEL, 2 * LANES), lambda i: (0, 0))],
        out_specs=[pl.BlockSpec((tm, LANES), lambda i: (i, 0)),
                   pl.BlockSpec((tm, LANES), lambda i: (i, 0))],
        out_shape=[jax.ShapeDtypeStruct((TOKENS, LANES), BF16),
                   jax.ShapeDtypeStruct((TOKENS, LANES), F32)],
        compiler_params=pltpu.CompilerParams(dimension_semantics=("arbitrary",),
                                             vmem_limit_bytes=VMEM_LIMIT),
        name="proj_idx",
    )(h, w_idx)


def _proj_rope_kernel(h_ref, w_ref, cos_ref, sin_ref, o_ref, *, tn, k_tile0, k_scale):
    acc = jnp.dot(h_ref[...], w_ref[...], preferred_element_type=F32)
    cos = cos_ref[...]
    sin = sin_ref[...]
    scale = jnp.where(pl.program_id(1) >= k_tile0, k_scale, 1.0).astype(F32)
    for g in range(tn // LANES):
        xg = acc[:, g * LANES:(g + 1) * LANES]
        rot = xg * cos + pltpu.roll(xg, LANES // 2, 1) * sin
        o_ref[:, g * LANES:(g + 1) * LANES] = (rot * scale).astype(o_ref.dtype)


def _proj_rope(h, w, col0, cos2, sin2, tm=1024, tn=512):
    ncols = 2 * R_HEADS * R_KEY_DIM
    c0 = col0 // tn
    assert col0 % tn == 0
    kern = functools.partial(_proj_rope_kernel, tn=tn, k_tile0=(R_HEADS * R_KEY_DIM) // tn,
                             k_scale=R_KEY_DIM ** -0.5)
    return pl.pallas_call(
        kern,
        grid=(TOKENS // tm, ncols // tn),
        in_specs=[pl.BlockSpec((tm, D_MODEL), lambda i, j: (i, 0)),
                  pl.BlockSpec((D_MODEL, tn), lambda i, j: (0, j + c0)),
                  pl.BlockSpec((tm, LANES), lambda i, j: (i, 0)),
                  pl.BlockSpec((tm, LANES), lambda i, j: (i, 0))],
        out_specs=pl.BlockSpec((tm, tn), lambda i, j: (i, j)),
        out_shape=jax.ShapeDtypeStruct((TOKENS, ncols), BF16),
        compiler_params=pltpu.CompilerParams(dimension_semantics=("arbitrary", "arbitrary"),
                                             vmem_limit_bytes=VMEM_LIMIT),
        name="proj_rope",
    )(h, w, cos2, sin2)


def _to_key(a):
    bits = pltpu.bitcast(a, jnp.int32)
    return jnp.where(bits < 0, bits ^ jnp.int32(0x7FFFFFFF), bits)


def _dsa_kernel(pmin_ref, pmax_ref,
                q_ref, k_ref, vt_ref, iq_ref, ikd_ref, iwt_ref, posq_ref, posk_ref, az_ref,
                tab_ref, far_ref, o_ref,
                keys_scr, iqm_scr, mb_scr, acc_scr, m_scr, l_scr, alpha_scr, s_scr, p_scr):
    b = pl.program_id(0)
    qi = pl.program_id(1)
    nkb = qi + 1

    lane = lax.broadcasted_iota(jnp.int32, (QB, LANES), 1)
    for p in range(IDX_HEADS // 2):
        pair = iq_ref[:, p * LANES:(p + 1) * LANES].astype(F32)
        iqm_scr[2 * p] = jnp.where(lane < IDX_DIM, pair, 0.0).astype(BF16)
        iqm_scr[2 * p + 1] = jnp.where(lane >= IDX_DIM, pair, 0.0).astype(BF16)
    iwt = iwt_ref[...]

    def idx_tile(kb):
        s0 = pl.multiple_of(kb * KB, KB)
        kid = ikd_ref[pl.ds(s0, KB), :]
        acc = jnp.zeros((KB, QB), F32)
        for h in range(IDX_HEADS):
            sc = lax.dot_general(kid, iqm_scr[h], NT_DIMS, preferred_element_type=F32)
            acc = acc + jnp.maximum(sc, 0.0) * iwt[h:h + 1, :]
        return _to_key(acc)

    def idx_body(kb, c):
        keys_scr[kb] = idx_tile(kb)
        return c

    lax.fori_loop(0, qi, idx_body, 0)
    rr = lax.broadcasted_iota(jnp.int32, (KB, QB), 0)
    cc = lax.broadcasted_iota(jnp.int32, (KB, QB), 1)
    keys_scr[qi] = jnp.where((rr // CHUNK) <= (cc // CHUNK), idx_tile(qi), jnp.int32(INT_MIN))

    def count_where(pred_fn):
        def body(kb, acc):
            part = jnp.where(pred_fn(kb, keys_scr[kb]), 1.0, 0.0)
            return acc + jnp.sum(part.reshape(KB // 8, 8, QB), axis=0)
        acc = lax.fori_loop(0, nkb, body, jnp.zeros((8, QB), F32))
        return jnp.sum(acc, axis=0, keepdims=True)

    def bisect_body(it, ans):
        bit = jnp.left_shift(jnp.int32(1), 31 - it)
        cand = ans | bit
        cand_s = cand ^ jnp.int32(INT_MIN)
        cnt = count_where(lambda kb, kk: kk >= cand_s)
        return jnp.where(cnt >= TOPK, cand, ans)

    ans0 = jnp.where(qi > 0, 0, 1) * jnp.ones((1, QB), jnp.int32)
    ans = lax.fori_loop(0, jnp.where(qi > 0, 32, 0), bisect_body, ans0)
    thr = ans ^ jnp.int32(INT_MIN)

    cnt_ge = count_where(lambda kb, kk: kk >= thr)
    has_tie = jnp.logical_and(qi > 0, jnp.max(cnt_ge) > TOPK)

    @pl.when(has_tie)
    def _():
        cnt_gt = count_where(lambda kb, kk: kk > thr)
        need = TOPK - cnt_gt

        def key_index(kb):
            return kb * KB + rr

        def jb_body(it, j0):
            cand = j0 | jnp.left_shift(jnp.int32(1), 11 - it)
            f = count_where(lambda kb, kk: jnp.logical_and(kk == thr, key_index(kb) < cand))
            return jnp.where(f < need, cand, j0)

        j0 = lax.fori_loop(0, 12, jb_body, jnp.zeros((1, QB), jnp.int32))
        jstar = j0 + 1

        def fix_body(kb, c):
            kk = keys_scr[kb]
            drop = jnp.logical_and(kk == thr, key_index(kb) >= jstar)
            keys_scr[kb] = jnp.where(drop, kk - 1, kk)
            return c

        lax.fori_loop(0, nkb, fix_body, 0)

    m_scr[...] = jnp.full(m_scr.shape, NEG, F32)
    l_scr[...] = jnp.zeros(l_scr.shape, F32)
    acc_scr[...] = jnp.zeros(acc_scr.shape, F32)
    for h in range(A_HEADS):
        mb_scr[0, h] = jnp.full((KB, QB), far_ref[h], F32)
    posq = posq_ref[...]
    pmin_q = pmin_ref[b, qi]
    scale = A_HEAD_DIM ** -0.5
    half = N_BUCKETS // 2
    max_exact = half // 2

    def att_body(kb, c):
        s0 = pl.multiple_of(kb * KB, KB)
        madd = jnp.where(keys_scr[kb] >= thr, 0.0, NEG)
        far = (pmin_q - pmax_ref[b, kb]) >= MAX_DISTANCE

        @pl.when(jnp.logical_not(far))
        def _():
            pk = posk_ref[pl.ds(s0, KB), :]
            rel = jnp.concatenate([pk] * (QB // LANES), axis=1) - posq
            n = jnp.abs(rel)
            nf = jnp.maximum(n, 1).astype(F32)
            large = max_exact + (jnp.log(nf / max_exact) / math.log(MAX_DISTANCE / max_exact)
                                 * (half - max_exact)).astype(jnp.int32)
            large = jnp.minimum(large, half - 1)
            bucket = jnp.where(rel > 0, half, 0) + jnp.where(n < max_exact, n, large)
            for h in range(A_HEADS):
                row = jnp.broadcast_to(tab_ref[h:h + 1, :], (KB, LANES))
                mb_scr[1, h] = jnp.concatenate(
                    [jnp.take_along_axis(row, bucket[:, g * LANES:(g + 1) * LANES], axis=1)
                     for g in range(QB // LANES)], axis=1)

        slot = jnp.where(far, 0, 1)
        for h in range(A_HEADS):
            hs = slice(h * A_HEAD_DIM, (h + 1) * A_HEAD_DIM)
            s_scr[h] = lax.dot_general(k_ref[pl.ds(s0, KB), hs], q_ref[:, hs], NT_DIMS,
                                       preferred_element_type=F32)
        for h in range(A_HEADS):
            s = s_scr[h] * scale + mb_scr[slot, h] + madd
            m_old = m_scr[h]
            m_new = jnp.maximum(m_old, jnp.max(s, axis=0, keepdims=True))
            alpha = jnp.exp(m_old - m_new)
            p = jnp.exp(s - m_new)
            l_scr[h] = alpha * l_scr[h] + jnp.sum(p, axis=0, keepdims=True)
            m_scr[h] = m_new
            alpha_scr[h] = alpha
            p_scr[h] = p.astype(BF16)
        for h in range(A_HEADS):
            hs = slice(h * A_HEAD_DIM, (h + 1) * A_HEAD_DIM)
            pv = jnp.dot(vt_ref[kb, hs, :], p_scr[h], preferred_element_type=F32)
            acc_scr[h] = alpha_scr[h] * acc_scr[h] + pv
        return c

    lax.fori_loop(0, nkb, att_body, 0)

    for h in range(A_HEADS):
        hs = slice(h * A_HEAD_DIM, (h + 1) * A_HEAD_DIM)
        z = az_ref[:, hs]
        out_t = acc_scr[h] / l_scr[h]
        o_ref[:, hs] = (out_t.T * (z * jax.nn.sigmoid(z))).astype(o_ref.dtype)


def _dsa(qkv, iq, ikd, iw, az, positions, rel_bias):
    qkv3 = qkv.reshape(BATCH, SEQ, 3 * A_WIDTH)
    iq3 = iq.reshape(BATCH, SEQ, A_WIDTH)
    ikd3 = ikd.reshape(BATCH, SEQ, LANES)
    vt = qkv3[:, :, 2 * A_WIDTH:].reshape(BATCH, NKB, KB, A_WIDTH).swapaxes(2, 3)
    iwt = iw.reshape(BATCH, SEQ, LANES)[:, :, :IDX_HEADS].swapaxes(1, 2)
    az3 = az.reshape(BATCH, SEQ, A_WIDTH)
    posq = positions.reshape(BATCH, 1, SEQ)
    posk = jnp.broadcast_to(positions[:, :, None], (BATCH, SEQ, LANES))
    pblk = positions.reshape(BATCH, NKB, KB)
    pmin = jnp.min(pblk, axis=-1)
    pmax = jnp.max(pblk, axis=-1)
    tab = jnp.zeros((A_HEADS, LANES), F32).at[:, :N_BUCKETS].set(rel_bias.astype(F32).T)
    far = rel_bias[N_BUCKETS // 2 - 1, :].astype(F32)

    grid_spec = pltpu.PrefetchScalarGridSpec(
        num_scalar_prefetch=2,
        grid=(BATCH, NQB),
        in_specs=[
            pl.BlockSpec((None, QB, A_WIDTH), lambda b, i, *_: (b, i, 0)),
            pl.BlockSpec((None, SEQ, A_WIDTH), lambda b, i, *_: (b, 0, 1),
                         pipeline_mode=pl.Buffered(1)),
            pl.BlockSpec((None, NKB, A_WIDTH, KB), lambda b, i, *_: (b, 0, 0, 0),
                         pipeline_mode=pl.Buffered(1)),
            pl.BlockSpec((None, QB, A_WIDTH), lambda b, i, *_: (b, i, 0)),
            pl.BlockSpec((None, SEQ, LANES), lambda b, i, *_: (b, 0, 0)),
            pl.BlockSpec((None, IDX_HEADS, QB), lambda b, i, *_: (b, 0, i)),
            pl.BlockSpec((None, 1, QB), lambda b, i, *_: (b, 0, i)),
            pl.BlockSpec((None, SEQ, LANES), lambda b, i, *_: (b, 0, 0)),
            pl.BlockSpec((None, QB, A_WIDTH), lambda b, i, *_: (b, i, 0)),
            pl.BlockSpec((A_HEADS, LANES), lambda b, i, *_: (0, 0)),
            pl.BlockSpec(memory_space=pltpu.SMEM),
        ],
        out_specs=pl.BlockSpec((None, QB, A_WIDTH), lambda b, i, *_: (b, i, 0)),
        scratch_shapes=[
            pltpu.VMEM((NKB, KB, QB), jnp.int32),
            pltpu.VMEM((IDX_HEADS, QB, LANES), BF16),
            pltpu.VMEM((2, A_HEADS, KB, QB), F32),
            pltpu.VMEM((A_HEADS, A_HEAD_DIM, QB), F32),
            pltpu.VMEM((A_HEADS, 1, QB), F32),
            pltpu.VMEM((A_HEADS, 1, QB), F32),
            pltpu.VMEM((A_HEADS, 1, QB), F32),
            pltpu.VMEM((A_HEADS, KB, QB), F32),
            pltpu.VMEM((A_HEADS, KB, QB), BF16),
        ],
    )
    out = pl.pallas_call(
        _dsa_kernel,
        grid_spec=grid_spec,
        out_shape=jax.ShapeDtypeStruct((BATCH, SEQ, A_WIDTH), BF16),
        compiler_params=pltpu.CompilerParams(dimension_semantics=("arbitrary", "arbitrary"),
                                             vmem_limit_bytes=VMEM_LIMIT),
        name="dsa",
    )(pmin, pmax, qkv3, qkv3, vt, iq3, ikd3, iwt, posq, posk, az3, tab, far)
    return out.reshape(TOKENS, A_WIDTH)


def _ret_kernel(cdec_ref, q_ref, k_ref, v_ref, z_ref, gain_ref, dec_ref, te_ref, fs_ref, o_ref,
                state_scr):
    @pl.when(pl.program_id(1) == 0)
    def _():
        state_scr[...] = jnp.zeros(state_scr.shape, F32)

    def chunk_body(c, carry):
        r0 = pl.multiple_of(c * CHUNK, CHUNK)
        for h in range(R_HEADS):
            hs = slice(h * R_KEY_DIM, (h + 1) * R_KEY_DIM)
            q = q_ref[pl.ds(r0, CHUNK), hs]
            k = k_ref[pl.ds(r0, CHUNK), hs]
            v = v_ref[pl.ds(r0, CHUNK), hs]
            sc = lax.dot_general(q, k, NT_DIMS, preferred_element_type=F32) * dec_ref[h]
            intra = jnp.dot(sc.astype(BF16), v, preferred_element_type=F32)
            state = state_scr[h]
            qs = (q.astype(F32) * fs_ref[h]).astype(BF16)
            cross = jnp.dot(qs, state.astype(BF16), preferred_element_type=F32)
            ke = (k.astype(F32) * te_ref[h]).T.astype(BF16)
            kv = jnp.dot(ke, v, preferred_element_type=F32)
            state_scr[h] = state * cdec_ref[h] + kv
            y = intra + cross
            mu = jnp.mean(y, axis=-1, keepdims=True)
            var = jnp.mean((y - mu) ** 2, axis=-1, keepdims=True)
            yn = (y - mu) * lax.rsqrt(var + EPS) * gain_ref[:, hs]
            z = z_ref[pl.ds(r0, CHUNK), hs]
            o_ref[pl.ds(r0, CHUNK), hs] = (yn * (z * jax.nn.sigmoid(z))).astype(o_ref.dtype)
        return carry

    lax.fori_loop(0, RB // CHUNK, chunk_body, 0)


def _retention(rqk, rv, gates, gn_gain):
    log_g = jnp.log(1.0 - 2.0 ** (-5.0 - jnp.arange(R_HEADS, dtype=F32)))
    pos = jnp.arange(CHUNK, dtype=F32)
    dist = jnp.abs(pos[:, None] - pos[None, :])
    intra_decay = jnp.exp(log_g[:, None, None] * dist)
    to_end = jnp.exp(log_g[:, None] * (CHUNK - 1.0 - pos)[None, :])
    from_start = jnp.exp(log_g[:, None] * (pos + 1.0)[None, :])
    chunk_decay = jnp.exp(log_g * CHUNK)
    te = jnp.broadcast_to(to_end[:, :, None], (R_HEADS, CHUNK, R_KEY_DIM))
    fs = jnp.broadcast_to(from_start[:, :, None], (R_HEADS, CHUNK, R_KEY_DIM))

    rqk3 = rqk.reshape(BATCH, SEQ, 2 * R_WIDTH)
    rv3 = rv.reshape(BATCH, SEQ, R_WIDTH)
    g3 = gates.reshape(BATCH, SEQ, gates.shape[-1])
    out = pl.pallas_call(
        _ret_kernel,
        grid=(BATCH, SEQ // RB),
        in_specs=[
            pl.BlockSpec(memory_space=pltpu.SMEM),
            pl.BlockSpec((None, RB, R_WIDTH), lambda b, i: (b, i, 0)),
            pl.BlockSpec((None, RB, R_WIDTH), lambda b, i: (b, i, 1)),
            pl.BlockSpec((None, RB, R_WIDTH), lambda b, i: (b, i, 0)),
            pl.BlockSpec((None, RB, R_WIDTH), lambda b, i: (b, i, 0)),
            pl.BlockSpec((1, R_WIDTH), lambda b, i: (0, 0)),
            pl.BlockSpec((R_HEADS, CHUNK, CHUNK), lambda b, i: (0, 0, 0)),
            pl.BlockSpec((R_HEADS, CHUNK, R_KEY_DIM), lambda b, i: (0, 0, 0)),
            pl.BlockSpec((R_HEADS, CHUNK, R_KEY_DIM), lambda b, i: (0, 0, 0)),
        ],
        out_specs=pl.BlockSpec((None, RB, R_WIDTH), lambda b, i: (b, i, 0)),
        out_shape=jax.ShapeDtypeStruct((BATCH, SEQ, R_WIDTH), BF16),
        scratch_shapes=[pltpu.VMEM((R_HEADS, R_KEY_DIM, R_VAL_DIM), F32)],
        compiler_params=pltpu.CompilerParams(dimension_semantics=("arbitrary", "arbitrary"),
                                             vmem_limit_bytes=VMEM_LIMIT),
        name="retention",
    )(chunk_decay, rqk3, rqk3, rv3, g3, gn_gain.reshape(1, R_WIDTH), intra_decay, te, fs)
    return out.reshape(TOKENS, R_WIDTH)


def _merge_kernel(a_ref, wa_ref, b_ref, wb_ref, ga_ref, gb_ref, o_ref):
    ta = jnp.dot(a_ref[...], wa_ref[...], preferred_element_type=F32)
    tb = jnp.dot(b_ref[...], wb_ref[...], preferred_element_type=F32)
    o_ref[...] = (jax.nn.sigmoid(ga_ref[...]) * ta + jax.nn.sigmoid(gb_ref[...]) * tb).astype(o_ref.dtype)


def _merge(a_out, wa, b_out, wb, gates, tm=1024, tn=512):
    ga0 = R_WIDTH // tn
    gb0 = (R_WIDTH + D_MODEL) // tn
    return pl.pallas_call(
        _merge_kernel,
        grid=(TOKENS // tm, D_MODEL // tn),
        in_specs=[pl.BlockSpec((tm, A_WIDTH), lambda i, j: (i, 0)),
                  pl.BlockSpec((A_WIDTH, tn), lambda i, j: (0, j)),
                  pl.BlockSpec((tm, R_WIDTH), lambda i, j: (i, 0)),
                  pl.BlockSpec((R_WIDTH, tn), lambda i, j: (0, j)),
                  pl.BlockSpec((tm, tn), lambda i, j: (i, j + ga0)),
                  pl.BlockSpec((tm, tn), lambda i, j: (i, j + gb0))],
        out_specs=pl.BlockSpec((tm, tn), lambda i, j: (i, j)),
        out_shape=jax.ShapeDtypeStruct((TOKENS, D_MODEL), BF16),
        compiler_params=pltpu.CompilerParams(dimension_semantics=("arbitrary", "arbitrary"),
                                             vmem_limit_bytes=VMEM_LIMIT),
        name="merge",
    )(a_out, wa, b_out, wb, gates, gates)


def _resid_kernel(m_ref, w_ref, x_ref, o_ref):
    o_ref[...] = x_ref[...] + jnp.dot(m_ref[...], w_ref[...], preferred_element_type=F32)


def _resid(merged, w_o, x2d, tm=1024, tn=512):
    return pl.pallas_call(
        _resid_kernel,
        grid=(TOKENS // tm, D_MODEL // tn),
        in_specs=[pl.BlockSpec((tm, D_MODEL), lambda i, j: (i, 0)),
                  pl.BlockSpec((D_MODEL, tn), lambda i, j: (0, j)),
                  pl.BlockSpec((tm, tn), lambda i, j: (i, j))],
        out_specs=pl.BlockSpec((tm, tn), lambda i, j: (i, j)),
        out_shape=jax.ShapeDtypeStruct((TOKENS, D_MODEL), F32),
        compiler_params=pltpu.CompilerParams(dimension_semantics=("arbitrary", "arbitrary"),
                                             vmem_limit_bytes=VMEM_LIMIT),
        name="resid",
    )(merged, w_o, x2d)


def _final_kernel(r_ref, p_ref, wp_ref, wg_ref, fg_ref, o_ref):
    r = r_ref[...]
    u = jnp.dot(p_ref[...].astype(BF16), wp_ref[...], preferred_element_type=F32)
    g = jnp.dot(r.astype(BF16), wg_ref[...], preferred_element_type=F32)
    y = r + u * jax.nn.sigmoid(g)
    ms = jnp.mean(y * y, axis=-1, keepdims=True)
    o_ref[...] = y * lax.rsqrt(ms + EPS) * fg_ref[...]


def _final(r, p2d, w_ple, w_gate, final_gain, tm=256):
    return pl.pallas_call(
        _final_kernel,
        grid=(TOKENS // tm,),
        in_specs=[pl.BlockSpec((tm, D_MODEL), lambda i: (i, 0)),
                  pl.BlockSpec((tm, PLE_DIM), lambda i: (i, 0)),
                  pl.BlockSpec((PLE_DIM, D_MODEL), lambda i: (0, 0), pipeline_mode=pl.Buffered(1)),
                  pl.BlockSpec((D_MODEL, D_MODEL), lambda i: (0, 0), pipeline_mode=pl.Buffered(1)),
                  pl.BlockSpec((1, D_MODEL), lambda i: (0, 0))],
        out_specs=pl.BlockSpec((tm, D_MODEL), lambda i: (i, 0)),
        out_shape=jax.ShapeDtypeStruct((TOKENS, D_MODEL), F32),
        compiler_params=pltpu.CompilerParams(dimension_semantics=("arbitrary",),
                                             vmem_limit_bytes=VMEM_LIMIT),
        name="final",
    )(r, p2d, w_ple, w_gate, final_gain.reshape(1, D_MODEL))


def _layer(x2d, p2d, positions, w_in, norm_gain, w_a_out, w_b_out, w_o, ret_gn_gain, w_ple,
           w_ple_gate, rel_bias):
    w_head = w_in[:, :COL_IK].astype(BF16)
    w_ik = w_in[:, COL_IK:COL_IW]
    w_iw = w_in[:, COL_IW:COL_R]
    w_idx = jnp.concatenate(
        [w_ik, w_ik, w_iw, jnp.zeros((D_MODEL, LANES - IDX_HEADS), w_in.dtype)], axis=1).astype(BF16)
    w_rest = w_in[:, COL_R:].astype(BF16)

    half = R_KEY_DIM // 2
    inv_freq = ROPE_BASE ** (-jnp.arange(half, dtype=F32) / half)
    ang = positions[:, :, None].astype(F32) * inv_freq
    cos = jnp.cos(ang)
    sin = jnp.sin(ang)
    cos2 = jnp.concatenate([cos, cos], axis=-1).reshape(TOKENS, R_KEY_DIM)
    sin2 = jnp.concatenate([-sin, sin], axis=-1).reshape(TOKENS, R_KEY_DIM)

    h = _rmsnorm(x2d, norm_gain)
    qkv = _proj(h, w_head, 0, 3 * A_WIDTH, BF16, name="proj_qkv")
    az = _proj(h, w_head, 3 * A_WIDTH, A_WIDTH, F32, name="proj_az")
    iq = _proj(h, w_head, COL_IQ, IDX_HEADS * IDX_DIM, BF16, scale=IDX_DIM ** -0.5, name="proj_iq")
    ikd, iw = _proj_idx(h, w_idx)
    rqk = _proj_rope(h, w_rest, 0, cos2, sin2)
    rv = _proj(h, w_rest, 2 * R_WIDTH, R_WIDTH, BF16, name="proj_rv")
    gates = _proj(h, w_rest, 3 * R_WIDTH, R_WIDTH + 2 * D_MODEL, F32, name="proj_gates")

    a_out = _dsa(qkv, iq, ikd, iw, az, positions, rel_bias)
    b_out = _retention(rqk, rv, gates, ret_gn_gain)

    merged = _merge(a_out, w_a_out.astype(BF16), b_out, w_b_out.astype(BF16), gates)
    r = _resid(merged, w_o.astype(BF16), x2d)
    return r, (p2d, w_ple.astype(BF16), w_ple_gate.astype(BF16))


def kernel(x, p, positions, w_in, norm_gain, w_a_out, w_b_out, w_o, ret_gn_gain, w_ple, w_ple_gate,
           rel_bias, final_gain):
    assert x.shape == (BATCH, SEQ, D_MODEL) and w_in.shape[0] == 1
    x2d = x.reshape(TOKENS, D_MODEL)
    r, (p2d, wp, wg) = _layer(x2d, p[0].reshape(TOKENS, PLE_DIM), positions, w_in[0], norm_gain[0],
                              w_a_out[0], w_b_out[0], w_o[0], ret_gn_gain[0], w_ple[0],
                              w_ple_gate[0], rel_bias)
    out = _final(r, p2d, wp, wg, final_gain)
    return out.reshape(BATCH, SEQ, D_MODEL)
```

```python
import functools
import math

import jax
import jax.numpy as jnp
from jax import lax
from jax.experimental import pallas as pl
from jax.experimental.pallas import tpu as pltpu

D_MODEL = 2048
BATCH = 4
SEQ = 4096
TOKENS = BATCH * SEQ
CHUNK = 64
PLE_DIM = 256
EPS = 1e-6
A_HEADS = 8
A_HEAD_DIM = 128
A_WIDTH = A_HEADS * A_HEAD_DIM
IDX_HEADS = 16
IDX_DIM = 64
TOPK = min(256, SEQ // 4)
R_HEADS = 8
R_KEY_DIM = 128
R_VAL_DIM = 128
R_WIDTH = R_HEADS * R_VAL_DIM
ROPE_BASE = 10000.0
N_BUCKETS = 32
MAX_DISTANCE = 128

COL_IQ = 4 * A_WIDTH
COL_IK = COL_IQ + IDX_HEADS * IDX_DIM
COL_IW = COL_IK + IDX_DIM
COL_RQ = COL_IW + IDX_HEADS
COL_RV = COL_RQ + 2 * R_WIDTH
IN_WIDTH = COL_RV + 2 * R_WIDTH + 2 * D_MODEL

PJ_Q, PJ_K, PJ_V, PJ_AZ, PJ_RV, PJ_RZ, PJ_GA, PJ_GB, PJ_IQ = 0, 1, 2, 3, 4, 5, 6, 8, 10
PJ_BLOCK = 1024
PJ_WIDTH = 11 * PJ_BLOCK

LANES = 128
QB = 256
KB = 256
NQB = SEQ // QB
NKB = SEQ // KB
RB = 512
NEG = -1e30
INT_MIN = -(2 ** 31)
VMEM_LIMIT = 56 * 1024 * 1024

F32 = jnp.float32
BF16 = jnp.bfloat16
NT_DIMS = (((1,), (1,)), ((), ()))


def _params(n_axes):
    return pltpu.CompilerParams(dimension_semantics=("arbitrary",) * n_axes,
                                vmem_limit_bytes=VMEM_LIMIT)


def _rmsnorm_kernel(x_ref, g_ref, o_ref):
    x = x_ref[...]
    ms = jnp.mean(x * x, axis=-1, keepdims=True)
    o_ref[...] = (x * lax.rsqrt(ms + EPS) * g_ref[...]).astype(o_ref.dtype)


def _rmsnorm(x2d, gain, tm=512):
    return pl.pallas_call(
        _rmsnorm_kernel,
        grid=(TOKENS // tm,),
        in_specs=[pl.BlockSpec((tm, D_MODEL), lambda i: (i, 0)),
                  pl.BlockSpec((1, D_MODEL), lambda i: (0, 0))],
        out_specs=pl.BlockSpec((tm, D_MODEL), lambda i: (i, 0)),
        out_shape=jax.ShapeDtypeStruct((TOKENS, D_MODEL), BF16),
        compiler_params=_params(1),
        name="rmsnorm",
    )(x2d, gain.reshape(1, D_MODEL))


def _proj_kernel(h_ref, w_ref, o_ref):
    o_ref[...] = jnp.dot(h_ref[...], w_ref[...], preferred_element_type=F32).astype(o_ref.dtype)


def _proj(h, w, tm=2048, tn=512):
    ncols = w.shape[1]
    return pl.pallas_call(
        _proj_kernel,
        grid=(TOKENS // tm, ncols // tn),
        in_specs=[pl.BlockSpec((tm, D_MODEL), lambda i, j: (i, 0)),
                  pl.BlockSpec((D_MODEL, tn), lambda i, j: (0, j))],
        out_specs=pl.BlockSpec((tm, tn), lambda i, j: (i, j)),
        out_shape=jax.ShapeDtypeStruct((TOKENS, ncols), BF16),
        compiler_params=_params(2),
        name="proj",
    )(h, w)


def _proj_idx_kernel(h_ref, w_ref, ik_ref, iw_ref):
    acc = jnp.dot(h_ref[...], w_ref[...], preferred_element_type=F32)
    ik = acc[:, :LANES]
    mu = jnp.mean(ik, axis=-1, keepdims=True)
    var = jnp.mean((ik - mu) ** 2, axis=-1, keepdims=True)
    ik_ref[...] = ((ik - mu) * lax.rsqrt(var + EPS)).astype(ik_ref.dtype)
    iw_ref[...] = acc[:, LANES:]


def _proj_idx(h, w_idx, tm=2048):
    return pl.pallas_call(
        _proj_idx_kernel,
        grid=(TOKENS // tm,),
        in_specs=[pl.BlockSpec((tm, D_MODEL), lambda i: (i, 0)),
                  pl.BlockSpec((D_MODEL, 2 * LANES), lambda i: (0, 0))],
        out_specs=[pl.BlockSpec((tm, LANES), lambda i: (i, 0)),
                   pl.BlockSpec((tm, LANES), lambda i: (i, 0))],
        out_shape=[jax.ShapeDtypeStruct((TOKENS, LANES), BF16),
                   jax.ShapeDtypeStruct((TOKENS, LANES), F32)],
        compiler_params=_params(1),
        name="proj_idx",
    )(h, w_idx)


def _proj_rope_kernel(h_ref, w_ref, cos_ref, sin_ref, o_ref, *, tn, k_tile0, k_scale):
    acc = jnp.dot(h_ref[...], w_ref[...], preferred_element_type=F32)
    cos = cos_ref[...]
    sin = sin_ref[...]
    scale = jnp.where(pl.program_id(1) >= k_tile0, k_scale, 1.0).astype(F32)
    for g in range(tn // LANES):
        xg = acc[:, g * LANES:(g + 1) * LANES]
        rot = xg * cos + pltpu.roll(xg, LANES // 2, 1) * sin
        o_ref[:, g * LANES:(g + 1) * LANES] = (rot * scale).astype(o_ref.dtype)


def _proj_rope(h, w, cos2, sin2, tm=2048, tn=512):
    ncols = 2 * R_WIDTH
    kern = functools.partial(_proj_rope_kernel, tn=tn, k_tile0=R_WIDTH // tn,
                             k_scale=R_KEY_DIM ** -0.5)
    return pl.pallas_call(
        kern,
        grid=(TOKENS // tm, ncols // tn),
        in_specs=[pl.BlockSpec((tm, D_MODEL), lambda i, j: (i, 0)),
                  pl.BlockSpec((D_MODEL, tn), lambda i, j: (0, j)),
                  pl.BlockSpec((tm, LANES), lambda i, j: (i, 0)),
                  pl.BlockSpec((tm, LANES), lambda i, j: (i, 0))],
        out_specs=pl.BlockSpec((tm, tn), lambda i, j: (i, j)),
        out_shape=jax.ShapeDtypeStruct((TOKENS, ncols), BF16),
        compiler_params=_params(2),
        name="proj_rope",
    )(h, w, cos2, sin2)


def _to_key(a):
    bits = pltpu.bitcast(a, jnp.int32)
    return jnp.where(bits < 0, bits ^ jnp.int32(0x7FFFFFFF), bits)


def _dsa_kernel(pmin_ref, pmax_ref,
                q_ref, k_ref, vt_ref, iq_ref, ikd_ref, iwt_ref, posq_ref, posk_ref, az_ref,
                tab_ref, far_ref, o_ref,
                keys_scr, iqm_scr, mb_scr, acc_scr, m_scr, l_scr, alpha_scr, s_scr, p_scr):
    b = pl.program_id(0)
    qi = pl.program_id(1)
    nkb = qi + 1

    lane = lax.broadcasted_iota(jnp.int32, (QB, LANES), 1)
    for p in range(IDX_HEADS // 2):
        pair = iq_ref[:, p * LANES:(p + 1) * LANES].astype(F32)
        iqm_scr[2 * p] = jnp.where(lane < IDX_DIM, pair, 0.0).astype(BF16)
        iqm_scr[2 * p + 1] = jnp.where(lane >= IDX_DIM, pair, 0.0).astype(BF16)
    iwt = iwt_ref[...]

    def idx_tile(kb):
        s0 = pl.multiple_of(kb * KB, KB)
        kid = ikd_ref[pl.ds(s0, KB), :]
        acc = jnp.zeros((KB, QB), F32)
        for h in range(IDX_HEADS):
            sc = lax.dot_general(kid, iqm_scr[h], NT_DIMS, preferred_element_type=F32)
            acc = acc + jnp.maximum(sc, 0.0) * iwt[h:h + 1, :]
        return _to_key(acc)

    def idx_body(kb, c):
        keys_scr[kb] = idx_tile(kb)
        return c

    lax.fori_loop(0, qi, idx_body, 0)
    rr = lax.broadcasted_iota(jnp.int32, (KB, QB), 0)
    cc = lax.broadcasted_iota(jnp.int32, (KB, QB), 1)
    keys_scr[qi] = jnp.where((rr // CHUNK) <= (cc // CHUNK), idx_tile(qi), jnp.int32(INT_MIN))

    def count_where(pred_fn):
        def body(kb, acc):
            part = jnp.where(pred_fn(kb, keys_scr[kb]), 1.0, 0.0)
            return acc + jnp.sum(part.reshape(KB // 8, 8, QB), axis=0)
        acc = lax.fori_loop(0, nkb, body, jnp.zeros((8, QB), F32))
        return jnp.sum(acc, axis=0, keepdims=True)

    def bisect_body(it, ans):
        bit = jnp.left_shift(jnp.int32(1), 31 - it)
        cand = ans | bit
        cand_s = cand ^ jnp.int32(INT_MIN)
        cnt = count_where(lambda kb, kk: kk >= cand_s)
        return jnp.where(cnt >= TOPK, cand, ans)

    ans0 = jnp.where(qi > 0, 0, 1) * jnp.ones((1, QB), jnp.int32)
    ans = lax.fori_loop(0, jnp.where(qi > 0, 32, 0), bisect_body, ans0)
    thr = ans ^ jnp.int32(INT_MIN)

    cnt_ge = count_where(lambda kb, kk: kk >= thr)
    has_tie = jnp.logical_and(qi > 0, jnp.max(cnt_ge) > TOPK)

    @pl.when(has_tie)
    def _():
        cnt_gt = count_where(lambda kb, kk: kk > thr)
        need = TOPK - cnt_gt

        def key_index(kb):
            return kb * KB + rr

        def jb_body(it, j0):
            cand = j0 | jnp.left_shift(jnp.int32(1), 11 - it)
            f = count_where(lambda kb, kk: jnp.logical_and(kk == thr, key_index(kb) < cand))
            return jnp.where(f < need, cand, j0)

        j0 = lax.fori_loop(0, 12, jb_body, jnp.zeros((1, QB), jnp.int32))
        jstar = j0 + 1

        def fix_body(kb, c):
            kk = keys_scr[kb]
            drop = jnp.logical_and(kk == thr, key_index(kb) >= jstar)
            keys_scr[kb] = jnp.where(drop, kk - 1, kk)
            return c

        lax.fori_loop(0, nkb, fix_body, 0)

    m_scr[...] = jnp.full(m_scr.shape, NEG, F32)
    l_scr[...] = jnp.zeros(l_scr.shape, F32)
    acc_scr[...] = jnp.zeros(acc_scr.shape, F32)
    for h in range(A_HEADS):
        mb_scr[0, h] = jnp.full((KB, QB), far_ref[h], F32)
    posq = posq_ref[...]
    pmin_q = pmin_ref[b, qi]
    scale = A_HEAD_DIM ** -0.5
    half = N_BUCKETS // 2
    max_exact = half // 2

    def att_body(kb, c):
        s0 = pl.multiple_of(kb * KB, KB)
        madd = jnp.where(keys_scr[kb] >= thr, 0.0, NEG)
        far = (pmin_q - pmax_ref[b, kb]) >= MAX_DISTANCE

        @pl.when(jnp.logical_not(far))
        def _():
            pk = posk_ref[pl.ds(s0, KB), :]
            rel = jnp.concatenate([pk] * (QB // LANES), axis=1) - posq
            n = jnp.abs(rel)
            nf = jnp.maximum(n, 1).astype(F32)
            large = max_exact + (jnp.log(nf / max_exact) / math.log(MAX_DISTANCE / max_exact)
                                 * (half - max_exact)).astype(jnp.int32)
            large = jnp.minimum(large, half - 1)
            bucket = jnp.where(rel > 0, half, 0) + jnp.where(n < max_exact, n, large)
            for h in range(A_HEADS):
                row = jnp.broadcast_to(tab_ref[h:h + 1, :], (KB, LANES))
                mb_scr[1, h] = jnp.concatenate(
                    [jnp.take_along_axis(row, bucket[:, g * LANES:(g + 1) * LANES], axis=1)
                     for g in range(QB // LANES)], axis=1)

        slot = jnp.where(far, 0, 1)
        for h in range(A_HEADS):
            hs = slice(h * A_HEAD_DIM, (h + 1) * A_HEAD_DIM)
            s_scr[h] = lax.dot_general(k_ref[pl.ds(s0, KB), hs], q_ref[:, hs], NT_DIMS,
                                       preferred_element_type=F32)
        for h in range(A_HEADS):
            s = s_scr[h] * scale + mb_scr[slot, h] + madd
            m_old = m_scr[h]
            m_new = jnp.maximum(m_old, jnp.max(s, axis=0, keepdims=True))
            alpha = jnp.exp(m_old - m_new)
            p = jnp.exp(s - m_new)
            l_scr[h] = alpha * l_scr[h] + jnp.sum(p, axis=0, keepdims=True)
            m_scr[h] = m_new
            alpha_scr[h] = alpha
            p_scr[h] = p.astype(BF16)
        for h in range(A_HEADS):
            hs = slice(h * A_HEAD_DIM, (h + 1) * A_HEAD_DIM)
            pv = jnp.dot(vt_ref[kb, hs, :], p_scr[h], preferred_element_type=F32)
            acc_scr[h] = alpha_scr[h] * acc_scr[h] + pv
        return c

    lax.fori_loop(0, nkb, att_body, 0)

    for h in range(A_HEADS):
        hs = slice(h * A_HEAD_DIM, (h + 1) * A_HEAD_DIM)
        z = az_ref[:, hs].astype(F32)
        out_t = acc_scr[h] / l_scr[h]
        o_ref[:, hs] = (out_t.T * (z * jax.nn.sigmoid(z))).astype(o_ref.dtype)


def _dsa(pj, ikd, iw, positions, rel_bias):
    pj3 = pj.reshape(BATCH, SEQ, PJ_WIDTH)
    ikd3 = ikd.reshape(BATCH, SEQ, LANES)
    vt = pj3[:, :, PJ_V * PJ_BLOCK:(PJ_V + 1) * PJ_BLOCK]
    vt = vt.reshape(BATCH, NKB, KB, A_WIDTH).swapaxes(2, 3)
    iwt = iw.reshape(BATCH, SEQ, LANES)[:, :, :IDX_HEADS].swapaxes(1, 2)
    posq = positions.reshape(BATCH, 1, SEQ)
    posk = jnp.broadcast_to(positions[:, :, None], (BATCH, SEQ, LANES))
    pblk = positions.reshape(BATCH, NKB, KB)
    pmin = jnp.min(pblk, axis=-1)
    pmax = jnp.max(pblk, axis=-1)
    tab = jnp.zeros((A_HEADS, LANES), F32).at[:, :N_BUCKETS].set(rel_bias.astype(F32).T)
    far = rel_bias[N_BUCKETS // 2 - 1, :].astype(F32)

    grid_spec = pltpu.PrefetchScalarGridSpec(
        num_scalar_prefetch=2,
        grid=(BATCH, NQB),
        in_specs=[
            pl.BlockSpec((None, QB, A_WIDTH), lambda b, i, *_: (b, i, PJ_Q)),
            pl.BlockSpec((None, SEQ, A_WIDTH), lambda b, i, *_: (b, 0, PJ_K),
                         pipeline_mode=pl.Buffered(1)),
            pl.BlockSpec((None, NKB, A_WIDTH, KB), lambda b, i, *_: (b, 0, 0, 0),
                         pipeline_mode=pl.Buffered(1)),
            pl.BlockSpec((None, QB, A_WIDTH), lambda b, i, *_: (b, i, PJ_IQ)),
            pl.BlockSpec((None, SEQ, LANES), lambda b, i, *_: (b, 0, 0)),
            pl.BlockSpec((None, IDX_HEADS, QB), lambda b, i, *_: (b, 0, i)),
            pl.BlockSpec((None, 1, QB), lambda b, i, *_: (b, 0, i)),
            pl.BlockSpec((None, SEQ, LANES), lambda b, i, *_: (b, 0, 0)),
            pl.BlockSpec((None, QB, A_WIDTH), lambda b, i, *_: (b, i, PJ_AZ)),
            pl.BlockSpec((A_HEADS, LANES), lambda b, i, *_: (0, 0)),
            pl.BlockSpec(memory_space=pltpu.SMEM),
        ],
        out_specs=pl.BlockSpec((None, QB, A_WIDTH), lambda b, i, *_: (b, i, 0)),
        scratch_shapes=[
            pltpu.VMEM((NKB, KB, QB), jnp.int32),
            pltpu.VMEM((IDX_HEADS, QB, LANES), BF16),
            pltpu.VMEM((2, A_HEADS, KB, QB), F32),
            pltpu.VMEM((A_HEADS, A_HEAD_DIM, QB), F32),
            pltpu.VMEM((A_HEADS, 1, QB), F32),
            pltpu.VMEM((A_HEADS, 1, QB), F32),
            pltpu.VMEM((A_HEADS, 1, QB), F32),
            pltpu.VMEM((A_HEADS, KB, QB), F32),
            pltpu.VMEM((A_HEADS, KB, QB), BF16),
        ],
    )
    out = pl.pallas_call(
        _dsa_kernel,
        grid_spec=grid_spec,
        out_shape=jax.ShapeDtypeStruct((BATCH, SEQ, A_WIDTH), BF16),
        compiler_params=_params(2),
        name="dsa",
    )(pmin, pmax, pj3, pj3, vt, pj3, ikd3, iwt, posq, posk, pj3, tab, far)
    return out.reshape(TOKENS, A_WIDTH)


def _ret_kernel(cdec_ref, q_ref, k_ref, v_ref, z_ref, gain_ref, dec_ref, te_ref, fs_ref, o_ref,
                state_scr):
    @pl.when(pl.program_id(1) == 0)
    def _():
        state_scr[...] = jnp.zeros(state_scr.shape, F32)

    def chunk_body(c, carry):
        r0 = pl.multiple_of(c * CHUNK, CHUNK)
        for h in range(R_HEADS):
            hs = slice(h * R_KEY_DIM, (h + 1) * R_KEY_DIM)
            q = q_ref[pl.ds(r0, CHUNK), hs]
            k = k_ref[pl.ds(r0, CHUNK), hs]
            v = v_ref[pl.ds(r0, CHUNK), hs]
            sc = lax.dot_general(q, k, NT_DIMS, preferred_element_type=F32) * dec_ref[h]
            intra = jnp.dot(sc.astype(BF16), v, preferred_element_type=F32)
            state = state_scr[h]
            qs = (q.astype(F32) * fs_ref[h]).astype(BF16)
            cross = jnp.dot(qs, state.astype(BF16), preferred_element_type=F32)
            ke = (k.astype(F32) * te_ref[h]).T.astype(BF16)
            kv = jnp.dot(ke, v, preferred_element_type=F32)
            state_scr[h] = state * cdec_ref[h] + kv
            y = intra + cross
            mu = jnp.mean(y, axis=-1, keepdims=True)
            var = jnp.mean((y - mu) ** 2, axis=-1, keepdims=True)
            yn = (y - mu) * lax.rsqrt(var + EPS) * gain_ref[:, hs]
            z = z_ref[pl.ds(r0, CHUNK), hs].astype(F32)
            o_ref[pl.ds(r0, CHUNK), hs] = (yn * (z * jax.nn.sigmoid(z))).astype(o_ref.dtype)
        return carry

    lax.fori_loop(0, RB // CHUNK, chunk_body, 0)


def _retention(rqk, pj, gn_gain):
    log_g = jnp.log(1.0 - 2.0 ** (-5.0 - jnp.arange(R_HEADS, dtype=F32)))
    pos = jnp.arange(CHUNK, dtype=F32)
    dist = jnp.abs(pos[:, None] - pos[None, :])
    intra_decay = jnp.exp(log_g[:, None, None] * dist)
    to_end = jnp.exp(log_g[:, None] * (CHUNK - 1.0 - pos)[None, :])
    from_start = jnp.exp(log_g[:, None] * (pos + 1.0)[None, :])
    chunk_decay = jnp.exp(log_g * CHUNK)
    te = jnp.broadcast_to(to_end[:, :, None], (R_HEADS, CHUNK, R_KEY_DIM))
    fs = jnp.broadcast_to(from_start[:, :, None], (R_HEADS, CHUNK, R_KEY_DIM))

    rqk3 = rqk.reshape(BATCH, SEQ, 2 * R_WIDTH)
    pj3 = pj.reshape(BATCH, SEQ, PJ_WIDTH)
    out = pl.pallas_call(
        _ret_kernel,
        grid=(BATCH, SEQ // RB),
        in_specs=[
            pl.BlockSpec(memory_space=pltpu.SMEM),
            pl.BlockSpec((None, RB, R_WIDTH), lambda b, i: (b, i, 0)),
            pl.BlockSpec((None, RB, R_WIDTH), lambda b, i: (b, i, 1)),
            pl.BlockSpec((None, RB, R_WIDTH), lambda b, i: (b, i, PJ_RV)),
            pl.BlockSpec((None, RB, R_WIDTH), lambda b, i: (b, i, PJ_RZ)),
            pl.BlockSpec((1, R_WIDTH), lambda b, i: (0, 0)),
            pl.BlockSpec((R_HEADS, CHUNK, CHUNK), lambda b, i: (0, 0, 0)),
            pl.BlockSpec((R_HEADS, CHUNK, R_KEY_DIM), lambda b, i: (0, 0, 0)),
            pl.BlockSpec((R_HEADS, CHUNK, R_KEY_DIM), lambda b, i: (0, 0, 0)),
        ],
        out_specs=pl.BlockSpec((None, RB, R_WIDTH), lambda b, i: (b, i, 0)),
        out_shape=jax.ShapeDtypeStruct((BATCH, SEQ, R_WIDTH), BF16),
        scratch_shapes=[pltpu.VMEM((R_HEADS, R_KEY_DIM, R_VAL_DIM), F32)],
        compiler_params=_params(2),
        name="retention",
    )(chunk_decay, rqk3, rqk3, pj3, pj3, gn_gain.reshape(1, R_WIDTH), intra_decay, te, fs)
    return out.reshape(TOKENS, R_WIDTH)


def _out_kernel(a_ref, b_ref, ga_ref, gb_ref, x_ref, p_ref, wa_ref, wb_ref, wo_ref, wp_ref, wg_ref,
                fg_ref, o_ref):
    ta = jnp.dot(a_ref[...], wa_ref[...], preferred_element_type=F32)
    tb = jnp.dot(b_ref[...], wb_ref[...], preferred_element_type=F32)
    merged = (jax.nn.sigmoid(ga_ref[...].astype(F32)) * ta
              + jax.nn.sigmoid(gb_ref[...].astype(F32)) * tb)
    r = x_ref[...] + jnp.dot(merged.astype(BF16), wo_ref[...], preferred_element_type=F32)
    u = jnp.dot(p_ref[...].astype(BF16), wp_ref[...], preferred_element_type=F32)
    g = jnp.dot(r.astype(BF16), wg_ref[...], preferred_element_type=F32)
    y = r + u * jax.nn.sigmoid(g)
    ms = jnp.mean(y * y, axis=-1, keepdims=True)
    o_ref[...] = y * lax.rsqrt(ms + EPS) * fg_ref[...]


def _output(a_out, b_out, pj, x2d, p2d, wa, wb, wo, wp, wg, final_gain, tm=256):
    def resident(shape):
        return pl.BlockSpec(shape, lambda i: (0, 0), pipeline_mode=pl.Buffered(1))

    return pl.pallas_call(
        _out_kernel,
        grid=(TOKENS // tm,),
        in_specs=[pl.BlockSpec((tm, A_WIDTH), lambda i: (i, 0)),
                  pl.BlockSpec((tm, R_WIDTH), lambda i: (i, 0)),
                  pl.BlockSpec((tm, D_MODEL), lambda i: (i, PJ_GA * PJ_BLOCK // D_MODEL)),
                  pl.BlockSpec((tm, D_MODEL), lambda i: (i, PJ_GB * PJ_BLOCK // D_MODEL)),
                  pl.BlockSpec((tm, D_MODEL), lambda i: (i, 0)),
                  pl.BlockSpec((tm, PLE_DIM), lambda i: (i, 0)),
                  resident((A_WIDTH, D_MODEL)),
                  resident((R_WIDTH, D_MODEL)),
                  resident((D_MODEL, D_MODEL)),
                  resident((PLE_DIM, D_MODEL)),
                  resident((D_MODEL, D_MODEL)),
                  pl.BlockSpec((1, D_MODEL), lambda i: (0, 0))],
        out_specs=pl.BlockSpec((tm, D_MODEL), lambda i: (i, 0)),
        out_shape=jax.ShapeDtypeStruct((TOKENS, D_MODEL), F32),
        compiler_params=_params(1),
        name="output",
    )(a_out, b_out, pj, pj, x2d, p2d, wa, wb, wo, wp, wg, final_gain.reshape(1, D_MODEL))


def _prepare_weights(w_in):
    w_plain = jnp.concatenate(
        [w_in[:, :COL_IQ],
         w_in[:, COL_RV:],
         w_in[:, COL_IQ:COL_IK] * (IDX_DIM ** -0.5)],
        axis=1).astype(BF16)
    w_rope = w_in[:, COL_RQ:COL_RV].astype(BF16)
    w_ik = w_in[:, COL_IK:COL_IW]
    w_iw = w_in[:, COL_IW:COL_RQ] * (IDX_HEADS ** -0.5)
    w_idx = jnp.concatenate(
        [w_ik, w_ik, w_iw, jnp.zeros((D_MODEL, LANES - IDX_HEADS), w_in.dtype)], axis=1).astype(BF16)
    return w_plain, w_rope, w_idx


def _rope_tables(positions):
    half = R_KEY_DIM // 2
    inv_freq = ROPE_BASE ** (-jnp.arange(half, dtype=F32) / half)
    ang = positions[:, :, None].astype(F32) * inv_freq
    cos = jnp.cos(ang)
    sin = jnp.sin(ang)
    cos2 = jnp.concatenate([cos, cos], axis=-1).reshape(TOKENS, R_KEY_DIM)
    sin2 = jnp.concatenate([-sin, sin], axis=-1).reshape(TOKENS, R_KEY_DIM)
    return cos2, sin2


def kernel(x, p, positions, w_in, norm_gain, w_a_out, w_b_out, w_o, ret_gn_gain, w_ple, w_ple_gate,
           rel_bias, final_gain):
    assert x.shape == (BATCH, SEQ, D_MODEL) and w_in.shape == (1, D_MODEL, IN_WIDTH)
    x2d = x.reshape(TOKENS, D_MODEL)
    p2d = p[0].reshape(TOKENS, PLE_DIM)
    w_plain, w_rope, w_idx = _prepare_weights(w_in[0])
    cos2, sin2 = _rope_tables(positions)

    h = _rmsnorm(x2d, norm_gain[0])
    pj = _proj(h, w_plain)
    rqk = _proj_rope(h, w_rope, cos2, sin2)
    ikd, iw = _proj_idx(h, w_idx)

    a_out = _dsa(pj, ikd, iw, positions, rel_bias)
    b_out = _retention(rqk, pj, ret_gn_gain[0])

    out = _output(a_out, b_out, pj, x2d, p2d, w_a_out[0].astype(BF16), w_b_out[0].astype(BF16),
                  w_o[0].astype(BF16), w_ple[0].astype(BF16), w_ple_gate[0].astype(BF16), final_gain)
    return out.reshape(BATCH, SEQ, D_MODEL)
```

```python
import functools
import math

import jax
import jax.numpy as jnp
from jax import lax
from jax.experimental import pallas as pl
from jax.experimental.pallas import tpu as pltpu

D_MODEL = 2048
BATCH = 4
SEQ = 4096
TOKENS = BATCH * SEQ
CHUNK = 64
PLE_DIM = 256
EPS = 1e-6
A_HEADS = 8
A_HEAD_DIM = 128
A_WIDTH = A_HEADS * A_HEAD_DIM
IDX_HEADS = 16
IDX_DIM = 64
TOPK = min(256, SEQ // 4)
R_HEADS = 8
R_KEY_DIM = 128
R_VAL_DIM = 128
R_WIDTH = R_HEADS * R_VAL_DIM
ROPE_BASE = 10000.0
N_BUCKETS = 32
MAX_DISTANCE = 128

COL_IQ = 4 * A_WIDTH
COL_IK = COL_IQ + IDX_HEADS * IDX_DIM
COL_IW = COL_IK + IDX_DIM
COL_RQ = COL_IW + IDX_HEADS
COL_RV = COL_RQ + 2 * R_WIDTH
IN_WIDTH = COL_RV + 2 * R_WIDTH + 2 * D_MODEL

PJ_Q, PJ_K, PJ_V, PJ_AZ, PJ_RV, PJ_RZ, PJ_GA, PJ_GB, PJ_IQ = 0, 1, 2, 3, 4, 5, 6, 8, 10
PJ_BLOCK = 1024
PJ_WIDTH = 11 * PJ_BLOCK

LANES = 128
QB = 256
KB = 256
NQB = SEQ // QB
NKB = SEQ // KB
RB = 512
NEG = -1e30
INT_MIN = -(2 ** 31)
VMEM_LIMIT = 56 * 1024 * 1024

F32 = jnp.float32
BF16 = jnp.bfloat16
NT_DIMS = (((1,), (1,)), ((), ()))


def _params(n_axes):
    return pltpu.CompilerParams(dimension_semantics=("arbitrary",) * n_axes,
                                vmem_limit_bytes=VMEM_LIMIT)


def _rmsnorm_kernel(x_ref, g_ref, o_ref):
    x = x_ref[...]
    ms = jnp.mean(x * x, axis=-1, keepdims=True)
    o_ref[...] = (x * lax.rsqrt(ms + EPS) * g_ref[...]).astype(o_ref.dtype)


def _rmsnorm(x2d, gain, tm=512):
    return pl.pallas_call(
        _rmsnorm_kernel,
        grid=(TOKENS // tm,),
        in_specs=[pl.BlockSpec((tm, D_MODEL), lambda i: (i, 0)),
                  pl.BlockSpec((1, D_MODEL), lambda i: (0, 0))],
        out_specs=pl.BlockSpec((tm, D_MODEL), lambda i: (i, 0)),
        out_shape=jax.ShapeDtypeStruct((TOKENS, D_MODEL), BF16),
        compiler_params=_params(1),
        name="rmsnorm",
    )(x2d, gain.reshape(1, D_MODEL))


def _proj_kernel(h_ref, w_ref, o_ref):
    o_ref[...] = jnp.dot(h_ref[...], w_ref[...], preferred_element_type=F32).astype(o_ref.dtype)


def _proj(h, w, tm=2048, tn=512):
    ncols = w.shape[1]
    return pl.pallas_call(
        _proj_kernel,
        grid=(TOKENS // tm, ncols // tn),
        in_specs=[pl.BlockSpec((tm, D_MODEL), lambda i, j: (i, 0)),
                  pl.BlockSpec((D_MODEL, tn), lambda i, j: (0, j))],
        out_specs=pl.BlockSpec((tm, tn), lambda i, j: (i, j)),
        out_shape=jax.ShapeDtypeStruct((TOKENS, ncols), BF16),
        compiler_params=_params(2),
        name="proj",
    )(h, w)


def _proj_idx_kernel(h_ref, w_ref, ik_ref, iw_ref):
    acc = jnp.dot(h_ref[...], w_ref[...], preferred_element_type=F32)
    ik = acc[:, :LANES]
    mu = jnp.mean(ik, axis=-1, keepdims=True)
    var = jnp.mean((ik - mu) ** 2, axis=-1, keepdims=True)
    ik_ref[...] = ((ik - mu) * lax.rsqrt(var + EPS)).astype(ik_ref.dtype)
    iw_ref[...] = acc[:, LANES:]


def _proj_idx(h, w_idx, tm=2048):
    return pl.pallas_call(
        _proj_idx_kernel,
        grid=(TOKENS // tm,),
        in_specs=[pl.BlockSpec((tm, D_MODEL), lambda i: (i, 0)),
                  pl.BlockSpec((D_MODEL, 2 * LANES), lambda i: (0, 0))],
        out_specs=[pl.BlockSpec((tm, LANES), lambda i: (i, 0)),
                   pl.BlockSpec((tm, LANES), lambda i: (i, 0))],
        out_shape=[jax.ShapeDtypeStruct((TOKENS, LANES), BF16),
                   jax.ShapeDtypeStruct((TOKENS, LANES), F32)],
        compiler_params=_params(1),
        name="proj_idx",
    )(h, w_idx)


def _proj_rope_kernel(h_ref, w_ref, cos_ref, sin_ref, o_ref, *, tn, k_tile0, k_scale):
    acc = jnp.dot(h_ref[...], w_ref[...], preferred_element_type=F32)
    cos = cos_ref[...]
    sin = sin_ref[...]
    scale = jnp.where(pl.program_id(1) >= k_tile0, k_scale, 1.0).astype(F32)
    for g in range(tn // LANES):
        xg = acc[:, g * LANES:(g + 1) * LANES]
        rot = xg * cos + pltpu.roll(xg, LANES // 2, 1) * sin
        o_ref[:, g * LANES:(g + 1) * LANES] = (rot * scale).astype(o_ref.dtype)


def _proj_rope(h, w, cos2, sin2, tm=2048, tn=512):
    ncols = 2 * R_WIDTH
    kern = functools.partial(_proj_rope_kernel, tn=tn, k_tile0=R_WIDTH // tn,
                             k_scale=R_KEY_DIM ** -0.5)
    return pl.pallas_call(
        kern,
        grid=(TOKENS // tm, ncols // tn),
        in_specs=[pl.BlockSpec((tm, D_MODEL), lambda i, j: (i, 0)),
                  pl.BlockSpec((D_MODEL, tn), lambda i, j: (0, j)),
                  pl.BlockSpec((tm, LANES), lambda i, j: (i, 0)),
                  pl.BlockSpec((tm, LANES), lambda i, j: (i, 0))],
        out_specs=pl.BlockSpec((tm, tn), lambda i, j: (i, j)),
        out_shape=jax.ShapeDtypeStruct((TOKENS, ncols), BF16),
        compiler_params=_params(2),
        name="proj_rope",
    )(h, w, cos2, sin2)


def _to_key(a):
    bits = pltpu.bitcast(a, jnp.int32)
    return jnp.where(bits < 0, bits ^ jnp.int32(0x7FFFFFFF), bits)


def _dsa_kernel(pmin_ref, pmax_ref,
                q_ref, k_ref, vt_ref, iq_ref, ikd_ref, iwt_ref, posq_ref, posk_ref, az_ref,
                tab_ref, far_ref, o_ref,
                keys_scr, dig_scr, dm_scr, iqm_scr, acc_scr, m_scr, l_scr, alpha_scr, s_scr, p_scr):
    b = pl.program_id(0)
    qi = pl.program_id(1)
    nkb = qi + 1

    lane = lax.broadcasted_iota(jnp.int32, (QB, LANES), 1)
    for p in range(IDX_HEADS // 2):
        pair = iq_ref[:, p * LANES:(p + 1) * LANES].astype(F32)
        iqm_scr[2 * p] = jnp.where(lane < IDX_DIM, pair, 0.0).astype(BF16)
        iqm_scr[2 * p + 1] = jnp.where(lane >= IDX_DIM, pair, 0.0).astype(BF16)
    iwt = iwt_ref[...]

    def to_bf16(int_plane):
        return int_plane.astype(F32).astype(BF16)

    def idx_tile(kb, admissible=None):
        s0 = pl.multiple_of(kb * KB, KB)
        kid = ikd_ref[pl.ds(s0, KB), :]
        acc = jnp.zeros((KB, QB), F32)
        for h in range(IDX_HEADS):
            sc = lax.dot_general(kid, iqm_scr[h], NT_DIMS, preferred_element_type=F32)
            acc = acc + jnp.maximum(sc, 0.0) * iwt[h:h + 1, :]
        key = _to_key(acc)
        top = (key >> 24) + 128
        if admissible is not None:
            key = jnp.where(admissible, key, jnp.int32(INT_MIN))
            top = jnp.where(admissible, top, -1)
        keys_scr[kb] = key
        dig_scr[0, kb] = to_bf16(top)
        for d in range(1, 4):
            dig_scr[d, kb] = to_bf16((key >> (24 - 8 * d)) & 255)

    def idx_body(kb, c):
        idx_tile(kb)
        return c

    lax.fori_loop(0, qi, idx_body, 0)
    rr = lax.broadcasted_iota(jnp.int32, (KB, QB), 0)
    cc = lax.broadcasted_iota(jnp.int32, (KB, QB), 1)
    idx_tile(qi, (rr // CHUNK) <= (cc // CHUNK))

    nkb_sel = jnp.where(qi > 0, nkb, 0)
    one_b = jnp.ones((), BF16)
    zero_b = jnp.zeros((), BF16)

    def count_ge(plane_ref, cand):
        cand_b = cand.astype(BF16)

        def body(kb, acc):
            part = jnp.where(plane_ref[kb] >= cand_b, one_b, zero_b)
            slabs = [part[16 * g:16 * (g + 1)] for g in range(KB // 16)]
            while len(slabs) > 1:
                slabs = [slabs[i] + slabs[i + 1] for i in range(0, len(slabs), 2)]
            return acc + slabs[0]

        acc = lax.fori_loop(0, nkb_sel, body, jnp.zeros((16, QB), BF16))
        return jnp.sum(acc.astype(F32), axis=0, keepdims=True)

    def byte_level(plane_ref, need):
        def bit_body(it, carry):
            t, cnt_rejected = carry
            cand = t + jnp.left_shift(jnp.int32(1), 7 - it).astype(F32)
            cnt = count_ge(plane_ref, cand)
            ok = cnt >= need
            return jnp.where(ok, cand, t), jnp.where(ok, cnt_rejected, cnt)

        zeros = jnp.zeros((1, QB), F32)
        t, cnt_above = lax.fori_loop(0, jnp.where(qi > 0, 8, 0), bit_body, (zeros, zeros))
        return t, need - cnt_above

    def remask(src_ref, t, digit_ref):
        t_b = t.astype(BF16)

        def body(kb, c):
            dm_scr[kb] = jnp.where(src_ref[kb] == t_b, digit_ref[kb], -one_b)
            return c

        lax.fori_loop(0, nkb_sel, body, 0)

    need = jnp.full((1, QB), TOPK, F32)
    t1, need = byte_level(dig_scr.at[0], need)
    remask(dig_scr.at[0], t1, dig_scr.at[1])
    t2, need = byte_level(dm_scr, need)
    remask(dm_scr, t2, dig_scr.at[2])
    t3, need = byte_level(dm_scr, need)
    remask(dm_scr, t3, dig_scr.at[3])
    t4, _ = byte_level(dm_scr, need)
    thr_bytes = [t.astype(jnp.int32) for t in (t1, t2, t3, t4)]
    thr = (((thr_bytes[0] - 128) << 24) | (thr_bytes[1] << 16) | (thr_bytes[2] << 8) | thr_bytes[3])
    thr = jnp.where(qi > 0, thr, jnp.int32(INT_MIN + 1))

    def count_where(pred_fn):
        def body(kb, acc):
            part = jnp.where(pred_fn(kb, keys_scr[kb]), 1.0, 0.0)
            return acc + jnp.sum(part.reshape(KB // 8, 8, QB), axis=0)
        acc = lax.fori_loop(0, nkb, body, jnp.zeros((8, QB), F32))
        return jnp.sum(acc, axis=0, keepdims=True)

    cnt_last = count_ge(dm_scr, t4)
    has_tie = jnp.logical_and(qi > 0, jnp.max(cnt_last - need) > 0)

    @pl.when(has_tie)
    def _():
        cnt_gt = count_where(lambda kb, kk: kk > thr)
        need = TOPK - cnt_gt

        def key_index(kb):
            return kb * KB + rr

        def jb_body(it, j0):
            cand = j0 | jnp.left_shift(jnp.int32(1), 11 - it)
            f = count_where(lambda kb, kk: jnp.logical_and(kk == thr, key_index(kb) < cand))
            return jnp.where(f < need, cand, j0)

        j0 = lax.fori_loop(0, 12, jb_body, jnp.zeros((1, QB), jnp.int32))
        jstar = j0 + 1

        def fix_body(kb, c):
            kk = keys_scr[kb]
            drop = jnp.logical_and(kk == thr, key_index(kb) >= jstar)
            keys_scr[kb] = jnp.where(drop, kk - 1, kk)
            return c

        lax.fori_loop(0, nkb, fix_body, 0)

    m_scr[...] = jnp.full(m_scr.shape, NEG, F32)
    l_scr[...] = jnp.zeros(l_scr.shape, F32)
    acc_scr[...] = jnp.zeros(acc_scr.shape, F32)
    posq = posq_ref[...]
    pmin_q = pmin_ref[b, qi]
    log2e = math.log2(math.e)
    scale = A_HEAD_DIM ** -0.5 * log2e
    half = N_BUCKETS // 2
    max_exact = half // 2

    def att_tile(kb, near):
        s0 = pl.multiple_of(kb * KB, KB)
        madd = jnp.where(keys_scr[kb] >= thr, 0.0, NEG)
        for h in range(A_HEADS):
            hs = slice(h * A_HEAD_DIM, (h + 1) * A_HEAD_DIM)
            s_scr[h] = lax.dot_general(k_ref[pl.ds(s0, KB), hs], q_ref[:, hs], NT_DIMS,
                                       preferred_element_type=F32)
        if near:
            pk = posk_ref[pl.ds(s0, KB), :]
            rel = jnp.concatenate([pk] * (QB // LANES), axis=1) - posq
            n = jnp.abs(rel)
            nf = jnp.maximum(n, 1).astype(F32)
            large = max_exact + (jnp.log(nf / max_exact) / math.log(MAX_DISTANCE / max_exact)
                                 * (half - max_exact)).astype(jnp.int32)
            large = jnp.minimum(large, half - 1)
            bucket = jnp.where(rel > 0, half, 0) + jnp.where(n < max_exact, n, large)
        for h in range(A_HEADS):
            if near:
                row = jnp.broadcast_to(tab_ref[h:h + 1, :], (KB, LANES))
                bias = jnp.concatenate(
                    [jnp.take_along_axis(row, bucket[:, g * LANES:(g + 1) * LANES], axis=1)
                     for g in range(QB // LANES)], axis=1)
                s = s_scr[h] * scale + ((bias - far_ref[h]) * log2e + madd)
            else:
                s = s_scr[h] * scale + madd
            m_old = m_scr[h]
            m_new = jnp.maximum(m_old, jnp.max(s, axis=0, keepdims=True))
            alpha = jnp.exp2(m_old - m_new)
            p = jnp.exp2(s - m_new)
            l_scr[h] = alpha * l_scr[h] + jnp.sum(p, axis=0, keepdims=True)
            m_scr[h] = m_new
            alpha_scr[h] = alpha
            p_scr[h] = p.astype(BF16)
        for h in range(A_HEADS):
            hs = slice(h * A_HEAD_DIM, (h + 1) * A_HEAD_DIM)
            pv = jnp.dot(vt_ref[kb, hs, :], p_scr[h], preferred_element_type=F32)
            acc_scr[h] = alpha_scr[h] * acc_scr[h] + pv

    def att_body(kb, c):
        far = (pmin_q - pmax_ref[b, kb]) >= MAX_DISTANCE
        pl.when(far)(functools.partial(att_tile, kb, False))
        pl.when(jnp.logical_not(far))(functools.partial(att_tile, kb, True))
        return c

    lax.fori_loop(0, nkb, att_body, 0)

    for h in range(A_HEADS):
        hs = slice(h * A_HEAD_DIM, (h + 1) * A_HEAD_DIM)
        z = az_ref[:, hs].astype(F32)
        out_t = acc_scr[h] / l_scr[h]
        o_ref[:, hs] = (out_t.T * (z * jax.nn.sigmoid(z))).astype(o_ref.dtype)


def _dsa(pj, ikd, iw, positions, rel_bias):
    pj3 = pj.reshape(BATCH, SEQ, PJ_WIDTH)
    ikd3 = ikd.reshape(BATCH, SEQ, LANES)
    vt = pj3[:, :, PJ_V * PJ_BLOCK:(PJ_V + 1) * PJ_BLOCK]
    vt = vt.reshape(BATCH, NKB, KB, A_WIDTH).swapaxes(2, 3)
    iwt = iw.reshape(BATCH, SEQ, LANES)[:, :, :IDX_HEADS].swapaxes(1, 2)
    posq = positions.reshape(BATCH, 1, SEQ)
    posk = jnp.broadcast_to(positions[:, :, None], (BATCH, SEQ, LANES))
    pblk = positions.reshape(BATCH, NKB, KB)
    pmin = jnp.min(pblk, axis=-1)
    pmax = jnp.max(pblk, axis=-1)
    tab = jnp.zeros((A_HEADS, LANES), F32).at[:, :N_BUCKETS].set(rel_bias.astype(F32).T)
    far = rel_bias[N_BUCKETS // 2 - 1, :].astype(F32)

    grid_spec = pltpu.PrefetchScalarGridSpec(
        num_scalar_prefetch=2,
        grid=(BATCH, NQB),
        in_specs=[
            pl.BlockSpec((None, QB, A_WIDTH), lambda b, i, *_: (b, i, PJ_Q)),
            pl.BlockSpec((None, SEQ, A_WIDTH), lambda b, i, *_: (b, 0, PJ_K),
                         pipeline_mode=pl.Buffered(1)),
            pl.BlockSpec((None, NKB, A_WIDTH, KB), lambda b, i, *_: (b, 0, 0, 0),
                         pipeline_mode=pl.Buffered(1)),
            pl.BlockSpec((None, QB, A_WIDTH), lambda b, i, *_: (b, i, PJ_IQ)),
            pl.BlockSpec((None, SEQ, LANES), lambda b, i, *_: (b, 0, 0)),
            pl.BlockSpec((None, IDX_HEADS, QB), lambda b, i, *_: (b, 0, i)),
            pl.BlockSpec((None, 1, QB), lambda b, i, *_: (b, 0, i)),
            pl.BlockSpec((None, SEQ, LANES), lambda b, i, *_: (b, 0, 0)),
            pl.BlockSpec((None, QB, A_WIDTH), lambda b, i, *_: (b, i, PJ_AZ)),
            pl.BlockSpec((A_HEADS, LANES), lambda b, i, *_: (0, 0)),
            pl.BlockSpec(memory_space=pltpu.SMEM),
        ],
        out_specs=pl.BlockSpec((None, QB, A_WIDTH), lambda b, i, *_: (b, i, 0)),
        scratch_shapes=[
            pltpu.VMEM((NKB, KB, QB), jnp.int32),
            pltpu.VMEM((4, NKB, KB, QB), BF16),
            pltpu.VMEM((NKB, KB, QB), BF16),
            pltpu.VMEM((IDX_HEADS, QB, LANES), BF16),
            pltpu.VMEM((A_HEADS, A_HEAD_DIM, QB), F32),
            pltpu.VMEM((A_HEADS, 1, QB), F32),
            pltpu.VMEM((A_HEADS, 1, QB), F32),
            pltpu.VMEM((A_HEADS, 1, QB), F32),
            pltpu.VMEM((A_HEADS, KB, QB), F32),
            pltpu.VMEM((A_HEADS, KB, QB), BF16),
        ],
    )
    out = pl.pallas_call(
        _dsa_kernel,
        grid_spec=grid_spec,
        out_shape=jax.ShapeDtypeStruct((BATCH, SEQ, A_WIDTH), BF16),
        compiler_params=_params(2),
        name="dsa",
    )(pmin, pmax, pj3, pj3, vt, pj3, ikd3, iwt, posq, posk, pj3, tab, far)
    return out.reshape(TOKENS, A_WIDTH)


def _ret_kernel(cdec_ref, q_ref, k_ref, v_ref, z_ref, gain_ref, dec_ref, te_ref, fs_ref, o_ref,
                state_scr):
    @pl.when(pl.program_id(1) == 0)
    def _():
        state_scr[...] = jnp.zeros(state_scr.shape, F32)

    def chunk_body(c, carry):
        r0 = pl.multiple_of(c * CHUNK, CHUNK)
        for h in range(R_HEADS):
            hs = slice(h * R_KEY_DIM, (h + 1) * R_KEY_DIM)
            q = q_ref[pl.ds(r0, CHUNK), hs]
            k = k_ref[pl.ds(r0, CHUNK), hs]
            v = v_ref[pl.ds(r0, CHUNK), hs]
            sc = lax.dot_general(q, k, NT_DIMS, preferred_element_type=F32) * dec_ref[h]
            intra = jnp.dot(sc.astype(BF16), v, preferred_element_type=F32)
            state = state_scr[h]
            qs = (q.astype(F32) * fs_ref[h]).astype(BF16)
            cross = jnp.dot(qs, state.astype(BF16), preferred_element_type=F32)
            ke = (k.astype(F32) * te_ref[h]).T.astype(BF16)
            kv = jnp.dot(ke, v, preferred_element_type=F32)
            state_scr[h] = state * cdec_ref[h] + kv
            y = intra + cross
            mu = jnp.mean(y, axis=-1, keepdims=True)
            var = jnp.mean((y - mu) ** 2, axis=-1, keepdims=True)
            yn = (y - mu) * lax.rsqrt(var + EPS) * gain_ref[:, hs]
            z = z_ref[pl.ds(r0, CHUNK), hs].astype(F32)
            o_ref[pl.ds(r0, CHUNK), hs] = (yn * (z * jax.nn.sigmoid(z))).astype(o_ref.dtype)
        return carry

    lax.fori_loop(0, RB // CHUNK, chunk_body, 0)


def _retention(rqk, pj, gn_gain):
    log_g = jnp.log(1.0 - 2.0 ** (-5.0 - jnp.arange(R_HEADS, dtype=F32)))
    pos = jnp.arange(CHUNK, dtype=F32)
    dist = jnp.abs(pos[:, None] - pos[None, :])
    intra_decay = jnp.exp(log_g[:, None, None] * dist)
    to_end = jnp.exp(log_g[:, None] * (CHUNK - 1.0 - pos)[None, :])
    from_start = jnp.exp(log_g[:, None] * (pos + 1.0)[None, :])
    chunk_decay = jnp.exp(log_g * CHUNK)
    te = jnp.broadcast_to(to_end[:, :, None], (R_HEADS, CHUNK, R_KEY_DIM))
    fs = jnp.broadcast_to(from_start[:, :, None], (R_HEADS, CHUNK, R_KEY_DIM))

    rqk3 = rqk.reshape(BATCH, SEQ, 2 * R_WIDTH)
    pj3 = pj.reshape(BATCH, SEQ, PJ_WIDTH)
    out = pl.pallas_call(
        _ret_kernel,
        grid=(BATCH, SEQ // RB),
        in_specs=[
            pl.BlockSpec(memory_space=pltpu.SMEM),
            pl.BlockSpec((None, RB, R_WIDTH), lambda b, i: (b, i, 0)),
            pl.BlockSpec((None, RB, R_WIDTH), lambda b, i: (b, i, 1)),
            pl.BlockSpec((None, RB, R_WIDTH), lambda b, i: (b, i, PJ_RV)),
            pl.BlockSpec((None, RB, R_WIDTH), lambda b, i: (b, i, PJ_RZ)),
            pl.BlockSpec((1, R_WIDTH), lambda b, i: (0, 0)),
            pl.BlockSpec((R_HEADS, CHUNK, CHUNK), lambda b, i: (0, 0, 0)),
            pl.BlockSpec((R_HEADS, CHUNK, R_KEY_DIM), lambda b, i: (0, 0, 0)),
            pl.BlockSpec((R_HEADS, CHUNK, R_KEY_DIM), lambda b, i: (0, 0, 0)),
        ],
        out_specs=pl.BlockSpec((None, RB, R_WIDTH), lambda b, i: (b, i, 0)),
        out_shape=jax.ShapeDtypeStruct((BATCH, SEQ, R_WIDTH), BF16),
        scratch_shapes=[pltpu.VMEM((R_HEADS, R_KEY_DIM, R_VAL_DIM), F32)],
        compiler_params=_params(2),
        name="retention",
    )(chunk_decay, rqk3, rqk3, pj3, pj3, gn_gain.reshape(1, R_WIDTH), intra_decay, te, fs)
    return out.reshape(TOKENS, R_WIDTH)


def _out_kernel(a_ref, b_ref, ga_ref, gb_ref, x_ref, p_ref, wa_ref, wb_ref, wo_ref, wp_ref, wg_ref,
                fg_ref, o_ref):
    ta = jnp.dot(a_ref[...], wa_ref[...], preferred_element_type=F32)
    tb = jnp.dot(b_ref[...], wb_ref[...], preferred_element_type=F32)
    merged = (jax.nn.sigmoid(ga_ref[...].astype(F32)) * ta
              + jax.nn.sigmoid(gb_ref[...].astype(F32)) * tb)
    r = x_ref[...] + jnp.dot(merged.astype(BF16), wo_ref[...], preferred_element_type=F32)
    u = jnp.dot(p_ref[...].astype(BF16), wp_ref[...], preferred_element_type=F32)
    g = jnp.dot(r.astype(BF16), wg_ref[...], preferred_element_type=F32)
    y = r + u * jax.nn.sigmoid(g)
    ms = jnp.mean(y * y, axis=-1, keepdims=True)
    o_ref[...] = y * lax.rsqrt(ms + EPS) * fg_ref[...]


def _output(a_out, b_out, pj, x2d, p2d, wa, wb, wo, wp, wg, final_gain, tm=256):
    def resident(shape):
        return pl.BlockSpec(shape, lambda i: (0, 0), pipeline_mode=pl.Buffered(1))

    return pl.pallas_call(
        _out_kernel,
        grid=(TOKENS // tm,),
        in_specs=[pl.BlockSpec((tm, A_WIDTH), lambda i: (i, 0)),
                  pl.BlockSpec((tm, R_WIDTH), lambda i: (i, 0)),
                  pl.BlockSpec((tm, D_MODEL), lambda i: (i, PJ_GA * PJ_BLOCK // D_MODEL)),
                  pl.BlockSpec((tm, D_MODEL), lambda i: (i, PJ_GB * PJ_BLOCK // D_MODEL)),
                  pl.BlockSpec((tm, D_MODEL), lambda i: (i, 0)),
                  pl.BlockSpec((tm, PLE_DIM), lambda i: (i, 0)),
                  resident((A_WIDTH, D_MODEL)),
                  resident((R_WIDTH, D_MODEL)),
                  resident((D_MODEL, D_MODEL)),
                  resident((PLE_DIM, D_MODEL)),
                  resident((D_MODEL, D_MODEL)),
                  pl.BlockSpec((1, D_MODEL), lambda i: (0, 0))],
        out_specs=pl.BlockSpec((tm, D_MODEL), lambda i: (i, 0)),
        out_shape=jax.ShapeDtypeStruct((TOKENS, D_MODEL), F32),
        compiler_params=_params(1),
        name="output",
    )(a_out, b_out, pj, pj, x2d, p2d, wa, wb, wo, wp, wg, final_gain.reshape(1, D_MODEL))


def _prepare_weights(w_in):
    w_plain = jnp.concatenate(
        [w_in[:, :COL_IQ],
         w_in[:, COL_RV:],
         w_in[:, COL_IQ:COL_IK] * (IDX_DIM ** -0.5)],
        axis=1).astype(BF16)
    w_rope = w_in[:, COL_RQ:COL_RV].astype(BF16)
    w_ik = w_in[:, COL_IK:COL_IW]
    w_iw = w_in[:, COL_IW:COL_RQ] * (IDX_HEADS ** -0.5)
    w_idx = jnp.concatenate(
        [w_ik, w_ik, w_iw, jnp.zeros((D_MODEL, LANES - IDX_HEADS), w_in.dtype)], axis=1).astype(BF16)
    return w_plain, w_rope, w_idx


def _rope_tables(positions):
    half = R_KEY_DIM // 2
    inv_freq = ROPE_BASE ** (-jnp.arange(half, dtype=F32) / half)
    ang = positions[:, :, None].astype(F32) * inv_freq
    cos = jnp.cos(ang)
    sin = jnp.sin(ang)
    cos2 = jnp.concatenate([cos, cos], axis=-1).reshape(TOKENS, R_KEY_DIM)
    sin2 = jnp.concatenate([-sin, sin], axis=-1).reshape(TOKENS, R_KEY_DIM)
    return cos2, sin2


def kernel(x, p, positions, w_in, norm_gain, w_a_out, w_b_out, w_o, ret_gn_gain, w_ple, w_ple_gate,
           rel_bias, final_gain):
    assert x.shape == (BATCH, SEQ, D_MODEL) and w_in.shape == (1, D_MODEL, IN_WIDTH)
    x2d = x.reshape(TOKENS, D_MODEL)
    p2d = p[0].reshape(TOKENS, PLE_DIM)
    w_plain, w_rope, w_idx = _prepare_weights(w_in[0])
    cos2, sin2 = _rope_tables(positions)

    h = _rmsnorm(x2d, norm_gain[0])
    pj = _proj(h, w_plain)
    rqk = _proj_rope(h, w_rope, cos2, sin2)
    ikd, iw = _proj_idx(h, w_idx)

    a_out = _dsa(pj, ikd, iw, positions, rel_bias)
    b_out = _retention(rqk, pj, ret_gn_gain[0])

    out = _output(a_out, b_out, pj, x2d, p2d, w_a_out[0].astype(BF16), w_b_out[0].astype(BF16),
                  w_o[0].astype(BF16), w_ple[0].astype(BF16), w_ple_gate[0].astype(BF16), final_gain)
    return out.reshape(BATCH, SEQ, D_MODEL)
```

```python
import functools
import math

import jax
import jax.numpy as jnp
from jax import lax
from jax.experimental import pallas as pl
from jax.experimental.pallas import tpu as pltpu

D_MODEL = 2048
BATCH = 4
SEQ = 4096
TOKENS = BATCH * SEQ
CHUNK = 64
PLE_DIM = 256
EPS = 1e-6
A_HEADS = 8
A_HEAD_DIM = 128
A_WIDTH = A_HEADS * A_HEAD_DIM
IDX_HEADS = 16
IDX_DIM = 64
TOPK = min(256, SEQ // 4)
R_HEADS = 8
R_KEY_DIM = 128
R_VAL_DIM = 128
R_WIDTH = R_HEADS * R_VAL_DIM
ROPE_BASE = 10000.0
N_BUCKETS = 32
MAX_DISTANCE = 128

COL_IQ = 4 * A_WIDTH
COL_IK = COL_IQ + IDX_HEADS * IDX_DIM
COL_IW = COL_IK + IDX_DIM
COL_RQ = COL_IW + IDX_HEADS
COL_RV = COL_RQ + 2 * R_WIDTH
IN_WIDTH = COL_RV + 2 * R_WIDTH + 2 * D_MODEL

PA_Q, PA_K, PA_AZ = 0, 1, 2
PA_WIDTH = 3 * A_WIDTH
PB_RV, PB_RZ, PB_GA, PB_GB = 0, 1, 2, 4
PB_WIDTH = 2 * R_WIDTH + 2 * D_MODEL

LANES = 128
QB = 256
KB = 256
NQB = SEQ // QB
NKB = SEQ // KB
RB = 512
NEG = -1e30
INT_MIN = -(2 ** 31)
LOG2E = math.log2(math.e)
VMEM_LIMIT = 56 * 1024 * 1024

F32 = jnp.float32
BF16 = jnp.bfloat16
NT_DIMS = (((1,), (1,)), ((), ()))


def _params(n_axes):
    return pltpu.CompilerParams(dimension_semantics=("arbitrary",) * n_axes,
                                vmem_limit_bytes=VMEM_LIMIT)


def _rmsnorm_kernel(x_ref, g_ref, o_ref):
    x = x_ref[...]
    ms = jnp.mean(x * x, axis=-1, keepdims=True)
    o_ref[...] = (x * lax.rsqrt(ms + EPS) * g_ref[...]).astype(o_ref.dtype)


def _rmsnorm(x2d, gain, tm=512):
    return pl.pallas_call(
        _rmsnorm_kernel,
        grid=(TOKENS // tm,),
        in_specs=[pl.BlockSpec((tm, D_MODEL), lambda i: (i, 0)),
                  pl.BlockSpec((1, D_MODEL), lambda i: (0, 0))],
        out_specs=pl.BlockSpec((tm, D_MODEL), lambda i: (i, 0)),
        out_shape=jax.ShapeDtypeStruct((TOKENS, D_MODEL), BF16),
        compiler_params=_params(1),
        name="rmsnorm",
    )(x2d, gain.reshape(1, D_MODEL))


def _proj_kernel(h_ref, w_ref, o_ref):
    o_ref[...] = jnp.dot(h_ref[...], w_ref[...], preferred_element_type=F32).astype(o_ref.dtype)


def _proj(h, w, name, skip=None, tm=2048, tn=512):
    ncols = w.shape[1]
    wmap = lambda i, j: (0, j)
    if skip is not None:
        c0, c1 = skip[0] // tn, skip[1] // tn
        ncols -= skip[1] - skip[0]
        wmap = lambda i, j: (0, jnp.where(j < c0, j, j + (c1 - c0)))
    return pl.pallas_call(
        _proj_kernel,
        grid=(TOKENS // tm, ncols // tn),
        in_specs=[pl.BlockSpec((tm, D_MODEL), lambda i, j: (i, 0)),
                  pl.BlockSpec((D_MODEL, tn), wmap)],
        out_specs=pl.BlockSpec((tm, tn), lambda i, j: (i, j)),
        out_shape=jax.ShapeDtypeStruct((TOKENS, ncols), BF16),
        compiler_params=_params(2),
        name=name,
    )(h, w)


def _proj_vt_kernel(h_ref, w_ref, o_ref):
    acc = jnp.dot(h_ref[...], w_ref[...], preferred_element_type=F32)
    for t in range(o_ref.shape[0]):
        o_ref[t] = acc[t * KB:(t + 1) * KB, :].T.astype(o_ref.dtype)


def _proj_vt(h, w, col0, tm=2048, tn=512):
    c0 = col0 // tn
    per_batch = SEQ // tm
    return pl.pallas_call(
        _proj_vt_kernel,
        grid=(TOKENS // tm, A_WIDTH // tn),
        in_specs=[pl.BlockSpec((tm, D_MODEL), lambda i, j: (i, 0)),
                  pl.BlockSpec((D_MODEL, tn), lambda i, j: (0, j + c0))],
        out_specs=pl.BlockSpec((None, tm // KB, tn, KB),
                               lambda i, j: (i // per_batch, i % per_batch, j, 0)),
        out_shape=jax.ShapeDtypeStruct((BATCH, NKB, A_WIDTH, KB), BF16),
        compiler_params=_params(2),
        name="proj_vt",
    )(h, w)


def _proj_idx_kernel(h_ref, w_ref, ik_ref, iw_ref):
    acc = jnp.dot(h_ref[...], w_ref[...], preferred_element_type=F32)
    ik = acc[:, :LANES]
    mu = jnp.mean(ik, axis=-1, keepdims=True)
    var = jnp.mean((ik - mu) ** 2, axis=-1, keepdims=True)
    ik_ref[...] = ((ik - mu) * lax.rsqrt(var + EPS)).astype(ik_ref.dtype)
    iw_ref[...] = acc[:, LANES:]


def _proj_idx(h, w_idx, tm=2048):
    return pl.pallas_call(
        _proj_idx_kernel,
        grid=(TOKENS // tm,),
        in_specs=[pl.BlockSpec((tm, D_MODEL), lambda i: (i, 0)),
                  pl.BlockSpec((D_MODEL, 2 * LANES), lambda i: (0, 0))],
        out_specs=[pl.BlockSpec((tm, LANES), lambda i: (i, 0)),
                   pl.BlockSpec((tm, LANES), lambda i: (i, 0))],
        out_shape=[jax.ShapeDtypeStruct((TOKENS, LANES), BF16),
                   jax.ShapeDtypeStruct((TOKENS, LANES), F32)],
        compiler_params=_params(1),
        name="proj_idx",
    )(h, w_idx)


def _proj_rope_kernel(h_ref, w_ref, pos_ref, freq_ref, sign_ref, o_ref, cos_scr, sin_scr, *,
                      tn, k_tile0, k_scale):
    @pl.when(pl.program_id(1) == 0)
    def _():
        ang = pos_ref[...].astype(F32) * freq_ref[...]
        cos_scr[...] = jnp.cos(ang)
        sin_scr[...] = jnp.sin(ang) * sign_ref[...]

    acc = jnp.dot(h_ref[...], w_ref[...], preferred_element_type=F32)
    cos = cos_scr[...]
    sin = sin_scr[...]
    scale = jnp.where(pl.program_id(1) >= k_tile0, k_scale, 1.0).astype(F32)
    for g in range(tn // LANES):
        xg = acc[:, g * LANES:(g + 1) * LANES]
        rot = xg * cos + pltpu.roll(xg, LANES // 2, 1) * sin
        o_ref[:, g * LANES:(g + 1) * LANES] = (rot * scale).astype(o_ref.dtype)


def _proj_rope(h, w, positions, tm=2048, tn=512):
    ncols = 2 * R_WIDTH
    half = R_KEY_DIM // 2
    inv_freq = ROPE_BASE ** (-jnp.arange(half, dtype=F32) / half)
    freq2 = jnp.concatenate([inv_freq, inv_freq]).reshape(1, R_KEY_DIM)
    sign2 = jnp.concatenate([-jnp.ones((half,), F32), jnp.ones((half,), F32)]).reshape(1, R_KEY_DIM)
    pos_b = jnp.broadcast_to(positions.reshape(TOKENS, 1), (TOKENS, R_KEY_DIM))
    kern = functools.partial(_proj_rope_kernel, tn=tn, k_tile0=R_WIDTH // tn,
                             k_scale=R_KEY_DIM ** -0.5)
    return pl.pallas_call(
        kern,
        grid=(TOKENS // tm, ncols // tn),
        in_specs=[pl.BlockSpec((tm, D_MODEL), lambda i, j: (i, 0)),
                  pl.BlockSpec((D_MODEL, tn), lambda i, j: (0, j)),
                  pl.BlockSpec((tm, R_KEY_DIM), lambda i, j: (i, 0)),
                  pl.BlockSpec((1, R_KEY_DIM), lambda i, j: (0, 0)),
                  pl.BlockSpec((1, R_KEY_DIM), lambda i, j: (0, 0))],
        out_specs=pl.BlockSpec((tm, tn), lambda i, j: (i, j)),
        out_shape=jax.ShapeDtypeStruct((TOKENS, ncols), BF16),
        scratch_shapes=[pltpu.VMEM((tm, R_KEY_DIM), F32), pltpu.VMEM((tm, R_KEY_DIM), F32)],
        compiler_params=_params(2),
        name="proj_rope",
    )(h, w, pos_b, freq2, sign2)


def _to_key(a):
    bits = pltpu.bitcast(a, jnp.int32)
    return jnp.where(bits < 0, bits ^ jnp.int32(0x7FFFFFFF), bits)


def _dsa_kernel(pmin_ref, pmax_ref,
                q_ref, k_ref, vt_ref, iq_ref, ikd_ref, iwt_ref, posq_ref, posk_ref, az_ref,
                tab_ref, far_ref, o_ref,
                keys_scr, dig_scr, dm_scr, iqm_scr, acc_scr, m_scr, l_scr, alpha_scr, s_scr, p_scr):
    b = pl.program_id(0)
    qi = pl.program_id(1)
    nkb = qi + 1

    lane = lax.broadcasted_iota(jnp.int32, (QB, LANES), 1)
    for p in range(IDX_HEADS // 2):
        pair = iq_ref[:, p * LANES:(p + 1) * LANES].astype(F32)
        iqm_scr[2 * p] = jnp.where(lane < IDX_DIM, pair, 0.0).astype(BF16)
        iqm_scr[2 * p + 1] = jnp.where(lane >= IDX_DIM, pair, 0.0).astype(BF16)
    iwt = iwt_ref[...]

    def to_bf16(int_plane):
        return int_plane.astype(F32).astype(BF16)

    def idx_tile(kb, admissible=None):
        s0 = pl.multiple_of(kb * KB, KB)
        kid = ikd_ref[pl.ds(s0, KB), :]
        acc = jnp.zeros((KB, QB), F32)
        for h in range(IDX_HEADS):
            sc = lax.dot_general(kid, iqm_scr[h], NT_DIMS, preferred_element_type=F32)
            acc = acc + jnp.maximum(sc, 0.0) * iwt[h:h + 1, :]
        key = _to_key(acc)
        top = (key >> 24) + 128
        if admissible is not None:
            key = jnp.where(admissible, key, jnp.int32(INT_MIN))
            top = jnp.where(admissible, top, -1)
        keys_scr[kb] = key
        dig_scr[0, kb] = to_bf16(top)
        for d in range(1, 4):
            dig_scr[d, kb] = to_bf16((key >> (24 - 8 * d)) & 255)

    def idx_body(kb, c):
        idx_tile(kb)
        return c

    lax.fori_loop(0, qi, idx_body, 0)
    rr = lax.broadcasted_iota(jnp.int32, (KB, QB), 0)
    cc = lax.broadcasted_iota(jnp.int32, (KB, QB), 1)
    idx_tile(qi, (rr // CHUNK) <= (cc // CHUNK))

    nkb_sel = jnp.where(qi > 0, nkb, 0)
    one_b = jnp.ones((), BF16)
    zero_b = jnp.zeros((), BF16)

    def count_ge(plane_ref, cand):
        cand_b = cand.astype(BF16)

        def body(kb, acc):
            part = jnp.where(plane_ref[kb] >= cand_b, one_b, zero_b)
            slabs = [part[16 * g:16 * (g + 1)] for g in range(KB // 16)]
            while len(slabs) > 1:
                slabs = [slabs[i] + slabs[i + 1] for i in range(0, len(slabs), 2)]
            return acc + slabs[0]

        acc = lax.fori_loop(0, nkb_sel, body, jnp.zeros((16, QB), BF16))
        return jnp.sum(acc.astype(F32), axis=0, keepdims=True)

    def byte_level(plane_ref, need):
        def bit_body(it, carry):
            t, cnt_rejected = carry
            cand = t + jnp.left_shift(jnp.int32(1), 7 - it).astype(F32)
            cnt = count_ge(plane_ref, cand)
            ok = cnt >= need
            return jnp.where(ok, cand, t), jnp.where(ok, cnt_rejected, cnt)

        zeros = jnp.zeros((1, QB), F32)
        t, cnt_above = lax.fori_loop(0, jnp.where(qi > 0, 8, 0), bit_body, (zeros, zeros))
        return t, need - cnt_above

    def remask(src_ref, t, digit_ref):
        t_b = t.astype(BF16)

        def body(kb, c):
            dm_scr[kb] = jnp.where(src_ref[kb] == t_b, digit_ref[kb], -one_b)
            return c

        lax.fori_loop(0, nkb_sel, body, 0)

    need = jnp.full((1, QB), TOPK, F32)
    t1, need = byte_level(dig_scr.at[0], need)
    remask(dig_scr.at[0], t1, dig_scr.at[1])
    t2, need = byte_level(dm_scr, need)
    remask(dm_scr, t2, dig_scr.at[2])
    t3, need = byte_level(dm_scr, need)
    remask(dm_scr, t3, dig_scr.at[3])
    t4, _ = byte_level(dm_scr, need)
    thr_bytes = [t.astype(jnp.int32) for t in (t1, t2, t3, t4)]
    thr = (((thr_bytes[0] - 128) << 24) | (thr_bytes[1] << 16) | (thr_bytes[2] << 8) | thr_bytes[3])
    thr = jnp.where(qi > 0, thr, jnp.int32(INT_MIN + 1))

    def count_where(pred_fn):
        def body(kb, acc):
            part = jnp.where(pred_fn(kb, keys_scr[kb]), 1.0, 0.0)
            return acc + jnp.sum(part.reshape(KB // 8, 8, QB), axis=0)
        acc = lax.fori_loop(0, nkb, body, jnp.zeros((8, QB), F32))
        return jnp.sum(acc, axis=0, keepdims=True)

    cnt_last = count_ge(dm_scr, t4)
    has_tie = jnp.logical_and(qi > 0, jnp.max(cnt_last - need) > 0)

    @pl.when(has_tie)
    def _():
        cnt_gt = count_where(lambda kb, kk: kk > thr)
        need = TOPK - cnt_gt

        def key_index(kb):
            return kb * KB + rr

        def jb_body(it, j0):
            cand = j0 | jnp.left_shift(jnp.int32(1), 11 - it)
            f = count_where(lambda kb, kk: jnp.logical_and(kk == thr, key_index(kb) < cand))
            return jnp.where(f < need, cand, j0)

        j0 = lax.fori_loop(0, 12, jb_body, jnp.zeros((1, QB), jnp.int32))
        jstar = j0 + 1

        def fix_body(kb, c):
            kk = keys_scr[kb]
            drop = jnp.logical_and(kk == thr, key_index(kb) >= jstar)
            keys_scr[kb] = jnp.where(drop, kk - 1, kk)
            return c

        lax.fori_loop(0, nkb, fix_body, 0)

    m_scr[...] = jnp.full(m_scr.shape, NEG, F32)
    l_scr[...] = jnp.zeros(l_scr.shape, F32)
    acc_scr[...] = jnp.zeros(acc_scr.shape, F32)
    posq = posq_ref[...]
    pmin_q = pmin_ref[b, qi]
    half = N_BUCKETS // 2
    max_exact = half // 2

    def att_tile(kb, near):
        s0 = pl.multiple_of(kb * KB, KB)
        madd = jnp.where(keys_scr[kb] >= thr, 0.0, NEG)
        for h in range(A_HEADS):
            hs = slice(h * A_HEAD_DIM, (h + 1) * A_HEAD_DIM)
            s_scr[h] = lax.dot_general(k_ref[pl.ds(s0, KB), hs], q_ref[:, hs], NT_DIMS,
                                       preferred_element_type=F32)
        if near:
            pk = posk_ref[pl.ds(s0, KB), :]
            rel = jnp.concatenate([pk] * (QB // LANES), axis=1) - posq
            n = jnp.abs(rel)
            nf = jnp.maximum(n, 1).astype(F32)
            large = max_exact + (jnp.log(nf / max_exact) / math.log(MAX_DISTANCE / max_exact)
                                 * (half - max_exact)).astype(jnp.int32)
            large = jnp.minimum(large, half - 1)
            bucket = jnp.where(rel > 0, half, 0) + jnp.where(n < max_exact, n, large)
        for h in range(A_HEADS):
            if near:
                row = jnp.broadcast_to(tab_ref[h:h + 1, :], (KB, LANES))
                bias = jnp.concatenate(
                    [jnp.take_along_axis(row, bucket[:, g * LANES:(g + 1) * LANES], axis=1)
                     for g in range(QB // LANES)], axis=1)
                s = s_scr[h] + ((bias - far_ref[h]) * LOG2E + madd)
            else:
                s = s_scr[h] + madd
            m_old = m_scr[h]
            m_new = jnp.maximum(m_old, jnp.max(s, axis=0, keepdims=True))
            alpha = jnp.exp2(m_old - m_new)
            p = jnp.exp2(s - m_new)
            l_scr[h] = alpha * l_scr[h] + jnp.sum(p, axis=0, keepdims=True)
            m_scr[h] = m_new
            alpha_scr[h] = alpha
            p_scr[h] = p.astype(BF16)
        for h in range(A_HEADS):
            hs = slice(h * A_HEAD_DIM, (h + 1) * A_HEAD_DIM)
            pv = jnp.dot(vt_ref[kb, hs, :], p_scr[h], preferred_element_type=F32)
            acc_scr[h] = alpha_scr[h] * acc_scr[h] + pv

    def att_body(kb, c):
        far = (pmin_q - pmax_ref[b, kb]) >= MAX_DISTANCE
        pl.when(far)(functools.partial(att_tile, kb, False))
        pl.when(jnp.logical_not(far))(functools.partial(att_tile, kb, True))
        return c

    lax.fori_loop(0, nkb, att_body, 0)

    for h in range(A_HEADS):
        hs = slice(h * A_HEAD_DIM, (h + 1) * A_HEAD_DIM)
        z = az_ref[:, hs].astype(F32)
        out_t = acc_scr[h] / l_scr[h]
        o_ref[:, hs] = (out_t.T * (z * jax.nn.sigmoid(z))).astype(o_ref.dtype)


def _dsa(pa, vt, iq, ikd, iw, positions, rel_bias):
    pa3 = pa.reshape(BATCH, SEQ, PA_WIDTH)
    iq3 = iq.reshape(BATCH, SEQ, A_WIDTH)
    ikd3 = ikd.reshape(BATCH, SEQ, LANES)
    iwt = iw.reshape(BATCH, SEQ, LANES)[:, :, :IDX_HEADS].swapaxes(1, 2)
    posq = positions.reshape(BATCH, 1, SEQ)
    posk = jnp.broadcast_to(positions[:, :, None], (BATCH, SEQ, LANES))
    pblk = positions.reshape(BATCH, NKB, KB)
    pmin = jnp.min(pblk, axis=-1)
    pmax = jnp.max(pblk, axis=-1)
    tab = jnp.zeros((A_HEADS, LANES), F32).at[:, :N_BUCKETS].set(rel_bias.astype(F32).T)
    far = rel_bias[N_BUCKETS // 2 - 1, :].astype(F32)

    grid_spec = pltpu.PrefetchScalarGridSpec(
        num_scalar_prefetch=2,
        grid=(BATCH, NQB),
        in_specs=[
            pl.BlockSpec((None, QB, A_WIDTH), lambda b, i, *_: (b, i, PA_Q)),
            pl.BlockSpec((None, SEQ, A_WIDTH), lambda b, i, *_: (b, 0, PA_K),
                         pipeline_mode=pl.Buffered(1)),
            pl.BlockSpec((None, NKB, A_WIDTH, KB), lambda b, i, *_: (b, 0, 0, 0),
                         pipeline_mode=pl.Buffered(1)),
            pl.BlockSpec((None, QB, A_WIDTH), lambda b, i, *_: (b, i, 0)),
            pl.BlockSpec((None, SEQ, LANES), lambda b, i, *_: (b, 0, 0)),
            pl.BlockSpec((None, IDX_HEADS, QB), lambda b, i, *_: (b, 0, i)),
            pl.BlockSpec((None, 1, QB), lambda b, i, *_: (b, 0, i)),
            pl.BlockSpec((None, SEQ, LANES), lambda b, i, *_: (b, 0, 0)),
            pl.BlockSpec((None, QB, A_WIDTH), lambda b, i, *_: (b, i, PA_AZ)),
            pl.BlockSpec((A_HEADS, LANES), lambda b, i, *_: (0, 0)),
            pl.BlockSpec(memory_space=pltpu.SMEM),
        ],
        out_specs=pl.BlockSpec((None, QB, A_WIDTH), lambda b, i, *_: (b, i, 0)),
        scratch_shapes=[
            pltpu.VMEM((NKB, KB, QB), jnp.int32),
            pltpu.VMEM((4, NKB, KB, QB), BF16),
            pltpu.VMEM((NKB, KB, QB), BF16),
            pltpu.VMEM((IDX_HEADS, QB, LANES), BF16),
            pltpu.VMEM((A_HEADS, A_HEAD_DIM, QB), F32),
            pltpu.VMEM((A_HEADS, 1, QB), F32),
            pltpu.VMEM((A_HEADS, 1, QB), F32),
            pltpu.VMEM((A_HEADS, 1, QB), F32),
            pltpu.VMEM((A_HEADS, KB, QB), F32),
            pltpu.VMEM((A_HEADS, KB, QB), BF16),
        ],
    )
    out = pl.pallas_call(
        _dsa_kernel,
        grid_spec=grid_spec,
        out_shape=jax.ShapeDtypeStruct((BATCH, SEQ, A_WIDTH), BF16),
        compiler_params=_params(2),
        name="dsa",
    )(pmin, pmax, pa3, pa3, vt, iq3, ikd3, iwt, posq, posk, pa3, tab, far)
    return out.reshape(TOKENS, A_WIDTH)


def _ret_kernel(cdec_ref, q_ref, k_ref, v_ref, z_ref, gain_ref, dec_ref, te_ref, fs_ref, o_ref,
                state_scr):
    @pl.when(pl.program_id(1) == 0)
    def _():
        state_scr[...] = jnp.zeros(state_scr.shape, F32)

    heads = [slice(h * R_KEY_DIM, (h + 1) * R_KEY_DIM) for h in range(R_HEADS)]

    def chunk_body(c, carry):
        r0 = pl.multiple_of(c * CHUNK, CHUNK)
        rows = pl.ds(r0, CHUNK)
        scores, cross, kv = [], [], []
        for h, hs in enumerate(heads):
            q = q_ref[rows, hs]
            k = k_ref[rows, hs]
            scores.append(lax.dot_general(q, k, NT_DIMS, preferred_element_type=F32))
            qs = (q.astype(F32) * fs_ref[h]).astype(BF16)
            cross.append(jnp.dot(qs, state_scr[h].astype(BF16), preferred_element_type=F32))
            ke = (k.astype(F32) * te_ref[h]).T.astype(BF16)
            kv.append(jnp.dot(ke, v_ref[rows, hs], preferred_element_type=F32))
        for h in range(R_HEADS):
            state_scr[h] = state_scr[h] * cdec_ref[h] + kv[h]
        ys = []
        for h, hs in enumerate(heads):
            sc = (scores[h] * dec_ref[h]).astype(BF16)
            ys.append(jnp.dot(sc, v_ref[rows, hs], preferred_element_type=F32) + cross[h])
        for h, hs in enumerate(heads):
            y = ys[h]
            mu = jnp.mean(y, axis=-1, keepdims=True)
            var = jnp.mean((y - mu) ** 2, axis=-1, keepdims=True)
            yn = (y - mu) * lax.rsqrt(var + EPS) * gain_ref[:, hs]
            z = z_ref[rows, hs].astype(F32)
            o_ref[rows, hs] = (yn * (z * jax.nn.sigmoid(z))).astype(o_ref.dtype)
        return carry

    lax.fori_loop(0, RB // CHUNK, chunk_body, 0)


def _retention(rqk, pb, gn_gain):
    log_g = jnp.log(1.0 - 2.0 ** (-5.0 - jnp.arange(R_HEADS, dtype=F32)))
    pos = jnp.arange(CHUNK, dtype=F32)
    dist = jnp.abs(pos[:, None] - pos[None, :])
    intra_decay = jnp.exp(log_g[:, None, None] * dist)
    to_end = jnp.exp(log_g[:, None] * (CHUNK - 1.0 - pos)[None, :])
    from_start = jnp.exp(log_g[:, None] * (pos + 1.0)[None, :])
    chunk_decay = jnp.exp(log_g * CHUNK)
    te = jnp.broadcast_to(to_end[:, :, None], (R_HEADS, CHUNK, R_KEY_DIM))
    fs = jnp.broadcast_to(from_start[:, :, None], (R_HEADS, CHUNK, R_KEY_DIM))

    rqk3 = rqk.reshape(BATCH, SEQ, 2 * R_WIDTH)
    pb3 = pb.reshape(BATCH, SEQ, PB_WIDTH)
    out = pl.pallas_call(
        _ret_kernel,
        grid=(BATCH, SEQ // RB),
        in_specs=[
            pl.BlockSpec(memory_space=pltpu.SMEM),
            pl.BlockSpec((None, RB, R_WIDTH), lambda b, i: (b, i, 0)),
            pl.BlockSpec((None, RB, R_WIDTH), lambda b, i: (b, i, 1)),
            pl.BlockSpec((None, RB, R_WIDTH), lambda b, i: (b, i, PB_RV)),
            pl.BlockSpec((None, RB, R_WIDTH), lambda b, i: (b, i, PB_RZ)),
            pl.BlockSpec((1, R_WIDTH), lambda b, i: (0, 0)),
            pl.BlockSpec((R_HEADS, CHUNK, CHUNK), lambda b, i: (0, 0, 0)),
            pl.BlockSpec((R_HEADS, CHUNK, R_KEY_DIM), lambda b, i: (0, 0, 0)),
            pl.BlockSpec((R_HEADS, CHUNK, R_KEY_DIM), lambda b, i: (0, 0, 0)),
        ],
        out_specs=pl.BlockSpec((None, RB, R_WIDTH), lambda b, i: (b, i, 0)),
        out_shape=jax.ShapeDtypeStruct((BATCH, SEQ, R_WIDTH), BF16),
        scratch_shapes=[pltpu.VMEM((R_HEADS, R_KEY_DIM, R_VAL_DIM), F32)],
        compiler_params=_params(2),
        name="retention",
    )(chunk_decay, rqk3, rqk3, pb3, pb3, gn_gain.reshape(1, R_WIDTH), intra_decay, te, fs)
    return out.reshape(TOKENS, R_WIDTH)


def _out_kernel(a_ref, b_ref, ga_ref, gb_ref, x_ref, p_ref, wa_ref, wb_ref, wo_ref, wp_ref, wg_ref,
                fg_ref, o_ref):
    ta = jnp.dot(a_ref[...], wa_ref[...], preferred_element_type=F32)
    tb = jnp.dot(b_ref[...], wb_ref[...], preferred_element_type=F32)
    merged = (jax.nn.sigmoid(ga_ref[...].astype(F32)) * ta
              + jax.nn.sigmoid(gb_ref[...].astype(F32)) * tb)
    r = x_ref[...] + jnp.dot(merged.astype(BF16), wo_ref[...], preferred_element_type=F32)
    u = jnp.dot(p_ref[...].astype(BF16), wp_ref[...], preferred_element_type=F32)
    g = jnp.dot(r.astype(BF16), wg_ref[...], preferred_element_type=F32)
    y = r + u * jax.nn.sigmoid(g)
    ms = jnp.mean(y * y, axis=-1, keepdims=True)
    o_ref[...] = y * lax.rsqrt(ms + EPS) * fg_ref[...]


def _output(a_out, b_out, pb, x2d, p2d, wa, wb, wo, wp, wg, final_gain, tm=256):
    def resident(shape):
        return pl.BlockSpec(shape, lambda i: (0, 0), pipeline_mode=pl.Buffered(1))

    return pl.pallas_call(
        _out_kernel,
        grid=(TOKENS // tm,),
        in_specs=[pl.BlockSpec((tm, A_WIDTH), lambda i: (i, 0)),
                  pl.BlockSpec((tm, R_WIDTH), lambda i: (i, 0)),
                  pl.BlockSpec((tm, D_MODEL), lambda i: (i, PB_GA * R_WIDTH // D_MODEL)),
                  pl.BlockSpec((tm, D_MODEL), lambda i: (i, PB_GB * R_WIDTH // D_MODEL)),
                  pl.BlockSpec((tm, D_MODEL), lambda i: (i, 0)),
                  pl.BlockSpec((tm, PLE_DIM), lambda i: (i, 0)),
                  resident((A_WIDTH, D_MODEL)),
                  resident((R_WIDTH, D_MODEL)),
                  resident((D_MODEL, D_MODEL)),
                  resident((PLE_DIM, D_MODEL)),
                  resident((D_MODEL, D_MODEL)),
                  pl.BlockSpec((1, D_MODEL), lambda i: (0, 0))],
        out_specs=pl.BlockSpec((tm, D_MODEL), lambda i: (i, 0)),
        out_shape=jax.ShapeDtypeStruct((TOKENS, D_MODEL), F32),
        compiler_params=_params(1),
        name="output",
    )(a_out, b_out, pb, pb, x2d, p2d, wa, wb, wo, wp, wg, final_gain.reshape(1, D_MODEL))


def _prepare_weights(w_in):
    q_scale = jnp.where(jnp.arange(COL_IQ) < A_WIDTH, A_HEAD_DIM ** -0.5 * LOG2E, 1.0).astype(F32)
    w_att = (w_in[:, :COL_IQ] * q_scale[None, :]).astype(BF16)
    w_iq = (w_in[:, COL_IQ:COL_IK] * (IDX_DIM ** -0.5)).astype(BF16)
    w_rope = w_in[:, COL_RQ:COL_RV].astype(BF16)
    w_ret = w_in[:, COL_RV:].astype(BF16)
    w_ik = w_in[:, COL_IK:COL_IW]
    w_iw = w_in[:, COL_IW:COL_RQ] * (IDX_HEADS ** -0.5)
    w_idx = jnp.concatenate(
        [w_ik, w_ik, w_iw, jnp.zeros((D_MODEL, LANES - IDX_HEADS), w_in.dtype)], axis=1).astype(BF16)
    return w_att, w_iq, w_rope, w_ret, w_idx


def kernel(x, p, positions, w_in, norm_gain, w_a_out, w_b_out, w_o, ret_gn_gain, w_ple, w_ple_gate,
           rel_bias, final_gain):
    assert x.shape == (BATCH, SEQ, D_MODEL) and w_in.shape == (1, D_MODEL, IN_WIDTH)
    x2d = x.reshape(TOKENS, D_MODEL)
    p2d = p[0].reshape(TOKENS, PLE_DIM)
    w_att, w_iq, w_rope, w_ret, w_idx = _prepare_weights(w_in[0])

    h = _rmsnorm(x2d, norm_gain[0])
    pa = _proj(h, w_att, "proj_att", skip=(2 * A_WIDTH, 3 * A_WIDTH))
    vt = _proj_vt(h, w_att, 2 * A_WIDTH)
    iq = _proj(h, w_iq, "proj_iq")
    pb = _proj(h, w_ret, "proj_ret")
    rqk = _proj_rope(h, w_rope, positions)
    ikd, iw = _proj_idx(h, w_idx)

    a_out = _dsa(pa, vt, iq, ikd, iw, positions, rel_bias)
    b_out = _retention(rqk, pb, ret_gn_gain[0])

    out = _output(a_out, b_out, pb, x2d, p2d, w_a_out[0].astype(BF16), w_b_out[0].astype(BF16),
                  w_o[0].astype(BF16), w_ple[0].astype(BF16), w_ple_gate[0].astype(BF16), final_gain)
    return out.reshape(BATCH, SEQ, D_MODEL)
```

```python
import functools
import math

import jax
import jax.numpy as jnp
from jax import lax
from jax.experimental import pallas as pl
from jax.experimental.pallas import tpu as pltpu

D_MODEL = 2048
BATCH = 4
SEQ = 4096
TOKENS = BATCH * SEQ
CHUNK = 64
PLE_DIM = 256
EPS = 1e-6
A_HEADS = 8
A_HEAD_DIM = 128
A_WIDTH = A_HEADS * A_HEAD_DIM
IDX_HEADS = 16
IDX_DIM = 64
TOPK = min(256, SEQ // 4)
R_HEADS = 8
R_KEY_DIM = 128
R_VAL_DIM = 128
R_WIDTH = R_HEADS * R_VAL_DIM
ROPE_BASE = 10000.0
N_BUCKETS = 32
MAX_DISTANCE = 128

COL_IQ = 4 * A_WIDTH
COL_IK = COL_IQ + IDX_HEADS * IDX_DIM
COL_IW = COL_IK + IDX_DIM
COL_RQ = COL_IW + IDX_HEADS
COL_RV = COL_RQ + 2 * R_WIDTH
IN_WIDTH = COL_RV + 2 * R_WIDTH + 2 * D_MODEL

PA_Q, PA_K, PA_AZ = 0, 1, 2
PA_WIDTH = 3 * A_WIDTH
PB_RV, PB_RZ, PB_GA, PB_GB = 0, 1, 2, 4
PB_WIDTH = 2 * R_WIDTH + 2 * D_MODEL

LANES = 128
QB = 256
KB = 256
NQB = SEQ // QB
NKB = SEQ // KB
RB = 512
NEG = -1e30
INT_MIN = -(2 ** 31)
LOG2E = math.log2(math.e)
VMEM_LIMIT = 56 * 1024 * 1024

F32 = jnp.float32
BF16 = jnp.bfloat16
NT_DIMS = (((1,), (1,)), ((), ()))


def _params(n_axes):
    return pltpu.CompilerParams(dimension_semantics=("arbitrary",) * n_axes,
                                vmem_limit_bytes=VMEM_LIMIT)


def _rmsnorm_kernel(x_ref, g_ref, o_ref):
    x = x_ref[...]
    ms = jnp.mean(x * x, axis=-1, keepdims=True)
    o_ref[...] = (x * lax.rsqrt(ms + EPS) * g_ref[...]).astype(o_ref.dtype)


def _rmsnorm(x2d, gain, tm=512):
    return pl.pallas_call(
        _rmsnorm_kernel,
        grid=(TOKENS // tm,),
        in_specs=[pl.BlockSpec((tm, D_MODEL), lambda i: (i, 0)),
                  pl.BlockSpec((1, D_MODEL), lambda i: (0, 0))],
        out_specs=pl.BlockSpec((tm, D_MODEL), lambda i: (i, 0)),
        out_shape=jax.ShapeDtypeStruct((TOKENS, D_MODEL), BF16),
        compiler_params=_params(1),
        name="rmsnorm",
    )(x2d, gain.reshape(1, D_MODEL))


def _wt_block(rows, row_of):
    assert rows % 8 == 0
    return pl.BlockSpec((pl.Element(rows), pl.Element(D_MODEL)),
                        lambda *g: (pl.multiple_of(row_of(*g), 8), 0))


def _proj_kernel(h_ref, wt_ref, *rest, scale):
    acc = lax.dot_general(h_ref[...], wt_ref[...].astype(BF16), NT_DIMS,
                          preferred_element_type=F32)
    if scale == "row":
        acc = acc * rest[0][...]
    elif scale is not None:
        acc = acc * scale
    rest[-1][...] = acc.astype(rest[-1].dtype)


def _proj(h, wt, name, row0, ncols, scale=None, skip=None, tm=2048, tn=512):
    def row_of(i, j):
        if skip is not None:
            j = jnp.where(j < skip[0] // tn, j, j + (skip[1] - skip[0]) // tn)
        return row0 + j * tn

    in_specs = [pl.BlockSpec((tm, D_MODEL), lambda i, j: (i, 0)), _wt_block(tn, row_of)]
    args = [h, wt]
    if scale is not None and not isinstance(scale, float):
        in_specs.append(pl.BlockSpec((1, tn), lambda i, j: (0, j)))
        args.append(scale)
        scale = "row"
    return pl.pallas_call(
        functools.partial(_proj_kernel, scale=scale),
        grid=(TOKENS // tm, ncols // tn),
        in_specs=in_specs,
        out_specs=pl.BlockSpec((tm, tn), lambda i, j: (i, j)),
        out_shape=jax.ShapeDtypeStruct((TOKENS, ncols), BF16),
        compiler_params=_params(2),
        name=name,
    )(*args)


def _proj_vt_kernel(h_ref, wt_ref, o_ref):
    acc_t = lax.dot_general(wt_ref[...].astype(BF16), h_ref[...], NT_DIMS,
                            preferred_element_type=F32)
    for t in range(o_ref.shape[0]):
        o_ref[t] = acc_t[:, t * KB:(t + 1) * KB].astype(o_ref.dtype)


def _proj_vt(h, wt, row0, tm=2048, tn=512):
    per_batch = SEQ // tm
    return pl.pallas_call(
        _proj_vt_kernel,
        grid=(TOKENS // tm, A_WIDTH // tn),
        in_specs=[pl.BlockSpec((tm, D_MODEL), lambda i, j: (i, 0)),
                  _wt_block(tn, lambda i, j: row0 + j * tn)],
        out_specs=pl.BlockSpec((None, tm // KB, tn, KB),
                               lambda i, j: (i // per_batch, i % per_batch, j, 0)),
        out_shape=jax.ShapeDtypeStruct((BATCH, NKB, A_WIDTH, KB), BF16),
        compiler_params=_params(2),
        name="proj_vt",
    )(h, wt)


def _proj_idx_kernel(h_ref, wik_ref, wiw_ref, ik_ref, iwt_ref):
    h = h_ref[...]
    wik = wik_ref[...].astype(BF16)
    ik = lax.dot_general(h, jnp.concatenate([wik, wik], axis=0), NT_DIMS,
                         preferred_element_type=F32)
    mu = jnp.mean(ik, axis=-1, keepdims=True)
    var = jnp.mean((ik - mu) ** 2, axis=-1, keepdims=True)
    ik_ref[...] = ((ik - mu) * lax.rsqrt(var + EPS)).astype(ik_ref.dtype)
    iwt = lax.dot_general(wiw_ref[...].astype(BF16), h, NT_DIMS, preferred_element_type=F32)
    iwt_ref[...] = iwt * (IDX_HEADS ** -0.5)


def _proj_idx(h, wt, tm=2048):
    return pl.pallas_call(
        _proj_idx_kernel,
        grid=(TOKENS // tm,),
        in_specs=[pl.BlockSpec((tm, D_MODEL), lambda i: (i, 0)),
                  _wt_block(IDX_DIM, lambda i: COL_IK),
                  _wt_block(IDX_HEADS, lambda i: COL_IW)],
        out_specs=[pl.BlockSpec((tm, LANES), lambda i: (i, 0)),
                   pl.BlockSpec((IDX_HEADS, tm), lambda i: (0, i))],
        out_shape=[jax.ShapeDtypeStruct((TOKENS, LANES), BF16),
                   jax.ShapeDtypeStruct((IDX_HEADS, TOKENS), F32)],
        compiler_params=_params(1),
        name="proj_idx",
    )(h, wt, wt)


def _proj_rope_kernel(h_ref, w_ref, pos_ref, freq_ref, sign_ref, o_ref, cos_scr, sin_scr, *,
                      tn, k_tile0, k_scale):
    @pl.when(pl.program_id(1) == 0)
    def _():
        ang = pos_ref[...].astype(F32) * freq_ref[...]
        cos_scr[...] = jnp.cos(ang)
        sin_scr[...] = jnp.sin(ang) * sign_ref[...]

    acc = lax.dot_general(h_ref[...], w_ref[...].astype(BF16), NT_DIMS,
                          preferred_element_type=F32)
    cos = cos_scr[...]
    sin = sin_scr[...]
    scale = jnp.where(pl.program_id(1) >= k_tile0, k_scale, 1.0).astype(F32)
    for g in range(tn // LANES):
        xg = acc[:, g * LANES:(g + 1) * LANES]
        rot = xg * cos + pltpu.roll(xg, LANES // 2, 1) * sin
        o_ref[:, g * LANES:(g + 1) * LANES] = (rot * scale).astype(o_ref.dtype)


def _proj_rope(h, wt, positions, tm=2048, tn=512):
    ncols = 2 * R_WIDTH
    half = R_KEY_DIM // 2
    inv_freq = ROPE_BASE ** (-jnp.arange(half, dtype=F32) / half)
    freq2 = jnp.concatenate([inv_freq, inv_freq]).reshape(1, R_KEY_DIM)
    sign2 = jnp.concatenate([-jnp.ones((half,), F32), jnp.ones((half,), F32)]).reshape(1, R_KEY_DIM)
    pos_b = jnp.broadcast_to(positions.reshape(TOKENS, 1), (TOKENS, R_KEY_DIM))
    kern = functools.partial(_proj_rope_kernel, tn=tn, k_tile0=R_WIDTH // tn,
                             k_scale=R_KEY_DIM ** -0.5)
    return pl.pallas_call(
        kern,
        grid=(TOKENS // tm, ncols // tn),
        in_specs=[pl.BlockSpec((tm, D_MODEL), lambda i, j: (i, 0)),
                  _wt_block(tn, lambda i, j: COL_RQ + j * tn),
                  pl.BlockSpec((tm, R_KEY_DIM), lambda i, j: (i, 0)),
                  pl.BlockSpec((1, R_KEY_DIM), lambda i, j: (0, 0)),
                  pl.BlockSpec((1, R_KEY_DIM), lambda i, j: (0, 0))],
        out_specs=pl.BlockSpec((tm, tn), lambda i, j: (i, j)),
        out_shape=jax.ShapeDtypeStruct((TOKENS, ncols), BF16),
        scratch_shapes=[pltpu.VMEM((tm, R_KEY_DIM), F32), pltpu.VMEM((tm, R_KEY_DIM), F32)],
        compiler_params=_params(2),
        name="proj_rope",
    )(h, wt, pos_b, freq2, sign2)


def _to_key(a):
    bits = pltpu.bitcast(a, jnp.int32)
    return jnp.where(bits < 0, bits ^ jnp.int32(0x7FFFFFFF), bits)


def _dsa_kernel(pmin_ref, pmax_ref,
                q_ref, k_ref, vt_ref, iq_ref, ikd_ref, iwt_ref, posq_ref, posk_ref, az_ref,
                tab_ref, far_ref, o_ref,
                keys_scr, dig_scr, dm_scr, iqm_scr, acc_scr, m_scr, l_scr, alpha_scr, s_scr, p_scr):
    b = pl.program_id(0)
    qi = pl.program_id(1)
    nkb = qi + 1

    lane = lax.broadcasted_iota(jnp.int32, (QB, LANES), 1)
    for p in range(IDX_HEADS // 2):
        pair = iq_ref[:, p * LANES:(p + 1) * LANES].astype(F32)
        iqm_scr[2 * p] = jnp.where(lane < IDX_DIM, pair, 0.0).astype(BF16)
        iqm_scr[2 * p + 1] = jnp.where(lane >= IDX_DIM, pair, 0.0).astype(BF16)
    iwt = iwt_ref[...]

    def to_bf16(int_plane):
        return int_plane.astype(F32).astype(BF16)

    def idx_tile(kb, admissible=None):
        s0 = pl.multiple_of(kb * KB, KB)
        kid = ikd_ref[pl.ds(s0, KB), :]
        acc = jnp.zeros((KB, QB), F32)
        for h in range(IDX_HEADS):
            sc = lax.dot_general(kid, iqm_scr[h], NT_DIMS, preferred_element_type=F32)
            acc = acc + jnp.maximum(sc, 0.0) * iwt[h:h + 1, :]
        key = _to_key(acc)
        top = (key >> 24) + 128
        if admissible is not None:
            key = jnp.where(admissible, key, jnp.int32(INT_MIN))
            top = jnp.where(admissible, top, -1)
        keys_scr[kb] = key
        dig_scr[0, kb] = to_bf16(top)
        for d in range(1, 4):
            dig_scr[d, kb] = to_bf16((key >> (24 - 8 * d)) & 255)

    def idx_body(kb, c):
        idx_tile(kb)
        return c

    lax.fori_loop(0, qi, idx_body, 0)
    rr = lax.broadcasted_iota(jnp.int32, (KB, QB), 0)
    cc = lax.broadcasted_iota(jnp.int32, (KB, QB), 1)
    idx_tile(qi, (rr // CHUNK) <= (cc // CHUNK))

    nkb_sel = jnp.where(qi > 0, nkb, 0)
    one_b = jnp.ones((), BF16)
    zero_b = jnp.zeros((), BF16)

    def count_ge(plane_ref, cand):
        cand_b = cand.astype(BF16)

        def body(kb, acc):
            part = jnp.where(plane_ref[kb] >= cand_b, one_b, zero_b)
            slabs = [part[16 * g:16 * (g + 1)] for g in range(KB // 16)]
            while len(slabs) > 1:
                slabs = [slabs[i] + slabs[i + 1] for i in range(0, len(slabs), 2)]
            return acc + slabs[0]

        acc = lax.fori_loop(0, nkb_sel, body, jnp.zeros((16, QB), BF16))
        return jnp.sum(acc.astype(F32), axis=0, keepdims=True)

    def byte_level(plane_ref, need):
        def bit_body(it, carry):
            t, cnt_rejected = carry
            cand = t + jnp.left_shift(jnp.int32(1), 7 - it).astype(F32)
            cnt = count_ge(plane_ref, cand)
            ok = cnt >= need
            return jnp.where(ok, cand, t), jnp.where(ok, cnt_rejected, cnt)

        zeros = jnp.zeros((1, QB), F32)
        t, cnt_above = lax.fori_loop(0, jnp.where(qi > 0, 8, 0), bit_body, (zeros, zeros))
        return t, need - cnt_above

    def remask(src_ref, t, digit_ref):
        t_b = t.astype(BF16)

        def body(kb, c):
            dm_scr[kb] = jnp.where(src_ref[kb] == t_b, digit_ref[kb], -one_b)
            return c

        lax.fori_loop(0, nkb_sel, body, 0)

    need = jnp.full((1, QB), TOPK, F32)
    t1, need = byte_level(dig_scr.at[0], need)
    remask(dig_scr.at[0], t1, dig_scr.at[1])
    t2, need = byte_level(dm_scr, need)
    remask(dm_scr, t2, dig_scr.at[2])
    t3, need = byte_level(dm_scr, need)
    remask(dm_scr, t3, dig_scr.at[3])
    t4, _ = byte_level(dm_scr, need)
    thr_bytes = [t.astype(jnp.int32) for t in (t1, t2, t3, t4)]
    thr = (((thr_bytes[0] - 128) << 24) | (thr_bytes[1] << 16) | (thr_bytes[2] << 8) | thr_bytes[3])
    thr = jnp.where(qi > 0, thr, jnp.int32(INT_MIN + 1))

    def count_where(pred_fn):
        def body(kb, acc):
            part = jnp.where(pred_fn(kb, keys_scr[kb]), 1.0, 0.0)
            return acc + jnp.sum(part.reshape(KB // 8, 8, QB), axis=0)
        acc = lax.fori_loop(0, nkb, body, jnp.zeros((8, QB), F32))
        return jnp.sum(acc, axis=0, keepdims=True)

    cnt_last = count_ge(dm_scr, t4)
    has_tie = jnp.logical_and(qi > 0, jnp.max(cnt_last - need) > 0)

    @pl.when(has_tie)
    def _():
        cnt_gt = count_where(lambda kb, kk: kk > thr)
        need = TOPK - cnt_gt

        def key_index(kb):
            return kb * KB + rr

        def jb_body(it, j0):
            cand = j0 | jnp.left_shift(jnp.int32(1), 11 - it)
            f = count_where(lambda kb, kk: jnp.logical_and(kk == thr, key_index(kb) < cand))
            return jnp.where(f < need, cand, j0)

        j0 = lax.fori_loop(0, 12, jb_body, jnp.zeros((1, QB), jnp.int32))
        jstar = j0 + 1

        def fix_body(kb, c):
            kk = keys_scr[kb]
            drop = jnp.logical_and(kk == thr, key_index(kb) >= jstar)
            keys_scr[kb] = jnp.where(drop, kk - 1, kk)
            return c

        lax.fori_loop(0, nkb, fix_body, 0)

    m_scr[...] = jnp.full(m_scr.shape, NEG, F32)
    l_scr[...] = jnp.zeros(l_scr.shape, F32)
    acc_scr[...] = jnp.zeros(acc_scr.shape, F32)
    posq = posq_ref[...]
    pmin_q = pmin_ref[b, qi]
    half = N_BUCKETS // 2
    max_exact = half // 2

    def att_tile(kb, near):
        s0 = pl.multiple_of(kb * KB, KB)
        madd = jnp.where(keys_scr[kb] >= thr, 0.0, NEG)
        for h in range(A_HEADS):
            hs = slice(h * A_HEAD_DIM, (h + 1) * A_HEAD_DIM)
            s_scr[h] = lax.dot_general(k_ref[pl.ds(s0, KB), hs], q_ref[:, hs], NT_DIMS,
                                       preferred_element_type=F32)
        if near:
            pk = posk_ref[pl.ds(s0, KB), :]
            rel = jnp.concatenate([pk] * (QB // LANES), axis=1) - posq
            n = jnp.abs(rel)
            nf = jnp.maximum(n, 1).astype(F32)
            large = max_exact + (jnp.log(nf / max_exact) / math.log(MAX_DISTANCE / max_exact)
                                 * (half - max_exact)).astype(jnp.int32)
            large = jnp.minimum(large, half - 1)
            bucket = jnp.where(rel > 0, half, 0) + jnp.where(n < max_exact, n, large)
        for h in range(A_HEADS):
            if near:
                row = jnp.broadcast_to(tab_ref[h:h + 1, :], (KB, LANES))
                bias = jnp.concatenate(
                    [jnp.take_along_axis(row, bucket[:, g * LANES:(g + 1) * LANES], axis=1)
                     for g in range(QB // LANES)], axis=1)
                s = s_scr[h] + ((bias - far_ref[h]) * LOG2E + madd)
            else:
                s = s_scr[h] + madd
            m_old = m_scr[h]
            m_new = jnp.maximum(m_old, jnp.max(s, axis=0, keepdims=True))
            alpha = jnp.exp2(m_old - m_new)
            p = jnp.exp2(s - m_new)
            l_scr[h] = alpha * l_scr[h] + jnp.sum(p, axis=0, keepdims=True)
            m_scr[h] = m_new
            alpha_scr[h] = alpha
            p_scr[h] = p.astype(BF16)
        for h in range(A_HEADS):
            hs = slice(h * A_HEAD_DIM, (h + 1) * A_HEAD_DIM)
            pv = jnp.dot(vt_ref[kb, hs, :], p_scr[h], preferred_element_type=F32)
            acc_scr[h] = alpha_scr[h] * acc_scr[h] + pv

    def att_body(kb, c):
        far = (pmin_q - pmax_ref[b, kb]) >= MAX_DISTANCE
        pl.when(far)(functools.partial(att_tile, kb, False))
        pl.when(jnp.logical_not(far))(functools.partial(att_tile, kb, True))
        return c

    lax.fori_loop(0, nkb, att_body, 0)

    for h in range(A_HEADS):
        hs = slice(h * A_HEAD_DIM, (h + 1) * A_HEAD_DIM)
        z = az_ref[:, hs].astype(F32)
        out_t = acc_scr[h] / l_scr[h]
        o_ref[:, hs] = (out_t.T * (z * jax.nn.sigmoid(z))).astype(o_ref.dtype)


def _dsa(pa, vt, iq, ikd, iwt, positions, rel_bias):
    pa3 = pa.reshape(BATCH, SEQ, PA_WIDTH)
    iq3 = iq.reshape(BATCH, SEQ, A_WIDTH)
    ikd3 = ikd.reshape(BATCH, SEQ, LANES)
    posq = positions.reshape(BATCH, 1, SEQ)
    posk = jnp.broadcast_to(positions[:, :, None], (BATCH, SEQ, LANES))
    pblk = positions.reshape(BATCH, NKB, KB)
    pmin = jnp.min(pblk, axis=-1)
    pmax = jnp.max(pblk, axis=-1)
    tab = jnp.zeros((A_HEADS, LANES), F32).at[:, :N_BUCKETS].set(rel_bias.astype(F32).T)
    far = rel_bias[N_BUCKETS // 2 - 1, :].astype(F32)

    grid_spec = pltpu.PrefetchScalarGridSpec(
        num_scalar_prefetch=2,
        grid=(BATCH, NQB),
        in_specs=[
            pl.BlockSpec((None, QB, A_WIDTH), lambda b, i, *_: (b, i, PA_Q)),
            pl.BlockSpec((None, SEQ, A_WIDTH), lambda b, i, *_: (b, 0, PA_K),
                         pipeline_mode=pl.Buffered(1)),
            pl.BlockSpec((None, NKB, A_WIDTH, KB), lambda b, i, *_: (b, 0, 0, 0),
                         pipeline_mode=pl.Buffered(1)),
            pl.BlockSpec((None, QB, A_WIDTH), lambda b, i, *_: (b, i, 0)),
            pl.BlockSpec((None, SEQ, LANES), lambda b, i, *_: (b, 0, 0)),
            pl.BlockSpec((IDX_HEADS, QB), lambda b, i, *_: (0, b * NQB + i)),
            pl.BlockSpec((None, 1, QB), lambda b, i, *_: (b, 0, i)),
            pl.BlockSpec((None, SEQ, LANES), lambda b, i, *_: (b, 0, 0)),
            pl.BlockSpec((None, QB, A_WIDTH), lambda b, i, *_: (b, i, PA_AZ)),
            pl.BlockSpec((A_HEADS, LANES), lambda b, i, *_: (0, 0)),
            pl.BlockSpec(memory_space=pltpu.SMEM),
        ],
        out_specs=pl.BlockSpec((None, QB, A_WIDTH), lambda b, i, *_: (b, i, 0)),
        scratch_shapes=[
            pltpu.VMEM((NKB, KB, QB), jnp.int32),
            pltpu.VMEM((4, NKB, KB, QB), BF16),
            pltpu.VMEM((NKB, KB, QB), BF16),
            pltpu.VMEM((IDX_HEADS, QB, LANES), BF16),
            pltpu.VMEM((A_HEADS, A_HEAD_DIM, QB), F32),
            pltpu.VMEM((A_HEADS, 1, QB), F32),
            pltpu.VMEM((A_HEADS, 1, QB), F32),
            pltpu.VMEM((A_HEADS, 1, QB), F32),
            pltpu.VMEM((A_HEADS, KB, QB), F32),
            pltpu.VMEM((A_HEADS, KB, QB), BF16),
        ],
    )
    out = pl.pallas_call(
        _dsa_kernel,
        grid_spec=grid_spec,
        out_shape=jax.ShapeDtypeStruct((BATCH, SEQ, A_WIDTH), BF16),
        compiler_params=_params(2),
        name="dsa",
    )(pmin, pmax, pa3, pa3, vt, iq3, ikd3, iwt, posq, posk, pa3, tab, far)
    return out.reshape(TOKENS, A_WIDTH)


def _ret_kernel(cdec_ref, q_ref, k_ref, v_ref, z_ref, gain_ref, dec_ref, te_ref, fs_ref, o_ref,
                state_scr):
    @pl.when(pl.program_id(1) == 0)
    def _():
        state_scr[...] = jnp.zeros(state_scr.shape, F32)

    heads = [slice(h * R_KEY_DIM, (h + 1) * R_KEY_DIM) for h in range(R_HEADS)]

    def chunk_body(c, carry):
        r0 = pl.multiple_of(c * CHUNK, CHUNK)
        rows = pl.ds(r0, CHUNK)
        scores, cross, kv = [], [], []
        for h, hs in enumerate(heads):
            q = q_ref[rows, hs]
            k = k_ref[rows, hs]
            scores.append(lax.dot_general(q, k, NT_DIMS, preferred_element_type=F32))
            qs = (q.astype(F32) * fs_ref[h]).astype(BF16)
            cross.append(jnp.dot(qs, state_scr[h].astype(BF16), preferred_element_type=F32))
            ke = (k.astype(F32) * te_ref[h]).T.astype(BF16)
            kv.append(jnp.dot(ke, v_ref[rows, hs], preferred_element_type=F32))
        for h in range(R_HEADS):
            state_scr[h] = state_scr[h] * cdec_ref[h] + kv[h]
        ys = []
        for h, hs in enumerate(heads):
            sc = (scores[h] * dec_ref[h]).astype(BF16)
            ys.append(jnp.dot(sc, v_ref[rows, hs], preferred_element_type=F32) + cross[h])
        for h, hs in enumerate(heads):
            y = ys[h]
            mu = jnp.mean(y, axis=-1, keepdims=True)
            var = jnp.mean((y - mu) ** 2, axis=-1, keepdims=True)
            yn = (y - mu) * lax.rsqrt(var + EPS) * gain_ref[:, hs]
            z = z_ref[rows, hs].astype(F32)
            o_ref[rows, hs] = (yn * (z * jax.nn.sigmoid(z))).astype(o_ref.dtype)
        return carry

    lax.fori_loop(0, RB // CHUNK, chunk_body, 0)


def _retention(rqk, pb, gn_gain):
    log_g = jnp.log(1.0 - 2.0 ** (-5.0 - jnp.arange(R_HEADS, dtype=F32)))
    pos = jnp.arange(CHUNK, dtype=F32)
    dist = jnp.abs(pos[:, None] - pos[None, :])
    intra_decay = jnp.exp(log_g[:, None, None] * dist)
    to_end = jnp.exp(log_g[:, None] * (CHUNK - 1.0 - pos)[None, :])
    from_start = jnp.exp(log_g[:, None] * (pos + 1.0)[None, :])
    chunk_decay = jnp.exp(log_g * CHUNK)
    te = jnp.broadcast_to(to_end[:, :, None], (R_HEADS, CHUNK, R_KEY_DIM))
    fs = jnp.broadcast_to(from_start[:, :, None], (R_HEADS, CHUNK, R_KEY_DIM))

    rqk3 = rqk.reshape(BATCH, SEQ, 2 * R_WIDTH)
    pb3 = pb.reshape(BATCH, SEQ, PB_WIDTH)
    out = pl.pallas_call(
        _ret_kernel,
        grid=(BATCH, SEQ // RB),
        in_specs=[
            pl.BlockSpec(memory_space=pltpu.SMEM),
            pl.BlockSpec((None, RB, R_WIDTH), lambda b, i: (b, i, 0)),
            pl.BlockSpec((None, RB, R_WIDTH), lambda b, i: (b, i, 1)),
            pl.BlockSpec((None, RB, R_WIDTH), lambda b, i: (b, i, PB_RV)),
            pl.BlockSpec((None, RB, R_WIDTH), lambda b, i: (b, i, PB_RZ)),
            pl.BlockSpec((1, R_WIDTH), lambda b, i: (0, 0)),
            pl.BlockSpec((R_HEADS, CHUNK, CHUNK), lambda b, i: (0, 0, 0)),
            pl.BlockSpec((R_HEADS, CHUNK, R_KEY_DIM), lambda b, i: (0, 0, 0)),
            pl.BlockSpec((R_HEADS, CHUNK, R_KEY_DIM), lambda b, i: (0, 0, 0)),
        ],
        out_specs=pl.BlockSpec((None, RB, R_WIDTH), lambda b, i: (b, i, 0)),
        out_shape=jax.ShapeDtypeStruct((BATCH, SEQ, R_WIDTH), BF16),
        scratch_shapes=[pltpu.VMEM((R_HEADS, R_KEY_DIM, R_VAL_DIM), F32)],
        compiler_params=_params(2),
        name="retention",
    )(chunk_decay, rqk3, rqk3, pb3, pb3, gn_gain.reshape(1, R_WIDTH), intra_decay, te, fs)
    return out.reshape(TOKENS, R_WIDTH)


def _out_kernel(a_ref, b_ref, ga_ref, gb_ref, x_ref, p_ref, wa_ref, wb_ref, wo_ref, wp_ref, wg_ref,
                fg_ref, o_ref):
    ta = jnp.dot(a_ref[...], wa_ref[...], preferred_element_type=F32)
    tb = jnp.dot(b_ref[...], wb_ref[...], preferred_element_type=F32)
    merged = (jax.nn.sigmoid(ga_ref[...].astype(F32)) * ta
              + jax.nn.sigmoid(gb_ref[...].astype(F32)) * tb)
    r = x_ref[...] + jnp.dot(merged.astype(BF16), wo_ref[...], preferred_element_type=F32)
    u = jnp.dot(p_ref[...].astype(BF16), wp_ref[...], preferred_element_type=F32)
    g = jnp.dot(r.astype(BF16), wg_ref[...], preferred_element_type=F32)
    y = r + u * jax.nn.sigmoid(g)
    ms = jnp.mean(y * y, axis=-1, keepdims=True)
    o_ref[...] = y * lax.rsqrt(ms + EPS) * fg_ref[...]


def _output(a_out, b_out, pb, x2d, p2d, wa, wb, wo, wp, wg, final_gain, tm=256):
    def resident(shape):
        return pl.BlockSpec(shape, lambda i: (0, 0), pipeline_mode=pl.Buffered(1))

    return pl.pallas_call(
        _out_kernel,
        grid=(TOKENS // tm,),
        in_specs=[pl.BlockSpec((tm, A_WIDTH), lambda i: (i, 0)),
                  pl.BlockSpec((tm, R_WIDTH), lambda i: (i, 0)),
                  pl.BlockSpec((tm, D_MODEL), lambda i: (i, PB_GA * R_WIDTH // D_MODEL)),
                  pl.BlockSpec((tm, D_MODEL), lambda i: (i, PB_GB * R_WIDTH // D_MODEL)),
                  pl.BlockSpec((tm, D_MODEL), lambda i: (i, 0)),
                  pl.BlockSpec((tm, PLE_DIM), lambda i: (i, 0)),
                  resident((A_WIDTH, D_MODEL)),
                  resident((R_WIDTH, D_MODEL)),
                  resident((D_MODEL, D_MODEL)),
                  resident((PLE_DIM, D_MODEL)),
                  resident((D_MODEL, D_MODEL)),
                  pl.BlockSpec((1, D_MODEL), lambda i: (0, 0))],
        out_specs=pl.BlockSpec((tm, D_MODEL), lambda i: (i, 0)),
        out_shape=jax.ShapeDtypeStruct((TOKENS, D_MODEL), F32),
        compiler_params=_params(1),
        name="output",
    )(a_out, b_out, pb, pb, x2d, p2d, wa, wb, wo, wp, wg, final_gain.reshape(1, D_MODEL))


def kernel(x, p, positions, w_in, norm_gain, w_a_out, w_b_out, w_o, ret_gn_gain, w_ple, w_ple_gate,
           rel_bias, final_gain):
    assert x.shape == (BATCH, SEQ, D_MODEL) and w_in.shape == (1, D_MODEL, IN_WIDTH)
    x2d = x.reshape(TOKENS, D_MODEL)
    p2d = p[0].reshape(TOKENS, PLE_DIM)
    wt = jnp.swapaxes(w_in[0], 0, 1)
    q_scale = jnp.where(jnp.arange(PA_WIDTH) < A_WIDTH, A_HEAD_DIM ** -0.5 * LOG2E, 1.0)
    q_scale = q_scale.astype(F32).reshape(1, PA_WIDTH)

    h = _rmsnorm(x2d, norm_gain[0])
    pa = _proj(h, wt, "proj_att", 0, PA_WIDTH, scale=q_scale,
               skip=(2 * A_WIDTH, 3 * A_WIDTH))
    vt = _proj_vt(h, wt, 2 * A_WIDTH)
    iq = _proj(h, wt, "proj_iq", COL_IQ, IDX_HEADS * IDX_DIM, scale=IDX_DIM ** -0.5)
    pb = _proj(h, wt, "proj_ret", COL_RV, PB_WIDTH)
    rqk = _proj_rope(h, wt, positions)
    ikd, iwt = _proj_idx(h, wt)

    a_out = _dsa(pa, vt, iq, ikd, iwt, positions, rel_bias)
    b_out = _retention(rqk, pb, ret_gn_gain[0])

    out = _output(a_out, b_out, pb, x2d, p2d, w_a_out[0].astype(BF16), w_b_out[0].astype(BF16),
                  w_o[0].astype(BF16), w_ple[0].astype(BF16), w_ple_gate[0].astype(BF16), final_gain)
    return out.reshape(BATCH, SEQ, D_MODEL)
```

```python
import functools
import math

import jax
import jax.numpy as jnp
from jax import lax
from jax.experimental import pallas as pl
from jax.experimental.pallas import tpu as pltpu

D_MODEL = 2048
BATCH = 4
SEQ = 4096
TOKENS = BATCH * SEQ
CHUNK = 64
PLE_DIM = 256
EPS = 1e-6
A_HEADS = 8
A_HEAD_DIM = 128
A_WIDTH = A_HEADS * A_HEAD_DIM
IDX_HEADS = 16
IDX_DIM = 64
TOPK = min(256, SEQ // 4)
R_HEADS = 8
R_KEY_DIM = 128
R_VAL_DIM = 128
R_WIDTH = R_HEADS * R_VAL_DIM
ROPE_BASE = 10000.0
N_BUCKETS = 32
MAX_DISTANCE = 128

COL_IQ = 4 * A_WIDTH
COL_IK = COL_IQ + IDX_HEADS * IDX_DIM
COL_IW = COL_IK + IDX_DIM
COL_RQ = COL_IW + IDX_HEADS
COL_RV = COL_RQ + 2 * R_WIDTH
IN_WIDTH = COL_RV + 2 * R_WIDTH + 2 * D_MODEL

PA_Q, PA_K, PA_AZ = 0, 1, 2
PA_WIDTH = 3 * A_WIDTH
PB_RV, PB_RZ, PB_GA, PB_GB = 0, 1, 2, 4
PB_WIDTH = 2 * R_WIDTH + 2 * D_MODEL

LANES = 128
QB = 256
KB = 256
NQB = SEQ // QB
NKB = SEQ // KB
RB = 512
NEG = -1e30
INT_MIN = -(2 ** 31)
LOG2E = math.log2(math.e)
VMEM_LIMIT = 56 * 1024 * 1024

F32 = jnp.float32
BF16 = jnp.bfloat16
NT_DIMS = (((1,), (1,)), ((), ()))


def _params(n_axes):
    return pltpu.CompilerParams(dimension_semantics=("arbitrary",) * n_axes,
                                vmem_limit_bytes=VMEM_LIMIT)


def _rmsnorm_kernel(x_ref, g_ref, o_ref):
    x = x_ref[...]
    ms = jnp.mean(x * x, axis=-1, keepdims=True)
    o_ref[...] = (x * lax.rsqrt(ms + EPS) * g_ref[...]).astype(o_ref.dtype)


def _rmsnorm(x2d, gain, tm=512):
    return pl.pallas_call(
        _rmsnorm_kernel,
        grid=(TOKENS // tm,),
        in_specs=[pl.BlockSpec((tm, D_MODEL), lambda i: (i, 0)),
                  pl.BlockSpec((1, D_MODEL), lambda i: (0, 0))],
        out_specs=pl.BlockSpec((tm, D_MODEL), lambda i: (i, 0)),
        out_shape=jax.ShapeDtypeStruct((TOKENS, D_MODEL), BF16),
        compiler_params=_params(1),
        name="rmsnorm",
    )(x2d, gain.reshape(1, D_MODEL))


def _wt_block(rows, row_of):
    assert rows % 8 == 0
    return pl.BlockSpec((pl.Element(rows), pl.Element(D_MODEL)),
                        lambda *g: (pl.multiple_of(row_of(*g), 8), 0))


def _proj_kernel(h_ref, wt_ref, *rest, scale):
    acc = lax.dot_general(h_ref[...], wt_ref[...].astype(BF16), NT_DIMS,
                          preferred_element_type=F32)
    if scale == "row":
        acc = acc * rest[0][...]
    elif scale is not None:
        acc = acc * scale
    rest[-1][...] = acc.astype(rest[-1].dtype)


def _proj(h, wt, name, row0, ncols, scale=None, skip=None, tm=2048, tn=512):
    def row_of(i, j):
        if skip is not None:
            j = jnp.where(j < skip[0] // tn, j, j + (skip[1] - skip[0]) // tn)
        return row0 + j * tn

    in_specs = [pl.BlockSpec((tm, D_MODEL), lambda i, j: (i, 0)), _wt_block(tn, row_of)]
    args = [h, wt]
    if scale is not None and not isinstance(scale, float):
        in_specs.append(pl.BlockSpec((1, tn), lambda i, j: (0, j)))
        args.append(scale)
        scale = "row"
    return pl.pallas_call(
        functools.partial(_proj_kernel, scale=scale),
        grid=(TOKENS // tm, ncols // tn),
        in_specs=in_specs,
        out_specs=pl.BlockSpec((tm, tn), lambda i, j: (i, j)),
        out_shape=jax.ShapeDtypeStruct((TOKENS, ncols), BF16),
        compiler_params=_params(2),
        name=name,
    )(*args)


def _proj_vt_kernel(h_ref, wt_ref, o_ref):
    acc_t = lax.dot_general(wt_ref[...].astype(BF16), h_ref[...], NT_DIMS,
                            preferred_element_type=F32)
    for t in range(o_ref.shape[0]):
        o_ref[t] = acc_t[:, t * KB:(t + 1) * KB].astype(o_ref.dtype)


def _proj_vt(h, wt, row0, tm=2048, tn=512):
    per_batch = SEQ // tm
    return pl.pallas_call(
        _proj_vt_kernel,
        grid=(TOKENS // tm, A_WIDTH // tn),
        in_specs=[pl.BlockSpec((tm, D_MODEL), lambda i, j: (i, 0)),
                  _wt_block(tn, lambda i, j: row0 + j * tn)],
        out_specs=pl.BlockSpec((None, tm // KB, tn, KB),
                               lambda i, j: (i // per_batch, i % per_batch, j, 0)),
        out_shape=jax.ShapeDtypeStruct((BATCH, NKB, A_WIDTH, KB), BF16),
        compiler_params=_params(2),
        name="proj_vt",
    )(h, wt)


def _proj_idx_kernel(h_ref, wik_ref, wiw_ref, ik_ref, iwt_ref):
    h = h_ref[...]
    wik = wik_ref[...].astype(BF16)
    ik = lax.dot_general(h, jnp.concatenate([wik, wik], axis=0), NT_DIMS,
                         preferred_element_type=F32)
    mu = jnp.mean(ik, axis=-1, keepdims=True)
    var = jnp.mean((ik - mu) ** 2, axis=-1, keepdims=True)
    ik_ref[...] = ((ik - mu) * lax.rsqrt(var + EPS)).astype(ik_ref.dtype)
    iwt = lax.dot_general(wiw_ref[...].astype(BF16), h, NT_DIMS, preferred_element_type=F32)
    iwt_ref[...] = iwt * (IDX_HEADS ** -0.5)


def _proj_idx(h, wt, tm=2048):
    return pl.pallas_call(
        _proj_idx_kernel,
        grid=(TOKENS // tm,),
        in_specs=[pl.BlockSpec((tm, D_MODEL), lambda i: (i, 0)),
                  _wt_block(IDX_DIM, lambda i: COL_IK),
                  _wt_block(IDX_HEADS, lambda i: COL_IW)],
        out_specs=[pl.BlockSpec((tm, LANES), lambda i: (i, 0)),
                   pl.BlockSpec((IDX_HEADS, tm), lambda i: (0, i))],
        out_shape=[jax.ShapeDtypeStruct((TOKENS, LANES), BF16),
                   jax.ShapeDtypeStruct((IDX_HEADS, TOKENS), F32)],
        compiler_params=_params(1),
        name="proj_idx",
    )(h, wt, wt)


def _proj_rope_kernel(h_ref, w_ref, pos_ref, freq_ref, sign_ref, o_ref, cos_scr, sin_scr, *,
                      tn, k_tile0, k_scale):
    @pl.when(pl.program_id(1) == 0)
    def _():
        ang = pos_ref[...].astype(F32) * freq_ref[...]
        cos_scr[...] = jnp.cos(ang)
        sin_scr[...] = jnp.sin(ang) * sign_ref[...]

    acc = lax.dot_general(h_ref[...], w_ref[...].astype(BF16), NT_DIMS,
                          preferred_element_type=F32)
    cos = cos_scr[...]
    sin = sin_scr[...]
    scale = jnp.where(pl.program_id(1) >= k_tile0, k_scale, 1.0).astype(F32)
    for g in range(tn // LANES):
        xg = acc[:, g * LANES:(g + 1) * LANES]
        rot = xg * cos + pltpu.roll(xg, LANES // 2, 1) * sin
        o_ref[:, g * LANES:(g + 1) * LANES] = (rot * scale).astype(o_ref.dtype)


def _proj_rope(h, wt, positions, tm=2048, tn=512):
    ncols = 2 * R_WIDTH
    half = R_KEY_DIM // 2
    inv_freq = ROPE_BASE ** (-jnp.arange(half, dtype=F32) / half)
    freq2 = jnp.concatenate([inv_freq, inv_freq]).reshape(1, R_KEY_DIM)
    sign2 = jnp.concatenate([-jnp.ones((half,), F32), jnp.ones((half,), F32)]).reshape(1, R_KEY_DIM)
    pos_b = jnp.broadcast_to(positions.reshape(TOKENS, 1), (TOKENS, R_KEY_DIM))
    kern = functools.partial(_proj_rope_kernel, tn=tn, k_tile0=R_WIDTH // tn,
                             k_scale=R_KEY_DIM ** -0.5)
    return pl.pallas_call(
        kern,
        grid=(TOKENS // tm, ncols // tn),
        in_specs=[pl.BlockSpec((tm, D_MODEL), lambda i, j: (i, 0)),
                  _wt_block(tn, lambda i, j: COL_RQ + j * tn),
                  pl.BlockSpec((tm, R_KEY_DIM), lambda i, j: (i, 0)),
                  pl.BlockSpec((1, R_KEY_DIM), lambda i, j: (0, 0)),
                  pl.BlockSpec((1, R_KEY_DIM), lambda i, j: (0, 0))],
        out_specs=pl.BlockSpec((tm, tn), lambda i, j: (i, j)),
        out_shape=jax.ShapeDtypeStruct((TOKENS, ncols), BF16),
        scratch_shapes=[pltpu.VMEM((tm, R_KEY_DIM), F32), pltpu.VMEM((tm, R_KEY_DIM), F32)],
        compiler_params=_params(2),
        name="proj_rope",
    )(h, wt, pos_b, freq2, sign2)


def _to_key(a):
    bits = pltpu.bitcast(a, jnp.int32)
    return jnp.where(bits < 0, bits ^ jnp.int32(0x7FFFFFFF), bits)


def _dsa_kernel(pmin_ref, pmax_ref,
                q_ref, k_ref, vt_ref, iq_ref, ikd_ref, iwt_ref, posq_ref, posk_ref, az_ref,
                tab_ref, far_ref, o_ref,
                keys_scr, dig_scr, dm_scr, iqm_scr, acc_scr, m_scr, l_scr, alpha_scr, s_scr, p_scr):
    b = pl.program_id(0)
    qi = pl.program_id(1)
    nkb = qi + 1

    lane = lax.broadcasted_iota(jnp.int32, (QB, LANES), 1)
    for p in range(IDX_HEADS // 2):
        pair = iq_ref[:, p * LANES:(p + 1) * LANES].astype(F32)
        iqm_scr[2 * p] = jnp.where(lane < IDX_DIM, pair, 0.0).astype(BF16)
        iqm_scr[2 * p + 1] = jnp.where(lane >= IDX_DIM, pair, 0.0).astype(BF16)
    iwt = iwt_ref[...]

    def to_bf16(int_plane):
        return int_plane.astype(F32).astype(BF16)

    rr = lax.broadcasted_iota(jnp.int32, (KB, QB), 0)
    cc = lax.broadcasted_iota(jnp.int32, (KB, QB), 1)

    def idx_scores(kb):
        s0 = pl.multiple_of(kb * KB, KB)
        kid = ikd_ref[pl.ds(s0, KB), :]
        acc = jnp.zeros((KB, QB), F32)
        for h in range(IDX_HEADS):
            sc = lax.dot_general(kid, iqm_scr[h], NT_DIMS, preferred_element_type=F32)
            acc = acc + jnp.maximum(sc, 0.0) * iwt[h:h + 1, :]
        return acc

    def idx_store(kb, acc, diagonal):
        key = _to_key(acc)
        top = (key >> 24) + 128
        if diagonal:
            admissible = (rr // CHUNK) <= (cc // CHUNK)
            key = jnp.where(admissible, key, jnp.int32(INT_MIN))
            top = jnp.where(admissible, top, -1)
        keys_scr[kb] = key
        dig_scr[0, kb] = to_bf16(top)
        for d in range(1, 4):
            dig_scr[d, kb] = to_bf16((key >> (24 - 8 * d)) & 255)

    def idx_tiles(kbs, diagonal_last):
        accs = [idx_scores(kb) for kb in kbs]
        for t, kb in enumerate(kbs):
            idx_store(kb, accs[t], diagonal_last and t == len(kbs) - 1)

    def idx_pair_body(j, c):
        idx_tiles([2 * j, 2 * j + 1], False)
        return c

    lax.fori_loop(0, lax.shift_right_logical(qi, 1), idx_pair_body, 0)
    qi_odd = (qi & 1) == 1
    pl.when(qi_odd)(lambda: idx_tiles([qi - 1, qi], True))
    pl.when(jnp.logical_not(qi_odd))(lambda: idx_tiles([qi], True))

    nkb_sel = jnp.where(qi > 0, nkb, 0)
    npair_sel = lax.shift_right_logical(nkb_sel + 1, 1)
    one_b = jnp.ones((), BF16)
    zero_b = jnp.zeros((), BF16)

    @pl.when((nkb & 1) == 1)
    def _():
        dig_scr[0, nkb] = jnp.full((KB, QB), -1, BF16)
        dm_scr[nkb] = jnp.full((KB, QB), -1, BF16)

    def count_ge(plane_ref, cand):
        cand_b = cand.astype(BF16)

        def body(j, acc):
            slabs = []
            for kb in (2 * j, 2 * j + 1):
                part = jnp.where(plane_ref[kb] >= cand_b, one_b, zero_b)
                slabs += [part[16 * g:16 * (g + 1)] for g in range(KB // 16)]
            while len(slabs) > 1:
                slabs = [slabs[i] + slabs[i + 1] for i in range(0, len(slabs), 2)]
            return acc + slabs[0]

        acc = lax.fori_loop(0, npair_sel, body, jnp.zeros((16, QB), BF16))
        return jnp.sum(acc.astype(F32), axis=0, keepdims=True)

    def byte_level(plane_ref, need):
        def bit_body(it, carry):
            t, cnt_rejected = carry
            cand = t + jnp.left_shift(jnp.int32(1), 7 - it).astype(F32)
            cnt = count_ge(plane_ref, cand)
            ok = cnt >= need
            return jnp.where(ok, cand, t), jnp.where(ok, cnt_rejected, cnt)

        zeros = jnp.zeros((1, QB), F32)
        t, cnt_above = lax.fori_loop(0, jnp.where(qi > 0, 8, 0), bit_body, (zeros, zeros))
        return t, need - cnt_above

    def remask(src_ref, t, digit_ref):
        t_b = t.astype(BF16)

        def body(kb, c):
            dm_scr[kb] = jnp.where(src_ref[kb] == t_b, digit_ref[kb], -one_b)
            return c

        lax.fori_loop(0, nkb_sel, body, 0)

    need = jnp.full((1, QB), TOPK, F32)
    t1, need = byte_level(dig_scr.at[0], need)
    remask(dig_scr.at[0], t1, dig_scr.at[1])
    t2, need = byte_level(dm_scr, need)
    remask(dm_scr, t2, dig_scr.at[2])
    t3, need = byte_level(dm_scr, need)
    remask(dm_scr, t3, dig_scr.at[3])
    t4, _ = byte_level(dm_scr, need)
    thr_bytes = [t.astype(jnp.int32) for t in (t1, t2, t3, t4)]
    thr = (((thr_bytes[0] - 128) << 24) | (thr_bytes[1] << 16) | (thr_bytes[2] << 8) | thr_bytes[3])
    thr = jnp.where(qi > 0, thr, jnp.int32(INT_MIN + 1))

    def count_where(pred_fn):
        def body(kb, acc):
            part = jnp.where(pred_fn(kb, keys_scr[kb]), 1.0, 0.0)
            return acc + jnp.sum(part.reshape(KB // 8, 8, QB), axis=0)
        acc = lax.fori_loop(0, nkb, body, jnp.zeros((8, QB), F32))
        return jnp.sum(acc, axis=0, keepdims=True)

    cnt_last = count_ge(dm_scr, t4)
    has_tie = jnp.logical_and(qi > 0, jnp.max(cnt_last - need) > 0)

    @pl.when(has_tie)
    def _():
        cnt_gt = count_where(lambda kb, kk: kk > thr)
        need = TOPK - cnt_gt

        def key_index(kb):
            return kb * KB + rr

        def jb_body(it, j0):
            cand = j0 | jnp.left_shift(jnp.int32(1), 11 - it)
            f = count_where(lambda kb, kk: jnp.logical_and(kk == thr, key_index(kb) < cand))
            return jnp.where(f < need, cand, j0)

        j0 = lax.fori_loop(0, 12, jb_body, jnp.zeros((1, QB), jnp.int32))
        jstar = j0 + 1

        def fix_body(kb, c):
            kk = keys_scr[kb]
            drop = jnp.logical_and(kk == thr, key_index(kb) >= jstar)
            keys_scr[kb] = jnp.where(drop, kk - 1, kk)
            return c

        lax.fori_loop(0, nkb, fix_body, 0)

    m_scr[...] = jnp.full(m_scr.shape, NEG, F32)
    l_scr[...] = jnp.zeros(l_scr.shape, F32)
    acc_scr[...] = jnp.zeros(acc_scr.shape, F32)
    posq = posq_ref[...]
    pmin_q = pmin_ref[b, qi]
    half = N_BUCKETS // 2
    max_exact = half // 2

    def att_tiles(kbs, near):
        for t, kb in enumerate(kbs):
            s0 = pl.multiple_of(kb * KB, KB)
            for h in range(A_HEADS):
                hs = slice(h * A_HEAD_DIM, (h + 1) * A_HEAD_DIM)
                s_scr[t, h] = lax.dot_general(k_ref[pl.ds(s0, KB), hs], q_ref[:, hs], NT_DIMS,
                                              preferred_element_type=F32)
        for t, kb in enumerate(kbs):
            s0 = pl.multiple_of(kb * KB, KB)
            madd = jnp.where(keys_scr[kb] >= thr, 0.0, NEG)
            if near:
                pk = posk_ref[pl.ds(s0, KB), :]
                rel = jnp.concatenate([pk] * (QB // LANES), axis=1) - posq
                n = jnp.abs(rel)
                nf = jnp.maximum(n, 1).astype(F32)
                large = max_exact + (jnp.log(nf / max_exact) / math.log(MAX_DISTANCE / max_exact)
                                     * (half - max_exact)).astype(jnp.int32)
                large = jnp.minimum(large, half - 1)
                bucket = jnp.where(rel > 0, half, 0) + jnp.where(n < max_exact, n, large)
            for h in range(A_HEADS):
                if near:
                    row = jnp.broadcast_to(tab_ref[h:h + 1, :], (KB, LANES))
                    bias = jnp.concatenate(
                        [jnp.take_along_axis(row, bucket[:, g * LANES:(g + 1) * LANES], axis=1)
                         for g in range(QB // LANES)], axis=1)
                    s = s_scr[t, h] + ((bias - far_ref[h]) * LOG2E + madd)
                else:
                    s = s_scr[t, h] + madd
                m_old = m_scr[h]
                m_new = jnp.maximum(m_old, jnp.max(s, axis=0, keepdims=True))
                alpha = jnp.exp2(m_old - m_new)
                p = jnp.exp2(s - m_new)
                l_scr[h] = alpha * l_scr[h] + jnp.sum(p, axis=0, keepdims=True)
                m_scr[h] = m_new
                alpha_scr[t, h] = alpha
                p_scr[t, h] = p.astype(BF16)
            for h in range(A_HEADS):
                hs = slice(h * A_HEAD_DIM, (h + 1) * A_HEAD_DIM)
                pv = jnp.dot(vt_ref[kb, hs, :], p_scr[t, h], preferred_element_type=F32)
                acc_scr[h] = alpha_scr[t, h] * acc_scr[h] + pv

    def is_far(kb):
        return (pmin_q - pmax_ref[b, kb]) >= MAX_DISTANCE

    def att_single(kb, c):
        far = is_far(kb)
        pl.when(far)(lambda: att_tiles([kb], False))
        pl.when(jnp.logical_not(far))(lambda: att_tiles([kb], True))
        return c

    def att_pair_body(j, c):
        kb0 = 2 * j
        kb1 = kb0 + 1
        both_far = jnp.logical_and(kb1 < nkb, jnp.logical_and(is_far(kb0), is_far(kb1)))
        pl.when(both_far)(lambda: att_tiles([kb0, kb1], False))

        @pl.when(jnp.logical_not(both_far))
        def _():
            lax.fori_loop(kb0, jnp.minimum(kb1 + 1, nkb), att_single, 0)

        return c

    lax.fori_loop(0, lax.shift_right_logical(nkb + 1, 1), att_pair_body, 0)

    for h in range(A_HEADS):
        hs = slice(h * A_HEAD_DIM, (h + 1) * A_HEAD_DIM)
        z = az_ref[:, hs].astype(F32)
        out_t = acc_scr[h] / l_scr[h]
        o_ref[:, hs] = (out_t.T * (z * jax.nn.sigmoid(z))).astype(o_ref.dtype)


def _dsa(pa, vt, iq, ikd, iwt, positions, rel_bias):
    pa3 = pa.reshape(BATCH, SEQ, PA_WIDTH)
    iq3 = iq.reshape(BATCH, SEQ, A_WIDTH)
    ikd3 = ikd.reshape(BATCH, SEQ, LANES)
    posq = positions.reshape(BATCH, 1, SEQ)
    posk = jnp.broadcast_to(positions[:, :, None], (BATCH, SEQ, LANES))
    pblk = positions.reshape(BATCH, NKB, KB)
    pmin = jnp.min(pblk, axis=-1)
    pmax = jnp.max(pblk, axis=-1)
    tab = jnp.zeros((A_HEADS, LANES), F32).at[:, :N_BUCKETS].set(rel_bias.astype(F32).T)
    far = rel_bias[N_BUCKETS // 2 - 1, :].astype(F32)

    grid_spec = pltpu.PrefetchScalarGridSpec(
        num_scalar_prefetch=2,
        grid=(BATCH, NQB),
        in_specs=[
            pl.BlockSpec((None, QB, A_WIDTH), lambda b, i, *_: (b, i, PA_Q)),
            pl.BlockSpec((None, SEQ, A_WIDTH), lambda b, i, *_: (b, 0, PA_K),
                         pipeline_mode=pl.Buffered(1)),
            pl.BlockSpec((None, NKB, A_WIDTH, KB), lambda b, i, *_: (b, 0, 0, 0),
                         pipeline_mode=pl.Buffered(1)),
            pl.BlockSpec((None, QB, A_WIDTH), lambda b, i, *_: (b, i, 0)),
            pl.BlockSpec((None, SEQ, LANES), lambda b, i, *_: (b, 0, 0)),
            pl.BlockSpec((IDX_HEADS, QB), lambda b, i, *_: (0, b * NQB + i)),
            pl.BlockSpec((None, 1, QB), lambda b, i, *_: (b, 0, i)),
            pl.BlockSpec((None, SEQ, LANES), lambda b, i, *_: (b, 0, 0)),
            pl.BlockSpec((None, QB, A_WIDTH), lambda b, i, *_: (b, i, PA_AZ)),
            pl.BlockSpec((A_HEADS, LANES), lambda b, i, *_: (0, 0)),
            pl.BlockSpec(memory_space=pltpu.SMEM),
        ],
        out_specs=pl.BlockSpec((None, QB, A_WIDTH), lambda b, i, *_: (b, i, 0)),
        scratch_shapes=[
            pltpu.VMEM((NKB, KB, QB), jnp.int32),
            pltpu.VMEM((4, NKB, KB, QB), BF16),
            pltpu.VMEM((NKB, KB, QB), BF16),
            pltpu.VMEM((IDX_HEADS, QB, LANES), BF16),
            pltpu.VMEM((A_HEADS, A_HEAD_DIM, QB), F32),
            pltpu.VMEM((A_HEADS, 1, QB), F32),
            pltpu.VMEM((A_HEADS, 1, QB), F32),
            pltpu.VMEM((2, A_HEADS, 1, QB), F32),
            pltpu.VMEM((2, A_HEADS, KB, QB), F32),
            pltpu.VMEM((2, A_HEADS, KB, QB), BF16),
        ],
    )
    out = pl.pallas_call(
        _dsa_kernel,
        grid_spec=grid_spec,
        out_shape=jax.ShapeDtypeStruct((BATCH, SEQ, A_WIDTH), BF16),
        compiler_params=_params(2),
        name="dsa",
    )(pmin, pmax, pa3, pa3, vt, iq3, ikd3, iwt, posq, posk, pa3, tab, far)
    return out.reshape(TOKENS, A_WIDTH)


def _ret_kernel(cdec_ref, q_ref, k_ref, v_ref, z_ref, gain_ref, dec_ref, te_ref, fs_ref, o_ref,
                state_scr):
    @pl.when(pl.program_id(1) == 0)
    def _():
        state_scr[...] = jnp.zeros(state_scr.shape, F32)

    heads = [slice(h * R_KEY_DIM, (h + 1) * R_KEY_DIM) for h in range(R_HEADS)]

    def chunk_body(c, carry):
        r0 = pl.multiple_of(c * CHUNK, CHUNK)
        rows = pl.ds(r0, CHUNK)
        scores, cross, kv = [], [], []
        for h, hs in enumerate(heads):
            q = q_ref[rows, hs]
            k = k_ref[rows, hs]
            scores.append(lax.dot_general(q, k, NT_DIMS, preferred_element_type=F32))
            qs = (q.astype(F32) * fs_ref[h]).astype(BF16)
            cross.append(jnp.dot(qs, state_scr[h].astype(BF16), preferred_element_type=F32))
            ke = (k.astype(F32) * te_ref[h]).T.astype(BF16)
            kv.append(jnp.dot(ke, v_ref[rows, hs], preferred_element_type=F32))
        for h in range(R_HEADS):
            state_scr[h] = state_scr[h] * cdec_ref[h] + kv[h]
        ys = []
        for h, hs in enumerate(heads):
            sc = (scores[h] * dec_ref[h]).astype(BF16)
            ys.append(jnp.dot(sc, v_ref[rows, hs], preferred_element_type=F32) + cross[h])
        for h, hs in enumerate(heads):
            y = ys[h]
            mu = jnp.mean(y, axis=-1, keepdims=True)
            var = jnp.mean((y - mu) ** 2, axis=-1, keepdims=True)
            yn = (y - mu) * lax.rsqrt(var + EPS) * gain_ref[:, hs]
            z = z_ref[rows, hs].astype(F32)
            o_ref[rows, hs] = (yn * (z * jax.nn.sigmoid(z))).astype(o_ref.dtype)
        return carry

    lax.fori_loop(0, RB // CHUNK, chunk_body, 0)


def _retention(rqk, pb, gn_gain):
    log_g = jnp.log(1.0 - 2.0 ** (-5.0 - jnp.arange(R_HEADS, dtype=F32)))
    pos = jnp.arange(CHUNK, dtype=F32)
    dist = jnp.abs(pos[:, None] - pos[None, :])
    intra_decay = jnp.exp(log_g[:, None, None] * dist)
    to_end = jnp.exp(log_g[:, None] * (CHUNK - 1.0 - pos)[None, :])
    from_start = jnp.exp(log_g[:, None] * (pos + 1.0)[None, :])
    chunk_decay = jnp.exp(log_g * CHUNK)
    te = jnp.broadcast_to(to_end[:, :, None], (R_HEADS, CHUNK, R_KEY_DIM))
    fs = jnp.broadcast_to(from_start[:, :, None], (R_HEADS, CHUNK, R_KEY_DIM))

    rqk3 = rqk.reshape(BATCH, SEQ, 2 * R_WIDTH)
    pb3 = pb.reshape(BATCH, SEQ, PB_WIDTH)
    out = pl.pallas_call(
        _ret_kernel,
        grid=(BATCH, SEQ // RB),
        in_specs=[
            pl.BlockSpec(memory_space=pltpu.SMEM),
            pl.BlockSpec((None, RB, R_WIDTH), lambda b, i: (b, i, 0)),
            pl.BlockSpec((None, RB, R_WIDTH), lambda b, i: (b, i, 1)),
            pl.BlockSpec((None, RB, R_WIDTH), lambda b, i: (b, i, PB_RV)),
            pl.BlockSpec((None, RB, R_WIDTH), lambda b, i: (b, i, PB_RZ)),
            pl.BlockSpec((1, R_WIDTH), lambda b, i: (0, 0)),
            pl.BlockSpec((R_HEADS, CHUNK, CHUNK), lambda b, i: (0, 0, 0)),
            pl.BlockSpec((R_HEADS, CHUNK, R_KEY_DIM), lambda b, i: (0, 0, 0)),
            pl.BlockSpec((R_HEADS, CHUNK, R_KEY_DIM), lambda b, i: (0, 0, 0)),
        ],
        out_specs=pl.BlockSpec((None, RB, R_WIDTH), lambda b, i: (b, i, 0)),
        out_shape=jax.ShapeDtypeStruct((BATCH, SEQ, R_WIDTH), BF16),
        scratch_shapes=[pltpu.VMEM((R_HEADS, R_KEY_DIM, R_VAL_DIM), F32)],
        compiler_params=_params(2),
        name="retention",
    )(chunk_decay, rqk3, rqk3, pb3, pb3, gn_gain.reshape(1, R_WIDTH), intra_decay, te, fs)
    return out.reshape(TOKENS, R_WIDTH)


def _out_kernel(a_ref, b_ref, ga_ref, gb_ref, x_ref, p_ref, wa_ref, wb_ref, wo_ref, wp_ref, wg_ref,
                fg_ref, o_ref):
    ta = jnp.dot(a_ref[...], wa_ref[...], preferred_element_type=F32)
    tb = jnp.dot(b_ref[...], wb_ref[...], preferred_element_type=F32)
    merged = (jax.nn.sigmoid(ga_ref[...].astype(F32)) * ta
              + jax.nn.sigmoid(gb_ref[...].astype(F32)) * tb)
    r = x_ref[...] + jnp.dot(merged.astype(BF16), wo_ref[...], preferred_element_type=F32)
    u = jnp.dot(p_ref[...].astype(BF16), wp_ref[...], preferred_element_type=F32)
    g = jnp.dot(r.astype(BF16), wg_ref[...], preferred_element_type=F32)
    y = r + u * jax.nn.sigmoid(g)
    ms = jnp.mean(y * y, axis=-1, keepdims=True)
    o_ref[...] = y * lax.rsqrt(ms + EPS) * fg_ref[...]


def _output(a_out, b_out, pb, x2d, p2d, wa, wb, wo, wp, wg, final_gain, tm=256):
    def resident(shape):
        return pl.BlockSpec(shape, lambda i: (0, 0), pipeline_mode=pl.Buffered(1))

    return pl.pallas_call(
        _out_kernel,
        grid=(TOKENS // tm,),
        in_specs=[pl.BlockSpec((tm, A_WIDTH), lambda i: (i, 0)),
                  pl.BlockSpec((tm, R_WIDTH), lambda i: (i, 0)),
                  pl.BlockSpec((tm, D_MODEL), lambda i: (i, PB_GA * R_WIDTH // D_MODEL)),
                  pl.BlockSpec((tm, D_MODEL), lambda i: (i, PB_GB * R_WIDTH // D_MODEL)),
                  pl.BlockSpec((tm, D_MODEL), lambda i: (i, 0)),
                  pl.BlockSpec((tm, PLE_DIM), lambda i: (i, 0)),
                  resident((A_WIDTH, D_MODEL)),
                  resident((R_WIDTH, D_MODEL)),
                  resident((D_MODEL, D_MODEL)),
                  resident((PLE_DIM, D_MODEL)),
                  resident((D_MODEL, D_MODEL)),
                  pl.BlockSpec((1, D_MODEL), lambda i: (0, 0))],
        out_specs=pl.BlockSpec((tm, D_MODEL), lambda i: (i, 0)),
        out_shape=jax.ShapeDtypeStruct((TOKENS, D_MODEL), F32),
        compiler_params=_params(1),
        name="output",
    )(a_out, b_out, pb, pb, x2d, p2d, wa, wb, wo, wp, wg, final_gain.reshape(1, D_MODEL))


def kernel(x, p, positions, w_in, norm_gain, w_a_out, w_b_out, w_o, ret_gn_gain, w_ple, w_ple_gate,
           rel_bias, final_gain):
    assert x.shape == (BATCH, SEQ, D_MODEL) and w_in.shape == (1, D_MODEL, IN_WIDTH)
    x2d = x.reshape(TOKENS, D_MODEL)
    p2d = p[0].reshape(TOKENS, PLE_DIM)
    wt = jnp.swapaxes(w_in[0], 0, 1)
    q_scale = jnp.where(jnp.arange(PA_WIDTH) < A_WIDTH, A_HEAD_DIM ** -0.5 * LOG2E, 1.0)
    q_scale = q_scale.astype(F32).reshape(1, PA_WIDTH)

    h = _rmsnorm(x2d, norm_gain[0])
    pa = _proj(h, wt, "proj_att", 0, PA_WIDTH, scale=q_scale,
               skip=(2 * A_WIDTH, 3 * A_WIDTH))
    vt = _proj_vt(h, wt, 2 * A_WIDTH)
    iq = _proj(h, wt, "proj_iq", COL_IQ, IDX_HEADS * IDX_DIM, scale=IDX_DIM ** -0.5)
    pb = _proj(h, wt, "proj_ret", COL_RV, PB_WIDTH)
    rqk = _proj_rope(h, wt, positions)
    ikd, iwt = _proj_idx(h, wt)

    a_out = _dsa(pa, vt, iq, ikd, iwt, positions, rel_bias)
    b_out = _retention(rqk, pb, ret_gn_gain[0])

    out = _output(a_out, b_out, pb, x2d, p2d, w_a_out[0].astype(BF16), w_b_out[0].astype(BF16),
                  w_o[0].astype(BF16), w_ple[0].astype(BF16), w_ple_gate[0].astype(BF16), final_gain)
    return out.reshape(BATCH, SEQ, D_MODEL)
```

```python
import functools
import math

import jax
import jax.numpy as jnp
from jax import lax
from jax.experimental import pallas as pl
from jax.experimental.pallas import tpu as pltpu

D_MODEL = 2048
BATCH = 4
SEQ = 4096
TOKENS = BATCH * SEQ
CHUNK = 64
PLE_DIM = 256
EPS = 1e-6
A_HEADS = 8
A_HEAD_DIM = 128
A_WIDTH = A_HEADS * A_HEAD_DIM
IDX_HEADS = 16
IDX_DIM = 64
TOPK = min(256, SEQ // 4)
R_HEADS = 8
R_KEY_DIM = 128
R_VAL_DIM = 128
R_WIDTH = R_HEADS * R_VAL_DIM
ROPE_BASE = 10000.0
N_BUCKETS = 32
MAX_DISTANCE = 128

COL_IQ = 4 * A_WIDTH
COL_IK = COL_IQ + IDX_HEADS * IDX_DIM
COL_IW = COL_IK + IDX_DIM
COL_RQ = COL_IW + IDX_HEADS
COL_RV = COL_RQ + 2 * R_WIDTH
IN_WIDTH = COL_RV + 2 * R_WIDTH + 2 * D_MODEL

PA_Q, PA_K, PA_AZ = 0, 1, 2
PA_WIDTH = 3 * A_WIDTH
PB_RV, PB_RZ, PB_GA, PB_GB = 0, 1, 2, 4
PB_WIDTH = 2 * R_WIDTH + 2 * D_MODEL

LANES = 128
QB = 256
KB = 256
NQB = SEQ // QB
NKB = SEQ // KB
RB = 512
NEG = -1e30
INT_MIN = -(2 ** 31)
LOG2E = math.log2(math.e)
VMEM_LIMIT = 56 * 1024 * 1024

F32 = jnp.float32
BF16 = jnp.bfloat16
NT_DIMS = (((1,), (1,)), ((), ()))


def _params(n_axes):
    return pltpu.CompilerParams(dimension_semantics=("arbitrary",) * n_axes,
                                vmem_limit_bytes=VMEM_LIMIT)


def _rmsnorm_kernel(x_ref, g_ref, o_ref):
    x = x_ref[...]
    ms = jnp.mean(x * x, axis=-1, keepdims=True)
    o_ref[...] = (x * lax.rsqrt(ms + EPS) * g_ref[...]).astype(o_ref.dtype)


def _rmsnorm(x2d, gain, tm=512):
    return pl.pallas_call(
        _rmsnorm_kernel,
        grid=(TOKENS // tm,),
        in_specs=[pl.BlockSpec((tm, D_MODEL), lambda i: (i, 0)),
                  pl.BlockSpec((1, D_MODEL), lambda i: (0, 0))],
        out_specs=pl.BlockSpec((tm, D_MODEL), lambda i: (i, 0)),
        out_shape=jax.ShapeDtypeStruct((TOKENS, D_MODEL), BF16),
        compiler_params=_params(1),
        name="rmsnorm",
    )(x2d, gain.reshape(1, D_MODEL))


def _wt_block(rows, row_of):
    assert rows % 8 == 0
    return pl.BlockSpec((pl.Element(rows), pl.Element(D_MODEL)),
                        lambda *g: (pl.multiple_of(row_of(*g), 8), 0))


def _proj_kernel(h_ref, wt_ref, *rest, scale):
    acc = lax.dot_general(h_ref[...], wt_ref[...].astype(BF16), NT_DIMS,
                          preferred_element_type=F32)
    if scale == "row":
        acc = acc * rest[0][...]
    elif scale is not None:
        acc = acc * scale
    rest[-1][...] = acc.astype(rest[-1].dtype)


def _proj(h, wt, name, row0, ncols, scale=None, skip=None, tm=2048, tn=512):
    def row_of(i, j):
        if skip is not None:
            j = jnp.where(j < skip[0] // tn, j, j + (skip[1] - skip[0]) // tn)
        return row0 + j * tn

    in_specs = [pl.BlockSpec((tm, D_MODEL), lambda i, j: (i, 0)), _wt_block(tn, row_of)]
    args = [h, wt]
    if scale is not None and not isinstance(scale, float):
        in_specs.append(pl.BlockSpec((1, tn), lambda i, j: (0, j)))
        args.append(scale)
        scale = "row"
    return pl.pallas_call(
        functools.partial(_proj_kernel, scale=scale),
        grid=(TOKENS // tm, ncols // tn),
        in_specs=in_specs,
        out_specs=pl.BlockSpec((tm, tn), lambda i, j: (i, j)),
        out_shape=jax.ShapeDtypeStruct((TOKENS, ncols), BF16),
        compiler_params=_params(2),
        name=name,
    )(*args)


def _proj_vt_kernel(h_ref, wt_ref, o_ref):
    acc_t = lax.dot_general(wt_ref[...].astype(BF16), h_ref[...], NT_DIMS,
                            preferred_element_type=F32)
    for t in range(o_ref.shape[0]):
        o_ref[t] = acc_t[:, t * KB:(t + 1) * KB].astype(o_ref.dtype)


def _proj_vt(h, wt, row0, tm=2048, tn=512):
    per_batch = SEQ // tm
    return pl.pallas_call(
        _proj_vt_kernel,
        grid=(TOKENS // tm, A_WIDTH // tn),
        in_specs=[pl.BlockSpec((tm, D_MODEL), lambda i, j: (i, 0)),
                  _wt_block(tn, lambda i, j: row0 + j * tn)],
        out_specs=pl.BlockSpec((None, tm // KB, tn, KB),
                               lambda i, j: (i // per_batch, i % per_batch, j, 0)),
        out_shape=jax.ShapeDtypeStruct((BATCH, NKB, A_WIDTH, KB), BF16),
        compiler_params=_params(2),
        name="proj_vt",
    )(h, wt)


def _proj_idx_kernel(h_ref, wik_ref, wiw_ref, ik_ref, iwt_ref):
    h = h_ref[...]
    wik = wik_ref[...].astype(BF16)
    ik = lax.dot_general(h, jnp.concatenate([wik, wik], axis=0), NT_DIMS,
                         preferred_element_type=F32)
    mu = jnp.mean(ik, axis=-1, keepdims=True)
    var = jnp.mean((ik - mu) ** 2, axis=-1, keepdims=True)
    ik_ref[...] = ((ik - mu) * lax.rsqrt(var + EPS)).astype(ik_ref.dtype)
    iwt = lax.dot_general(wiw_ref[...].astype(BF16), h, NT_DIMS, preferred_element_type=F32)
    iwt_ref[...] = iwt * (IDX_HEADS ** -0.5)


def _proj_idx(h, wt, tm=2048):
    return pl.pallas_call(
        _proj_idx_kernel,
        grid=(TOKENS // tm,),
        in_specs=[pl.BlockSpec((tm, D_MODEL), lambda i: (i, 0)),
                  _wt_block(IDX_DIM, lambda i: COL_IK),
                  _wt_block(IDX_HEADS, lambda i: COL_IW)],
        out_specs=[pl.BlockSpec((tm, LANES), lambda i: (i, 0)),
                   pl.BlockSpec((IDX_HEADS, tm), lambda i: (0, i))],
        out_shape=[jax.ShapeDtypeStruct((TOKENS, LANES), BF16),
                   jax.ShapeDtypeStruct((IDX_HEADS, TOKENS), F32)],
        compiler_params=_params(1),
        name="proj_idx",
    )(h, wt, wt)


def _proj_rope_kernel(h_ref, w_ref, pos_ref, freq_ref, o_ref, cos_scr, sin_scr, *,
                      tn, k_tile0, k_scale):
    @pl.when(pl.program_id(1) == 0)
    def _():
        hm = pos_ref.shape[0]
        ang = pos_ref[...].astype(F32) * freq_ref[...]
        c = jnp.cos(ang)
        s = jnp.sin(ang)
        cr = pltpu.roll(c, LANES // 2, 1)
        sr = pltpu.roll(s, LANES // 2, 1)
        low = lax.broadcasted_iota(jnp.int32, c.shape, 1) < LANES // 2
        cos_scr[:hm] = jnp.where(low, c, cr)
        cos_scr[hm:] = jnp.where(low, cr, c)
        sin_scr[:hm] = jnp.where(low, -s, sr)
        sin_scr[hm:] = jnp.where(low, -sr, s)

    acc = lax.dot_general(h_ref[...], w_ref[...].astype(BF16), NT_DIMS,
                          preferred_element_type=F32)
    cos = cos_scr[...]
    sin = sin_scr[...]
    scale = jnp.where(pl.program_id(1) >= k_tile0, k_scale, 1.0).astype(F32)
    for g in range(tn // LANES):
        xg = acc[:, g * LANES:(g + 1) * LANES]
        rot = xg * cos + pltpu.roll(xg, LANES // 2, 1) * sin
        o_ref[:, g * LANES:(g + 1) * LANES] = (rot * scale).astype(o_ref.dtype)


def _proj_rope(h, wt, positions, tm=2048, tn=512):
    ncols = 2 * R_WIDTH
    half = R_KEY_DIM // 2
    inv_freq = ROPE_BASE ** (-jnp.arange(half, dtype=F32) / half)
    freq2 = jnp.concatenate([inv_freq, inv_freq]).reshape(1, R_KEY_DIM)
    pos_t = positions.reshape(TOKENS // tm, 2, tm // 2, 1)
    pos2 = jnp.concatenate([jnp.broadcast_to(pos_t[:, 0], (TOKENS // tm, tm // 2, half)),
                            jnp.broadcast_to(pos_t[:, 1], (TOKENS // tm, tm // 2, half))], axis=-1)
    pos2 = pos2.reshape(TOKENS // 2, R_KEY_DIM)
    kern = functools.partial(_proj_rope_kernel, tn=tn, k_tile0=R_WIDTH // tn,
                             k_scale=R_KEY_DIM ** -0.5)
    return pl.pallas_call(
        kern,
        grid=(TOKENS // tm, ncols // tn),
        in_specs=[pl.BlockSpec((tm, D_MODEL), lambda i, j: (i, 0)),
                  _wt_block(tn, lambda i, j: COL_RQ + j * tn),
                  pl.BlockSpec((tm // 2, R_KEY_DIM), lambda i, j: (i, 0)),
                  pl.BlockSpec((1, R_KEY_DIM), lambda i, j: (0, 0))],
        out_specs=pl.BlockSpec((tm, tn), lambda i, j: (i, j)),
        out_shape=jax.ShapeDtypeStruct((TOKENS, ncols), BF16),
        scratch_shapes=[pltpu.VMEM((tm, R_KEY_DIM), F32), pltpu.VMEM((tm, R_KEY_DIM), F32)],
        compiler_params=_params(2),
        name="proj_rope",
    )(h, wt, pos2, freq2)


def _to_key(a):
    bits = pltpu.bitcast(a, jnp.int32)
    return jnp.where(bits < 0, bits ^ jnp.int32(0x7FFFFFFF), bits)


def _dsa_kernel(pmin_ref, pmax_ref,
                q_ref, k_ref, vt_ref, iq_ref, ikd_ref, iwt_ref, posq_ref, posk_ref, az_ref,
                tab_ref, far_ref, o_ref,
                keys_scr, dig_scr, dm_scr, iqm_scr, acc_scr, m_scr, l_scr, alpha_scr, s_scr, p_scr):
    b = pl.program_id(0)
    qi = pl.program_id(1)
    nkb = qi + 1

    lane = lax.broadcasted_iota(jnp.int32, (QB, LANES), 1)
    for p in range(IDX_HEADS // 2):
        pair = iq_ref[:, p * LANES:(p + 1) * LANES].astype(F32)
        iqm_scr[2 * p] = jnp.where(lane < IDX_DIM, pair, 0.0).astype(BF16)
        iqm_scr[2 * p + 1] = jnp.where(lane >= IDX_DIM, pair, 0.0).astype(BF16)
    iwt = iwt_ref[...]

    def to_bf16(int_plane):
        return int_plane.astype(F32).astype(BF16)

    rr = lax.broadcasted_iota(jnp.int32, (KB, QB), 0)
    cc = lax.broadcasted_iota(jnp.int32, (KB, QB), 1)

    def idx_scores(kb):
        s0 = pl.multiple_of(kb * KB, KB)
        kid = ikd_ref[pl.ds(s0, KB), :]
        acc = jnp.zeros((KB, QB), F32)
        for h in range(IDX_HEADS):
            sc = lax.dot_general(kid, iqm_scr[h], NT_DIMS, preferred_element_type=F32)
            acc = acc + jnp.maximum(sc, 0.0) * iwt[h:h + 1, :]
        return acc

    def idx_store(kb, acc, diagonal):
        key = _to_key(acc)
        top = (key >> 24) + 128
        if diagonal:
            admissible = (rr // CHUNK) <= (cc // CHUNK)
            key = jnp.where(admissible, key, jnp.int32(INT_MIN))
            top = jnp.where(admissible, top, -1)
        keys_scr[kb] = key
        dig_scr[0, kb] = to_bf16(top)
        for d in range(1, 4):
            dig_scr[d, kb] = to_bf16((key >> (24 - 8 * d)) & 255)

    def idx_tiles(kbs, diagonal_last):
        accs = [idx_scores(kb) for kb in kbs]
        for t, kb in enumerate(kbs):
            idx_store(kb, accs[t], diagonal_last and t == len(kbs) - 1)

    def idx_pair_body(j, c):
        idx_tiles([2 * j, 2 * j + 1], False)
        return c

    lax.fori_loop(0, lax.shift_right_logical(qi, 1), idx_pair_body, 0)
    qi_odd = (qi & 1) == 1
    pl.when(qi_odd)(lambda: idx_tiles([qi - 1, qi], True))
    pl.when(jnp.logical_not(qi_odd))(lambda: idx_tiles([qi], True))

    nkb_sel = jnp.where(qi > 0, nkb, 0)
    npair_sel = lax.shift_right_logical(nkb_sel + 1, 1)
    one_b = jnp.ones((), BF16)
    zero_b = jnp.zeros((), BF16)

    @pl.when((nkb & 1) == 1)
    def _():
        dig_scr[0, nkb] = jnp.full((KB, QB), -1, BF16)
        dm_scr[nkb] = jnp.full((KB, QB), -1, BF16)

    def count_ge(plane_ref, cand):
        cand_b = cand.astype(BF16)

        def body(j, acc):
            slabs = []
            for kb in (2 * j, 2 * j + 1):
                part = jnp.where(plane_ref[kb] >= cand_b, one_b, zero_b)
                slabs += [part[16 * g:16 * (g + 1)] for g in range(KB // 16)]
            while len(slabs) > 1:
                slabs = [slabs[i] + slabs[i + 1] for i in range(0, len(slabs), 2)]
            return acc + slabs[0]

        acc = lax.fori_loop(0, npair_sel, body, jnp.zeros((16, QB), BF16))
        return jnp.sum(acc.astype(F32), axis=0, keepdims=True)

    def byte_level(plane_ref, need):
        def bit_body(it, carry):
            t, cnt_rejected = carry
            cand = t + jnp.left_shift(jnp.int32(1), 7 - it).astype(F32)
            cnt = count_ge(plane_ref, cand)
            ok = cnt >= need
            return jnp.where(ok, cand, t), jnp.where(ok, cnt_rejected, cnt)

        zeros = jnp.zeros((1, QB), F32)
        t, cnt_above = lax.fori_loop(0, jnp.where(qi > 0, 8, 0), bit_body, (zeros, zeros))
        return t, need - cnt_above

    def remask(src_ref, t, digit_ref):
        t_b = t.astype(BF16)

        def body(kb, c):
            dm_scr[kb] = jnp.where(src_ref[kb] == t_b, digit_ref[kb], -one_b)
            return c

        lax.fori_loop(0, nkb_sel, body, 0)

    need = jnp.full((1, QB), TOPK, F32)
    t1, need = byte_level(dig_scr.at[0], need)
    remask(dig_scr.at[0], t1, dig_scr.at[1])
    t2, need = byte_level(dm_scr, need)
    remask(dm_scr, t2, dig_scr.at[2])
    t3, need = byte_level(dm_scr, need)
    remask(dm_scr, t3, dig_scr.at[3])
    t4, _ = byte_level(dm_scr, need)
    thr_bytes = [t.astype(jnp.int32) for t in (t1, t2, t3, t4)]
    thr = (((thr_bytes[0] - 128) << 24) | (thr_bytes[1] << 16) | (thr_bytes[2] << 8) | thr_bytes[3])
    thr = jnp.where(qi > 0, thr, jnp.int32(INT_MIN + 1))

    def count_where(pred_fn):
        def body(kb, acc):
            part = jnp.where(pred_fn(kb, keys_scr[kb]), 1.0, 0.0)
            return acc + jnp.sum(part.reshape(KB // 8, 8, QB), axis=0)
        acc = lax.fori_loop(0, nkb, body, jnp.zeros((8, QB), F32))
        return jnp.sum(acc, axis=0, keepdims=True)

    cnt_last = count_ge(dm_scr, t4)
    has_tie = jnp.logical_and(qi > 0, jnp.max(cnt_last - need) > 0)

    @pl.when(has_tie)
    def _():
        cnt_gt = count_where(lambda kb, kk: kk > thr)
        need = TOPK - cnt_gt

        def key_index(kb):
            return kb * KB + rr

        def jb_body(it, j0):
            cand = j0 | jnp.left_shift(jnp.int32(1), 11 - it)
            f = count_where(lambda kb, kk: jnp.logical_and(kk == thr, key_index(kb) < cand))
            return jnp.where(f < need, cand, j0)

        j0 = lax.fori_loop(0, 12, jb_body, jnp.zeros((1, QB), jnp.int32))
        jstar = j0 + 1

        def fix_body(kb, c):
            kk = keys_scr[kb]
            drop = jnp.logical_and(kk == thr, key_index(kb) >= jstar)
            keys_scr[kb] = jnp.where(drop, kk - 1, kk)
            return c

        lax.fori_loop(0, nkb, fix_body, 0)

    m_scr[...] = jnp.full(m_scr.shape, NEG, F32)
    l_scr[...] = jnp.zeros(l_scr.shape, F32)
    acc_scr[...] = jnp.zeros(acc_scr.shape, F32)
    posq = posq_ref[...]
    pmin_q = pmin_ref[b, qi]
    half = N_BUCKETS // 2
    max_exact = half // 2

    def att_tiles(kbs, near):
        for t, kb in enumerate(kbs):
            s0 = pl.multiple_of(kb * KB, KB)
            for h in range(A_HEADS):
                hs = slice(h * A_HEAD_DIM, (h + 1) * A_HEAD_DIM)
                s_scr[t, h] = lax.dot_general(k_ref[pl.ds(s0, KB), hs], q_ref[:, hs], NT_DIMS,
                                              preferred_element_type=F32)
        for t, kb in enumerate(kbs):
            s0 = pl.multiple_of(kb * KB, KB)
            madd = jnp.where(keys_scr[kb] >= thr, 0.0, NEG)
            if near:
                pk = posk_ref[pl.ds(s0, KB), :]
                rel = jnp.concatenate([pk] * (QB // LANES), axis=1) - posq
                n = jnp.abs(rel)
                nf = jnp.maximum(n, 1).astype(F32)
                large = max_exact + (jnp.log(nf / max_exact) / math.log(MAX_DISTANCE / max_exact)
                                     * (half - max_exact)).astype(jnp.int32)
                large = jnp.minimum(large, half - 1)
                bucket = jnp.where(rel > 0, half, 0) + jnp.where(n < max_exact, n, large)
            for h in range(A_HEADS):
                if near:
                    row = jnp.broadcast_to(tab_ref[h:h + 1, :], (KB, LANES))
                    bias = jnp.concatenate(
                        [jnp.take_along_axis(row, bucket[:, g * LANES:(g + 1) * LANES], axis=1)
                         for g in range(QB // LANES)], axis=1)
                    s = s_scr[t, h] + ((bias - far_ref[h]) * LOG2E + madd)
                else:
                    s = s_scr[t, h] + madd
                m_old = m_scr[h]
                m_new = jnp.maximum(m_old, jnp.max(s, axis=0, keepdims=True))
                alpha = jnp.exp2(m_old - m_new)
                p = jnp.exp2(s - m_new)
                l_scr[h] = alpha * l_scr[h] + jnp.sum(p, axis=0, keepdims=True)
                m_scr[h] = m_new
                alpha_scr[t, h] = alpha
                p_scr[t, h] = p.astype(BF16)
            for h in range(A_HEADS):
                hs = slice(h * A_HEAD_DIM, (h + 1) * A_HEAD_DIM)
                pv = jnp.dot(vt_ref[kb, hs, :], p_scr[t, h], preferred_element_type=F32)
                acc_scr[h] = alpha_scr[t, h] * acc_scr[h] + pv

    def is_far(kb):
        return (pmin_q - pmax_ref[b, kb]) >= MAX_DISTANCE

    def att_single(kb, c):
        far = is_far(kb)
        pl.when(far)(lambda: att_tiles([kb], False))
        pl.when(jnp.logical_not(far))(lambda: att_tiles([kb], True))
        return c

    def att_pair_body(j, c):
        kb0 = 2 * j
        kb1 = kb0 + 1
        both_far = jnp.logical_and(kb1 < nkb, jnp.logical_and(is_far(kb0), is_far(kb1)))
        pl.when(both_far)(lambda: att_tiles([kb0, kb1], False))

        @pl.when(jnp.logical_not(both_far))
        def _():
            lax.fori_loop(kb0, jnp.minimum(kb1 + 1, nkb), att_single, 0)

        return c

    lax.fori_loop(0, lax.shift_right_logical(nkb + 1, 1), att_pair_body, 0)

    for h in range(A_HEADS):
        hs = slice(h * A_HEAD_DIM, (h + 1) * A_HEAD_DIM)
        z = az_ref[:, hs].astype(F32)
        out_t = acc_scr[h] / l_scr[h]
        o_ref[:, hs] = (out_t.T * (z * jax.nn.sigmoid(z))).astype(o_ref.dtype)


def _dsa(pa, vt, iq, ikd, iwt, positions, rel_bias):
    pa3 = pa.reshape(BATCH, SEQ, PA_WIDTH)
    iq3 = iq.reshape(BATCH, SEQ, A_WIDTH)
    ikd3 = ikd.reshape(BATCH, SEQ, LANES)
    posq = positions.reshape(BATCH, 1, SEQ)
    posk = jnp.broadcast_to(positions[:, :, None], (BATCH, SEQ, LANES))
    pblk = positions.reshape(BATCH, NKB, KB)
    pmin = jnp.min(pblk, axis=-1)
    pmax = jnp.max(pblk, axis=-1)
    tab = jnp.zeros((A_HEADS, LANES), F32).at[:, :N_BUCKETS].set(rel_bias.astype(F32).T)
    far = rel_bias[N_BUCKETS // 2 - 1, :].astype(F32)

    grid_spec = pltpu.PrefetchScalarGridSpec(
        num_scalar_prefetch=2,
        grid=(BATCH, NQB),
        in_specs=[
            pl.BlockSpec((None, QB, A_WIDTH), lambda b, i, *_: (b, i, PA_Q)),
            pl.BlockSpec((None, SEQ, A_WIDTH), lambda b, i, *_: (b, 0, PA_K),
                         pipeline_mode=pl.Buffered(1)),
            pl.BlockSpec((None, NKB, A_WIDTH, KB), lambda b, i, *_: (b, 0, 0, 0),
                         pipeline_mode=pl.Buffered(1)),
            pl.BlockSpec((None, QB, A_WIDTH), lambda b, i, *_: (b, i, 0)),
            pl.BlockSpec((None, SEQ, LANES), lambda b, i, *_: (b, 0, 0)),
            pl.BlockSpec((IDX_HEADS, QB), lambda b, i, *_: (0, b * NQB + i)),
            pl.BlockSpec((None, 1, QB), lambda b, i, *_: (b, 0, i)),
            pl.BlockSpec((None, SEQ, LANES), lambda b, i, *_: (b, 0, 0)),
            pl.BlockSpec((None, QB, A_WIDTH), lambda b, i, *_: (b, i, PA_AZ)),
            pl.BlockSpec((A_HEADS, LANES), lambda b, i, *_: (0, 0)),
            pl.BlockSpec(memory_space=pltpu.SMEM),
        ],
        out_specs=pl.BlockSpec((None, QB, A_WIDTH), lambda b, i, *_: (b, i, 0)),
        scratch_shapes=[
            pltpu.VMEM((NKB, KB, QB), jnp.int32),
            pltpu.VMEM((4, NKB, KB, QB), BF16),
            pltpu.VMEM((NKB, KB, QB), BF16),
            pltpu.VMEM((IDX_HEADS, QB, LANES), BF16),
            pltpu.VMEM((A_HEADS, A_HEAD_DIM, QB), F32),
            pltpu.VMEM((A_HEADS, 1, QB), F32),
            pltpu.VMEM((A_HEADS, 1, QB), F32),
            pltpu.VMEM((2, A_HEADS, 1, QB), F32),
            pltpu.VMEM((2, A_HEADS, KB, QB), F32),
            pltpu.VMEM((2, A_HEADS, KB, QB), BF16),
        ],
    )
    out = pl.pallas_call(
        _dsa_kernel,
        grid_spec=grid_spec,
        out_shape=jax.ShapeDtypeStruct((BATCH, SEQ, A_WIDTH), BF16),
        compiler_params=_params(2),
        name="dsa",
    )(pmin, pmax, pa3, pa3, vt, iq3, ikd3, iwt, posq, posk, pa3, tab, far)
    return out.reshape(TOKENS, A_WIDTH)


def _ret_kernel(cdec_ref, q_ref, k_ref, v_ref, z_ref, gain_ref, dec_ref, te_ref, fs_ref, o_ref,
                state_scr):
    @pl.when(pl.program_id(1) == 0)
    def _():
        state_scr[...] = jnp.zeros(state_scr.shape, F32)

    heads = [slice(h * R_KEY_DIM, (h + 1) * R_KEY_DIM) for h in range(R_HEADS)]

    group = 4

    def chunk_group_body(g, carry):
        rows = [pl.ds(pl.multiple_of((g * group + t) * CHUNK, CHUNK), CHUNK) for t in range(group)]
        scores, kv = [], []
        for t in range(group):
            for h, hs in enumerate(heads):
                k = k_ref[rows[t], hs]
                scores.append(lax.dot_general(q_ref[rows[t], hs], k, NT_DIMS,
                                              preferred_element_type=F32))
                ke = (k.astype(F32) * te_ref[h]).T.astype(BF16)
                kv.append(jnp.dot(ke, v_ref[rows[t], hs], preferred_element_type=F32))
        cross = []
        states = [state_scr[h] for h in range(R_HEADS)]
        for t in range(group):
            for h, hs in enumerate(heads):
                qs = (q_ref[rows[t], hs].astype(F32) * fs_ref[h]).astype(BF16)
                cross.append(jnp.dot(qs, states[h].astype(BF16), preferred_element_type=F32))
                states[h] = states[h] * cdec_ref[h] + kv[t * R_HEADS + h]
        for h in range(R_HEADS):
            state_scr[h] = states[h]
        for t in range(group):
            for h, hs in enumerate(heads):
                i = t * R_HEADS + h
                sc = (scores[i] * dec_ref[h]).astype(BF16)
                y = jnp.dot(sc, v_ref[rows[t], hs], preferred_element_type=F32) + cross[i]
                mu = jnp.mean(y, axis=-1, keepdims=True)
                var = jnp.mean((y - mu) ** 2, axis=-1, keepdims=True)
                yn = (y - mu) * lax.rsqrt(var + EPS) * gain_ref[:, hs]
                z = z_ref[rows[t], hs].astype(F32)
                o_ref[rows[t], hs] = (yn * (z * jax.nn.sigmoid(z))).astype(o_ref.dtype)
        return carry

    lax.fori_loop(0, RB // (CHUNK * group), chunk_group_body, 0)


def _retention(rqk, pb, gn_gain):
    log_g = jnp.log(1.0 - 2.0 ** (-5.0 - jnp.arange(R_HEADS, dtype=F32)))
    pos = jnp.arange(CHUNK, dtype=F32)
    dist = jnp.abs(pos[:, None] - pos[None, :])
    intra_decay = jnp.exp(log_g[:, None, None] * dist)
    to_end = jnp.exp(log_g[:, None] * (CHUNK - 1.0 - pos)[None, :])
    from_start = jnp.exp(log_g[:, None] * (pos + 1.0)[None, :])
    chunk_decay = jnp.exp(log_g * CHUNK)
    te = jnp.broadcast_to(to_end[:, :, None], (R_HEADS, CHUNK, R_KEY_DIM))
    fs = jnp.broadcast_to(from_start[:, :, None], (R_HEADS, CHUNK, R_KEY_DIM))

    rqk3 = rqk.reshape(BATCH, SEQ, 2 * R_WIDTH)
    pb3 = pb.reshape(BATCH, SEQ, PB_WIDTH)
    out = pl.pallas_call(
        _ret_kernel,
        grid=(BATCH, SEQ // RB),
        in_specs=[
            pl.BlockSpec(memory_space=pltpu.SMEM),
            pl.BlockSpec((None, RB, R_WIDTH), lambda b, i: (b, i, 0)),
            pl.BlockSpec((None, RB, R_WIDTH), lambda b, i: (b, i, 1)),
            pl.BlockSpec((None, RB, R_WIDTH), lambda b, i: (b, i, PB_RV)),
            pl.BlockSpec((None, RB, R_WIDTH), lambda b, i: (b, i, PB_RZ)),
            pl.BlockSpec((1, R_WIDTH), lambda b, i: (0, 0)),
            pl.BlockSpec((R_HEADS, CHUNK, CHUNK), lambda b, i: (0, 0, 0)),
            pl.BlockSpec((R_HEADS, CHUNK, R_KEY_DIM), lambda b, i: (0, 0, 0)),
            pl.BlockSpec((R_HEADS, CHUNK, R_KEY_DIM), lambda b, i: (0, 0, 0)),
        ],
        out_specs=pl.BlockSpec((None, RB, R_WIDTH), lambda b, i: (b, i, 0)),
        out_shape=jax.ShapeDtypeStruct((BATCH, SEQ, R_WIDTH), BF16),
        scratch_shapes=[pltpu.VMEM((R_HEADS, R_KEY_DIM, R_VAL_DIM), F32)],
        compiler_params=_params(2),
        name="retention",
    )(chunk_decay, rqk3, rqk3, pb3, pb3, gn_gain.reshape(1, R_WIDTH), intra_decay, te, fs)
    return out.reshape(TOKENS, R_WIDTH)


def _out_kernel(a_ref, b_ref, ga_ref, gb_ref, x_ref, p_ref, wa_ref, wb_ref, wo_ref, wp_ref, wg_ref,
                fg_ref, o_ref):
    ta = jnp.dot(a_ref[...], wa_ref[...], preferred_element_type=F32)
    tb = jnp.dot(b_ref[...], wb_ref[...], preferred_element_type=F32)
    merged = (jax.nn.sigmoid(ga_ref[...].astype(F32)) * ta
              + jax.nn.sigmoid(gb_ref[...].astype(F32)) * tb)
    r = x_ref[...] + jnp.dot(merged.astype(BF16), wo_ref[...], preferred_element_type=F32)
    u = jnp.dot(p_ref[...].astype(BF16), wp_ref[...], preferred_element_type=F32)
    g = jnp.dot(r.astype(BF16), wg_ref[...], preferred_element_type=F32)
    y = r + u * jax.nn.sigmoid(g)
    ms = jnp.mean(y * y, axis=-1, keepdims=True)
    o_ref[...] = y * lax.rsqrt(ms + EPS) * fg_ref[...]


def _output(a_out, b_out, pb, x2d, p2d, wa, wb, wo, wp, wg, final_gain, tm=256):
    def resident(shape):
        return pl.BlockSpec(shape, lambda i: (0, 0), pipeline_mode=pl.Buffered(1))

    return pl.pallas_call(
        _out_kernel,
        grid=(TOKENS // tm,),
        in_specs=[pl.BlockSpec((tm, A_WIDTH), lambda i: (i, 0)),
                  pl.BlockSpec((tm, R_WIDTH), lambda i: (i, 0)),
                  pl.BlockSpec((tm, D_MODEL), lambda i: (i, PB_GA * R_WIDTH // D_MODEL)),
                  pl.BlockSpec((tm, D_MODEL), lambda i: (i, PB_GB * R_WIDTH // D_MODEL)),
                  pl.BlockSpec((tm, D_MODEL), lambda i: (i, 0)),
                  pl.BlockSpec((tm, PLE_DIM), lambda i: (i, 0)),
                  resident((A_WIDTH, D_MODEL)),
                  resident((R_WIDTH, D_MODEL)),
                  resident((D_MODEL, D_MODEL)),
                  resident((PLE_DIM, D_MODEL)),
                  resident((D_MODEL, D_MODEL)),
                  pl.BlockSpec((1, D_MODEL), lambda i: (0, 0))],
        out_specs=pl.BlockSpec((tm, D_MODEL), lambda i: (i, 0)),
        out_shape=jax.ShapeDtypeStruct((TOKENS, D_MODEL), F32),
        compiler_params=_params(1),
        name="output",
    )(a_out, b_out, pb, pb, x2d, p2d, wa, wb, wo, wp, wg, final_gain.reshape(1, D_MODEL))


def kernel(x, p, positions, w_in, norm_gain, w_a_out, w_b_out, w_o, ret_gn_gain, w_ple, w_ple_gate,
           rel_bias, final_gain):
    assert x.shape == (BATCH, SEQ, D_MODEL) and w_in.shape == (1, D_MODEL, IN_WIDTH)
    x2d = x.reshape(TOKENS, D_MODEL)
    p2d = p[0].reshape(TOKENS, PLE_DIM)
    wt = jnp.swapaxes(w_in[0], 0, 1)
    q_scale = jnp.where(jnp.arange(PA_WIDTH) < A_WIDTH, A_HEAD_DIM ** -0.5 * LOG2E, 1.0)
    q_scale = q_scale.astype(F32).reshape(1, PA_WIDTH)

    h = _rmsnorm(x2d, norm_gain[0])
    pa = _proj(h, wt, "proj_att", 0, PA_WIDTH, scale=q_scale,
               skip=(2 * A_WIDTH, 3 * A_WIDTH))
    vt = _proj_vt(h, wt, 2 * A_WIDTH)
    iq = _proj(h, wt, "proj_iq", COL_IQ, IDX_HEADS * IDX_DIM, scale=IDX_DIM ** -0.5)
    pb = _proj(h, wt, "proj_ret", COL_RV, PB_WIDTH)
    rqk = _proj_rope(h, wt, positions)
    ikd, iwt = _proj_idx(h, wt)

    a_out = _dsa(pa, vt, iq, ikd, iwt, positions, rel_bias)
    b_out = _retention(rqk, pb, ret_gn_gain[0])

    out = _output(a_out, b_out, pb, x2d, p2d, w_a_out[0].astype(BF16), w_b_out[0].astype(BF16),
                  w_o[0].astype(BF16), w_ple[0].astype(BF16), w_ple_gate[0].astype(BF16), final_gain)
    return out.reshape(BATCH, SEQ, D_MODEL)
```

```python
import functools
import math

import jax
import jax.numpy as jnp
from jax import lax
from jax.experimental import pallas as pl
from jax.experimental.pallas import tpu as pltpu

D_MODEL = 2048
BATCH = 4
SEQ = 4096
TOKENS = BATCH * SEQ
CHUNK = 64
PLE_DIM = 256
EPS = 1e-6
A_HEADS = 8
A_HEAD_DIM = 128
A_WIDTH = A_HEADS * A_HEAD_DIM
IDX_HEADS = 16
IDX_DIM = 64
TOPK = min(256, SEQ // 4)
R_HEADS = 8
R_KEY_DIM = 128
R_VAL_DIM = 128
R_WIDTH = R_HEADS * R_VAL_DIM
ROPE_BASE = 10000.0
N_BUCKETS = 32
MAX_DISTANCE = 128

COL_IQ = 4 * A_WIDTH
COL_IK = COL_IQ + IDX_HEADS * IDX_DIM
COL_IW = COL_IK + IDX_DIM
COL_RQ = COL_IW + IDX_HEADS
COL_RV = COL_RQ + 2 * R_WIDTH
IN_WIDTH = COL_RV + 2 * R_WIDTH + 2 * D_MODEL

PA_Q, PA_K, PA_AZ = 0, 1, 2
PA_WIDTH = 3 * A_WIDTH
PB_RV, PB_RZ, PB_GA, PB_GB = 0, 1, 2, 4
PB_WIDTH = 2 * R_WIDTH + 2 * D_MODEL

LANES = 128
QB = 256
KB = 256
NQB = SEQ // QB
NKB = SEQ // KB
VT_ROWS = A_HEAD_DIM + 16
RB = 512
NEG = -1e30
INT_MIN = -(2 ** 31)
LOG2E = math.log2(math.e)
VMEM_LIMIT = 56 * 1024 * 1024

F32 = jnp.float32
BF16 = jnp.bfloat16
NT_DIMS = (((1,), (1,)), ((), ()))


def _params(n_axes):
    return pltpu.CompilerParams(dimension_semantics=("arbitrary",) * n_axes,
                                vmem_limit_bytes=VMEM_LIMIT)


def _rmsnorm_kernel(x_ref, g_ref, o_ref):
    x = x_ref[...]
    ms = jnp.mean(x * x, axis=-1, keepdims=True)
    o_ref[...] = (x * lax.rsqrt(ms + EPS) * g_ref[...]).astype(o_ref.dtype)


def _rmsnorm(x2d, gain, tm=512):
    return pl.pallas_call(
        _rmsnorm_kernel,
        grid=(TOKENS // tm,),
        in_specs=[pl.BlockSpec((tm, D_MODEL), lambda i: (i, 0)),
                  pl.BlockSpec((1, D_MODEL), lambda i: (0, 0))],
        out_specs=pl.BlockSpec((tm, D_MODEL), lambda i: (i, 0)),
        out_shape=jax.ShapeDtypeStruct((TOKENS, D_MODEL), BF16),
        compiler_params=_params(1),
        name="rmsnorm",
    )(x2d, gain.reshape(1, D_MODEL))


def _wt_block(rows, row_of):
    assert rows % 8 == 0
    return pl.BlockSpec((pl.Element(rows), pl.Element(D_MODEL)),
                        lambda *g: (pl.multiple_of(row_of(*g), 8), 0))


def _proj_kernel(h_ref, wt_ref, *rest, scale):
    acc = lax.dot_general(h_ref[...], wt_ref[...].astype(BF16), NT_DIMS,
                          preferred_element_type=F32)
    if scale == "row":
        acc = acc * rest[0][...]
    elif scale is not None:
        acc = acc * scale
    rest[-1][...] = acc.astype(rest[-1].dtype)


def _proj(h, wt, name, row0, ncols, scale=None, skip=None, tm=2048, tn=512):
    def row_of(i, j):
        if skip is not None:
            j = jnp.where(j < skip[0] // tn, j, j + (skip[1] - skip[0]) // tn)
        return row0 + j * tn

    in_specs = [pl.BlockSpec((tm, D_MODEL), lambda i, j: (i, 0)), _wt_block(tn, row_of)]
    args = [h, wt]
    if scale is not None and not isinstance(scale, float):
        in_specs.append(pl.BlockSpec((1, tn), lambda i, j: (0, j)))
        args.append(scale)
        scale = "row"
    return pl.pallas_call(
        functools.partial(_proj_kernel, scale=scale),
        grid=(TOKENS // tm, ncols // tn),
        in_specs=in_specs,
        out_specs=pl.BlockSpec((tm, tn), lambda i, j: (i, j)),
        out_shape=jax.ShapeDtypeStruct((TOKENS, ncols), BF16),
        compiler_params=_params(2),
        name=name,
    )(*args)


def _proj_vt_kernel(h_ref, wt_ref, o_ref):
    acc_t = lax.dot_general(wt_ref[...].astype(BF16), h_ref[...], NT_DIMS,
                            preferred_element_type=F32)
    ones = jnp.ones((VT_ROWS - A_HEAD_DIM, KB), o_ref.dtype)
    for t in range(o_ref.shape[0]):
        for hh in range(acc_t.shape[0] // A_HEAD_DIM):
            r0 = hh * VT_ROWS
            o_ref[t, r0:r0 + A_HEAD_DIM] = acc_t[hh * A_HEAD_DIM:(hh + 1) * A_HEAD_DIM,
                                                 t * KB:(t + 1) * KB].astype(o_ref.dtype)
            o_ref[t, r0 + A_HEAD_DIM:r0 + VT_ROWS] = ones


def _proj_vt(h, wt, row0, tm=2048, tn=512):
    per_batch = SEQ // tm
    rows = tn // A_HEAD_DIM * VT_ROWS
    return pl.pallas_call(
        _proj_vt_kernel,
        grid=(TOKENS // tm, A_WIDTH // tn),
        in_specs=[pl.BlockSpec((tm, D_MODEL), lambda i, j: (i, 0)),
                  _wt_block(tn, lambda i, j: row0 + j * tn)],
        out_specs=pl.BlockSpec((None, tm // KB, rows, KB),
                               lambda i, j: (i // per_batch, i % per_batch, j, 0)),
        out_shape=jax.ShapeDtypeStruct((BATCH, NKB, A_HEADS * VT_ROWS, KB), BF16),
        compiler_params=_params(2),
        name="proj_vt",
    )(h, wt)


def _proj_idx_kernel(h_ref, wik_ref, wiw_ref, ik_ref, iwt_ref):
    h = h_ref[...]
    wik = wik_ref[...].astype(BF16)
    ik = lax.dot_general(h, jnp.concatenate([wik, wik], axis=0), NT_DIMS,
                         preferred_element_type=F32)
    mu = jnp.mean(ik, axis=-1, keepdims=True)
    var = jnp.mean((ik - mu) ** 2, axis=-1, keepdims=True)
    ik_ref[...] = ((ik - mu) * lax.rsqrt(var + EPS)).astype(ik_ref.dtype)
    iwt = lax.dot_general(wiw_ref[...].astype(BF16), h, NT_DIMS, preferred_element_type=F32)
    iwt_ref[...] = iwt * (IDX_HEADS ** -0.5)


def _proj_idx(h, wt, tm=2048):
    return pl.pallas_call(
        _proj_idx_kernel,
        grid=(TOKENS // tm,),
        in_specs=[pl.BlockSpec((tm, D_MODEL), lambda i: (i, 0)),
                  _wt_block(IDX_DIM, lambda i: COL_IK),
                  _wt_block(IDX_HEADS, lambda i: COL_IW)],
        out_specs=[pl.BlockSpec((tm, LANES), lambda i: (i, 0)),
                   pl.BlockSpec((IDX_HEADS, tm), lambda i: (0, i))],
        out_shape=[jax.ShapeDtypeStruct((TOKENS, LANES), BF16),
                   jax.ShapeDtypeStruct((IDX_HEADS, TOKENS), F32)],
        compiler_params=_params(1),
        name="proj_idx",
    )(h, wt, wt)


def _proj_rope_kernel(h_ref, w_ref, pos_ref, freq_ref, o_ref, cos_scr, sin_scr, *,
                      tn, k_tile0, k_scale):
    @pl.when(pl.program_id(1) == 0)
    def _():
        hm = pos_ref.shape[0]
        ang = pos_ref[...].astype(F32) * freq_ref[...]
        c = jnp.cos(ang)
        s = jnp.sin(ang)
        cr = pltpu.roll(c, LANES // 2, 1)
        sr = pltpu.roll(s, LANES // 2, 1)
        low = lax.broadcasted_iota(jnp.int32, c.shape, 1) < LANES // 2
        cos_scr[:hm] = jnp.where(low, c, cr)
        cos_scr[hm:] = jnp.where(low, cr, c)
        sin_scr[:hm] = jnp.where(low, -s, sr)
        sin_scr[hm:] = jnp.where(low, -sr, s)

    acc = lax.dot_general(h_ref[...], w_ref[...].astype(BF16), NT_DIMS,
                          preferred_element_type=F32)
    cos = cos_scr[...]
    sin = sin_scr[...]
    scale = jnp.where(pl.program_id(1) >= k_tile0, k_scale, 1.0).astype(F32)
    for g in range(tn // LANES):
        xg = acc[:, g * LANES:(g + 1) * LANES]
        rot = xg * cos + pltpu.roll(xg, LANES // 2, 1) * sin
        o_ref[:, g * LANES:(g + 1) * LANES] = (rot * scale).astype(o_ref.dtype)


def _proj_rope(h, wt, positions, tm=2048, tn=512):
    ncols = 2 * R_WIDTH
    half = R_KEY_DIM // 2
    inv_freq = ROPE_BASE ** (-jnp.arange(half, dtype=F32) / half)
    freq2 = jnp.concatenate([inv_freq, inv_freq]).reshape(1, R_KEY_DIM)
    pos_t = positions.reshape(TOKENS // tm, 2, tm // 2, 1)
    pos2 = jnp.concatenate([jnp.broadcast_to(pos_t[:, 0], (TOKENS // tm, tm // 2, half)),
                            jnp.broadcast_to(pos_t[:, 1], (TOKENS // tm, tm // 2, half))], axis=-1)
    pos2 = pos2.reshape(TOKENS // 2, R_KEY_DIM)
    kern = functools.partial(_proj_rope_kernel, tn=tn, k_tile0=R_WIDTH // tn,
                             k_scale=R_KEY_DIM ** -0.5)
    return pl.pallas_call(
        kern,
        grid=(TOKENS // tm, ncols // tn),
        in_specs=[pl.BlockSpec((tm, D_MODEL), lambda i, j: (i, 0)),
                  _wt_block(tn, lambda i, j: COL_RQ + j * tn),
                  pl.BlockSpec((tm // 2, R_KEY_DIM), lambda i, j: (i, 0)),
                  pl.BlockSpec((1, R_KEY_DIM), lambda i, j: (0, 0))],
        out_specs=pl.BlockSpec((tm, tn), lambda i, j: (i, j)),
        out_shape=jax.ShapeDtypeStruct((TOKENS, ncols), BF16),
        scratch_shapes=[pltpu.VMEM((tm, R_KEY_DIM), F32), pltpu.VMEM((tm, R_KEY_DIM), F32)],
        compiler_params=_params(2),
        name="proj_rope",
    )(h, wt, pos2, freq2)


def _to_key(a):
    bits = pltpu.bitcast(a, jnp.int32)
    return jnp.where(bits < 0, bits ^ jnp.int32(0x7FFFFFFF), bits)


def _dsa_kernel(pmin_ref, pmax_ref,
                q_ref, k_ref, vt_ref, iq_ref, ikd_ref, iwt_ref, posq_ref, posk_ref, az_ref,
                tab_ref, far_ref, o_ref,
                idx_scr, idxb_scr, iqm_scr, acc_scr, m_scr):
    b = pl.program_id(0)
    qi = pl.program_id(1)
    nkb = qi + 1

    lane = lax.broadcasted_iota(jnp.int32, (QB, LANES), 1)
    for p in range(IDX_HEADS // 2):
        pair = iq_ref[:, p * LANES:(p + 1) * LANES].astype(F32)
        iqm_scr[2 * p] = jnp.where(lane < IDX_DIM, pair, 0.0).astype(BF16)
        iqm_scr[2 * p + 1] = jnp.where(lane >= IDX_DIM, pair, 0.0).astype(BF16)
    iwt = iwt_ref[...]

    rr = lax.broadcasted_iota(jnp.int32, (KB, QB), 0)
    cc = lax.broadcasted_iota(jnp.int32, (KB, QB), 1)

    def idx_scores(kb):
        s0 = pl.multiple_of(kb * KB, KB)
        kid = ikd_ref[pl.ds(s0, KB), :]
        acc = jnp.zeros((KB, QB), F32)
        for h in range(IDX_HEADS):
            sc = lax.dot_general(kid, iqm_scr[h], NT_DIMS, preferred_element_type=F32)
            acc = acc + jnp.maximum(sc, 0.0) * iwt[h:h + 1, :]
        return acc

    def idx_store(kb, acc, diagonal):
        if diagonal:
            admissible = (rr // CHUNK) <= (cc // CHUNK)
            acc = jnp.where(admissible, acc, -jnp.inf)
        idx_scr[kb] = acc
        idxb_scr[kb] = acc.astype(BF16)

    def idx_tiles(kbs, diagonal_last):
        accs = [idx_scores(kb) for kb in kbs]
        for t, kb in enumerate(kbs):
            idx_store(kb, accs[t], diagonal_last and t == len(kbs) - 1)

    def idx_pair_body(j, c):
        idx_tiles([2 * j, 2 * j + 1], False)
        return c

    lax.fori_loop(0, lax.shift_right_logical(qi, 1), idx_pair_body, 0)
    qi_odd = (qi & 1) == 1
    pl.when(qi_odd)(lambda: idx_tiles([qi - 1, qi], True))
    pl.when(jnp.logical_not(qi_odd))(lambda: idx_tiles([qi], True))

    nkb_sel = jnp.where(qi > 0, nkb, 0)
    npair_sel = lax.shift_right_logical(nkb_sel + 1, 1)
    sign_bit = jnp.int32(INT_MIN)

    @pl.when((nkb & 1) == 1)
    def _():
        idx_scr[nkb] = jnp.full((KB, QB), -jnp.inf, F32)
        idxb_scr[nkb] = jnp.full((KB, QB), -jnp.inf, BF16)

    def pattern_to_float(u):
        s = u ^ sign_bit
        return pltpu.bitcast(jnp.where(s < 0, s ^ jnp.int32(0x7FFFFFFF), s), F32)

    def count_ge(plane_ref, cand, group):
        dt = plane_ref.dtype
        one = jnp.ones((), dt)
        zero = jnp.zeros((), dt)

        def body(j, acc):
            slabs = []
            for kb in (2 * j, 2 * j + 1):
                part = jnp.where(plane_ref[kb] >= cand, one, zero)
                slabs += [part[group * g:group * (g + 1)] for g in range(KB // group)]
            while len(slabs) > 1:
                slabs = [slabs[i] + slabs[i + 1] for i in range(0, len(slabs), 2)]
            return acc + slabs[0]

        acc = lax.fori_loop(0, npair_sel, body, jnp.zeros((group, QB), dt))
        return jnp.sum(acc.astype(F32), axis=0, keepdims=True)

    def bisect(n_bits, count_fn):
        def bit_body(it, ans):
            cand = ans | jnp.left_shift(jnp.int32(1), n_bits - 1 - it)
            return jnp.where(count_fn(cand) >= TOPK, cand, ans)

        return lax.fori_loop(0, jnp.where(qi > 0, n_bits, 0), bit_body,
                             jnp.zeros((1, QB), jnp.int32))

    def bf16_candidate(c16):
        s16 = c16 - 32768
        b16 = jnp.where(s16 < 0, s16 ^ 0x7FFF, s16)
        return pltpu.bitcast(b16 << 16, F32).astype(BF16)

    c16 = bisect(16, lambda c: count_ge(idxb_scr, bf16_candidate(c), 16))
    lo = (c16 << 16) - 0x10001
    delta = bisect(18, lambda d: count_ge(idx_scr, pattern_to_float(lo + d), 8))
    thr = pattern_to_float(lo + delta)
    thr = jnp.where(qi > 0, thr, jnp.finfo(F32).min)

    def count_where(pred_fn):
        def body(kb, acc):
            part = jnp.where(pred_fn(kb, idx_scr[kb]), 1.0, 0.0)
            return acc + jnp.sum(part.reshape(KB // 8, 8, QB), axis=0)
        acc = lax.fori_loop(0, nkb, body, jnp.zeros((8, QB), F32))
        return jnp.sum(acc, axis=0, keepdims=True)

    cnt_ge_thr = count_ge(idx_scr, thr, 8)
    has_tie = jnp.logical_and(qi > 0, jnp.max(cnt_ge_thr) > TOPK)

    @pl.when(has_tie)
    def _():
        cnt_gt = count_where(lambda kb, kk: kk > thr)
        need = TOPK - cnt_gt

        def key_index(kb):
            return kb * KB + rr

        def jb_body(it, j0):
            cand = j0 | jnp.left_shift(jnp.int32(1), 11 - it)
            f = count_where(lambda kb, kk: jnp.logical_and(kk == thr, key_index(kb) < cand))
            return jnp.where(f < need, cand, j0)

        j0 = lax.fori_loop(0, 12, jb_body, jnp.zeros((1, QB), jnp.int32))
        jstar = j0 + 1

        def fix_body(kb, c):
            kk = idx_scr[kb]
            drop = jnp.logical_and(kk == thr, key_index(kb) >= jstar)
            idx_scr[kb] = jnp.where(drop, -jnp.inf, kk)
            return c

        lax.fori_loop(0, nkb, fix_body, 0)

    m_scr[...] = jnp.full(m_scr.shape, NEG, F32)
    acc_scr[...] = jnp.zeros(acc_scr.shape, F32)
    posq = posq_ref[...]
    pmin_q = pmin_ref[b, qi]
    half = N_BUCKETS // 2
    max_exact = half // 2

    def att_tiles(kbs, near):
        scores = {}
        for t, kb in enumerate(kbs):
            s0 = pl.multiple_of(kb * KB, KB)
            for h in range(A_HEADS):
                hs = slice(h * A_HEAD_DIM, (h + 1) * A_HEAD_DIM)
                scores[t, h] = lax.dot_general(k_ref[pl.ds(s0, KB), hs], q_ref[:, hs], NT_DIMS,
                                               preferred_element_type=F32)
        for t, kb in enumerate(kbs):
            s0 = pl.multiple_of(kb * KB, KB)
            madd = jnp.where(idx_scr[kb] >= thr, 0.0, NEG)
            if near:
                pk = posk_ref[pl.ds(s0, KB), :]
                rel = jnp.concatenate([pk] * (QB // LANES), axis=1) - posq
                n = jnp.abs(rel)
                nf = jnp.maximum(n, 1).astype(F32)
                large = max_exact + (jnp.log(nf / max_exact) / math.log(MAX_DISTANCE / max_exact)
                                     * (half - max_exact)).astype(jnp.int32)
                large = jnp.minimum(large, half - 1)
                bucket = jnp.where(rel > 0, half, 0) + jnp.where(n < max_exact, n, large)
            probs, alphas = [], []
            for h in range(A_HEADS):
                if near:
                    row = jnp.broadcast_to(tab_ref[h:h + 1, :], (KB, LANES))
                    bias = jnp.concatenate(
                        [jnp.take_along_axis(row, bucket[:, g * LANES:(g + 1) * LANES], axis=1)
                         for g in range(QB // LANES)], axis=1)
                    s = scores[t, h] + ((bias - far_ref[h]) * LOG2E + madd)
                else:
                    s = scores[t, h] + madd
                m_old = m_scr[h]
                m_new = jnp.maximum(m_old, jnp.max(s, axis=0, keepdims=True))
                alphas.append(jnp.exp2(m_old - m_new))
                probs.append(jnp.exp2(s - m_new).astype(BF16))
                m_scr[h] = m_new
            for h in range(A_HEADS):
                pv = jnp.dot(vt_ref[kb, h * VT_ROWS:(h + 1) * VT_ROWS, :], probs[h],
                             preferred_element_type=F32)
                acc_scr[h] = alphas[h] * acc_scr[h] + pv

    def is_far(kb):
        return (pmin_q - pmax_ref[b, kb]) >= MAX_DISTANCE

    def att_single(kb, c):
        far = is_far(kb)
        pl.when(far)(lambda: att_tiles([kb], False))
        pl.when(jnp.logical_not(far))(lambda: att_tiles([kb], True))
        return c

    def att_pair_body(j, c):
        kb0 = 2 * j
        kb1 = kb0 + 1
        both_far = jnp.logical_and(kb1 < nkb, jnp.logical_and(is_far(kb0), is_far(kb1)))
        pl.when(both_far)(lambda: att_tiles([kb0, kb1], False))

        @pl.when(jnp.logical_not(both_far))
        def _():
            lax.fori_loop(kb0, jnp.minimum(kb1 + 1, nkb), att_single, 0)

        return c

    lax.fori_loop(0, lax.shift_right_logical(nkb + 1, 1), att_pair_body, 0)

    for h in range(A_HEADS):
        hs = slice(h * A_HEAD_DIM, (h + 1) * A_HEAD_DIM)
        z = az_ref[:, hs].astype(F32)
        out_t = acc_scr[h, :A_HEAD_DIM] / acc_scr[h, A_HEAD_DIM:A_HEAD_DIM + 1]
        o_ref[:, hs] = (out_t.T * (z * jax.nn.sigmoid(z))).astype(o_ref.dtype)


def _dsa(pa, vt, iq, ikd, iwt, positions, rel_bias):
    pa3 = pa.reshape(BATCH, SEQ, PA_WIDTH)
    iq3 = iq.reshape(BATCH, SEQ, A_WIDTH)
    ikd3 = ikd.reshape(BATCH, SEQ, LANES)
    posq = positions.reshape(BATCH, 1, SEQ)
    posk = jnp.broadcast_to(positions[:, :, None], (BATCH, SEQ, LANES))
    pblk = positions.reshape(BATCH, NKB, KB)
    pmin = jnp.min(pblk, axis=-1)
    pmax = jnp.max(pblk, axis=-1)
    tab = jnp.zeros((A_HEADS, LANES), F32).at[:, :N_BUCKETS].set(rel_bias.astype(F32).T)
    far = rel_bias[N_BUCKETS // 2 - 1, :].astype(F32)

    grid_spec = pltpu.PrefetchScalarGridSpec(
        num_scalar_prefetch=2,
        grid=(BATCH, NQB),
        in_specs=[
            pl.BlockSpec((None, QB, A_WIDTH), lambda b, i, *_: (b, i, PA_Q)),
            pl.BlockSpec((None, SEQ, A_WIDTH), lambda b, i, *_: (b, 0, PA_K),
                         pipeline_mode=pl.Buffered(1)),
            pl.BlockSpec((None, NKB, A_HEADS * VT_ROWS, KB), lambda b, i, *_: (b, 0, 0, 0),
                         pipeline_mode=pl.Buffered(1)),
            pl.BlockSpec((None, QB, A_WIDTH), lambda b, i, *_: (b, i, 0)),
            pl.BlockSpec((None, SEQ, LANES), lambda b, i, *_: (b, 0, 0)),
            pl.BlockSpec((IDX_HEADS, QB), lambda b, i, *_: (0, b * NQB + i)),
            pl.BlockSpec((None, 1, QB), lambda b, i, *_: (b, 0, i)),
            pl.BlockSpec((None, SEQ, LANES), lambda b, i, *_: (b, 0, 0)),
            pl.BlockSpec((None, QB, A_WIDTH), lambda b, i, *_: (b, i, PA_AZ)),
            pl.BlockSpec((A_HEADS, LANES), lambda b, i, *_: (0, 0)),
            pl.BlockSpec(memory_space=pltpu.SMEM),
        ],
        out_specs=pl.BlockSpec((None, QB, A_WIDTH), lambda b, i, *_: (b, i, 0)),
        scratch_shapes=[
            pltpu.VMEM((NKB, KB, QB), F32),
            pltpu.VMEM((NKB, KB, QB), BF16),
            pltpu.VMEM((IDX_HEADS, QB, LANES), BF16),
            pltpu.VMEM((A_HEADS, VT_ROWS, QB), F32),
            pltpu.VMEM((A_HEADS, 1, QB), F32),
        ],
    )
    out = pl.pallas_call(
        _dsa_kernel,
        grid_spec=grid_spec,
        out_shape=jax.ShapeDtypeStruct((BATCH, SEQ, A_WIDTH), BF16),
        compiler_params=_params(2),
        name="dsa",
    )(pmin, pmax, pa3, pa3, vt, iq3, ikd3, iwt, posq, posk, pa3, tab, far)
    return out.reshape(TOKENS, A_WIDTH)


def _ret_kernel(cdec_ref, q_ref, k_ref, v_ref, z_ref, gain_ref, dec_ref, te_ref, fs_ref, o_ref,
                state_scr):
    @pl.when(pl.program_id(1) == 0)
    def _():
        state_scr[...] = jnp.zeros(state_scr.shape, F32)

    heads = [slice(h * R_KEY_DIM, (h + 1) * R_KEY_DIM) for h in range(R_HEADS)]

    group = 4

    def chunk_group_body(g, carry):
        rows = [pl.ds(pl.multiple_of((g * group + t) * CHUNK, CHUNK), CHUNK) for t in range(group)]
        scores, kv = [], []
        for t in range(group):
            for h, hs in enumerate(heads):
                k = k_ref[rows[t], hs]
                scores.append(lax.dot_general(q_ref[rows[t], hs], k, NT_DIMS,
                                              preferred_element_type=F32))
                ke = (k.astype(F32) * te_ref[h]).T.astype(BF16)
                kv.append(jnp.dot(ke, v_ref[rows[t], hs], preferred_element_type=F32))
        cross = []
        states = [state_scr[h] for h in range(R_HEADS)]
        for t in range(group):
            for h, hs in enumerate(heads):
                qs = (q_ref[rows[t], hs].astype(F32) * fs_ref[h]).astype(BF16)
                cross.append(jnp.dot(qs, states[h].astype(BF16), preferred_element_type=F32))
                states[h] = states[h] * cdec_ref[h] + kv[t * R_HEADS + h]
        for h in range(R_HEADS):
            state_scr[h] = states[h]
        for t in range(group):
            for h, hs in enumerate(heads):
                i = t * R_HEADS + h
                sc = (scores[i] * dec_ref[h]).astype(BF16)
                y = jnp.dot(sc, v_ref[rows[t], hs], preferred_element_type=F32) + cross[i]
                mu = jnp.mean(y, axis=-1, keepdims=True)
                var = jnp.mean((y - mu) ** 2, axis=-1, keepdims=True)
                yn = (y - mu) * lax.rsqrt(var + EPS) * gain_ref[:, hs]
                z = z_ref[rows[t], hs].astype(F32)
                o_ref[rows[t], hs] = (yn * (z * jax.nn.sigmoid(z))).astype(o_ref.dtype)
        return carry

    lax.fori_loop(0, RB // (CHUNK * group), chunk_group_body, 0)


def _retention(rqk, pb, gn_gain):
    log_g = jnp.log(1.0 - 2.0 ** (-5.0 - jnp.arange(R_HEADS, dtype=F32)))
    pos = jnp.arange(CHUNK, dtype=F32)
    dist = jnp.abs(pos[:, None] - pos[None, :])
    intra_decay = jnp.exp(log_g[:, None, None] * dist)
    to_end = jnp.exp(log_g[:, None] * (CHUNK - 1.0 - pos)[None, :])
    from_start = jnp.exp(log_g[:, None] * (pos + 1.0)[None, :])
    chunk_decay = jnp.exp(log_g * CHUNK)
    te = jnp.broadcast_to(to_end[:, :, None], (R_HEADS, CHUNK, R_KEY_DIM))
    fs = jnp.broadcast_to(from_start[:, :, None], (R_HEADS, CHUNK, R_KEY_DIM))

    rqk3 = rqk.reshape(BATCH, SEQ, 2 * R_WIDTH)
    pb3 = pb.reshape(BATCH, SEQ, PB_WIDTH)
    out = pl.pallas_call(
        _ret_kernel,
        grid=(BATCH, SEQ // RB),
        in_specs=[
            pl.BlockSpec(memory_space=pltpu.SMEM),
            pl.BlockSpec((None, RB, R_WIDTH), lambda b, i: (b, i, 0)),
            pl.BlockSpec((None, RB, R_WIDTH), lambda b, i: (b, i, 1)),
            pl.BlockSpec((None, RB, R_WIDTH), lambda b, i: (b, i, PB_RV)),
            pl.BlockSpec((None, RB, R_WIDTH), lambda b, i: (b, i, PB_RZ)),
            pl.BlockSpec((1, R_WIDTH), lambda b, i: (0, 0)),
            pl.BlockSpec((R_HEADS, CHUNK, CHUNK), lambda b, i: (0, 0, 0)),
            pl.BlockSpec((R_HEADS, CHUNK, R_KEY_DIM), lambda b, i: (0, 0, 0)),
            pl.BlockSpec((R_HEADS, CHUNK, R_KEY_DIM), lambda b, i: (0, 0, 0)),
        ],
        out_specs=pl.BlockSpec((None, RB, R_WIDTH), lambda b, i: (b, i, 0)),
        out_shape=jax.ShapeDtypeStruct((BATCH, SEQ, R_WIDTH), BF16),
        scratch_shapes=[pltpu.VMEM((R_HEADS, R_KEY_DIM, R_VAL_DIM), F32)],
        compiler_params=_params(2),
        name="retention",
    )(chunk_decay, rqk3, rqk3, pb3, pb3, gn_gain.reshape(1, R_WIDTH), intra_decay, te, fs)
    return out.reshape(TOKENS, R_WIDTH)


def _out_kernel(a_ref, b_ref, ga_ref, gb_ref, x_ref, p_ref, wa_ref, wb_ref, wo_ref, wp_ref, wg_ref,
                fg_ref, o_ref):
    ta = jnp.dot(a_ref[...], wa_ref[...], preferred_element_type=F32)
    tb = jnp.dot(b_ref[...], wb_ref[...], preferred_element_type=F32)
    merged = (jax.nn.sigmoid(ga_ref[...].astype(F32)) * ta
              + jax.nn.sigmoid(gb_ref[...].astype(F32)) * tb)
    r = x_ref[...] + jnp.dot(merged.astype(BF16), wo_ref[...], preferred_element_type=F32)
    u = jnp.dot(p_ref[...].astype(BF16), wp_ref[...], preferred_element_type=F32)
    g = jnp.dot(r.astype(BF16), wg_ref[...], preferred_element_type=F32)
    y = r + u * jax.nn.sigmoid(g)
    ms = jnp.mean(y * y, axis=-1, keepdims=True)
    o_ref[...] = y * lax.rsqrt(ms + EPS) * fg_ref[...]


def _output(a_out, b_out, pb, x2d, p2d, wa, wb, wo, wp, wg, final_gain, tm=256):
    def resident(shape):
        return pl.BlockSpec(shape, lambda i: (0, 0), pipeline_mode=pl.Buffered(1))

    return pl.pallas_call(
        _out_kernel,
        grid=(TOKENS // tm,),
        in_specs=[pl.BlockSpec((tm, A_WIDTH), lambda i: (i, 0)),
                  pl.BlockSpec((tm, R_WIDTH), lambda i: (i, 0)),
                  pl.BlockSpec((tm, D_MODEL), lambda i: (i, PB_GA * R_WIDTH // D_MODEL)),
                  pl.BlockSpec((tm, D_MODEL), lambda i: (i, PB_GB * R_WIDTH // D_MODEL)),
                  pl.BlockSpec((tm, D_MODEL), lambda i: (i, 0)),
                  pl.BlockSpec((tm, PLE_DIM), lambda i: (i, 0)),
                  resident((A_WIDTH, D_MODEL)),
                  resident((R_WIDTH, D_MODEL)),
                  resident((D_MODEL, D_MODEL)),
                  resident((PLE_DIM, D_MODEL)),
                  resident((D_MODEL, D_MODEL)),
                  pl.BlockSpec((1, D_MODEL), lambda i: (0, 0))],
        out_specs=pl.BlockSpec((tm, D_MODEL), lambda i: (i, 0)),
        out_shape=jax.ShapeDtypeStruct((TOKENS, D_MODEL), F32),
        compiler_params=_params(1),
        name="output",
    )(a_out, b_out, pb, pb, x2d, p2d, wa, wb, wo, wp, wg, final_gain.reshape(1, D_MODEL))


def kernel(x, p, positions, w_in, norm_gain, w_a_out, w_b_out, w_o, ret_gn_gain, w_ple, w_ple_gate,
           rel_bias, final_gain):
    assert x.shape == (BATCH, SEQ, D_MODEL) and w_in.shape == (1, D_MODEL, IN_WIDTH)
    x2d = x.reshape(TOKENS, D_MODEL)
    p2d = p[0].reshape(TOKENS, PLE_DIM)
    wt = jnp.swapaxes(w_in[0], 0, 1)
    q_scale = jnp.where(jnp.arange(PA_WIDTH) < A_WIDTH, A_HEAD_DIM ** -0.5 * LOG2E, 1.0)
    q_scale = q_scale.astype(F32).reshape(1, PA_WIDTH)

    h = _rmsnorm(x2d, norm_gain[0])
    pa = _proj(h, wt, "proj_att", 0, PA_WIDTH, scale=q_scale,
               skip=(2 * A_WIDTH, 3 * A_WIDTH))
    vt = _proj_vt(h, wt, 2 * A_WIDTH)
    iq = _proj(h, wt, "proj_iq", COL_IQ, IDX_HEADS * IDX_DIM, scale=IDX_DIM ** -0.5)
    pb = _proj(h, wt, "proj_ret", COL_RV, PB_WIDTH)
    rqk = _proj_rope(h, wt, positions)
    ikd, iwt = _proj_idx(h, wt)

    a_out = _dsa(pa, vt, iq, ikd, iwt, positions, rel_bias)
    b_out = _retention(rqk, pb, ret_gn_gain[0])

    out = _output(a_out, b_out, pb, x2d, p2d, w_a_out[0].astype(BF16), w_b_out[0].astype(BF16),
                  w_o[0].astype(BF16), w_ple[0].astype(BF16), w_ple_gate[0].astype(BF16), final_gain)
    return out.reshape(BATCH, SEQ, D_MODEL)
```

```python
import functools
import math

import jax
import jax.numpy as jnp
from jax import lax
from jax.experimental import pallas as pl
from jax.experimental.pallas import tpu as pltpu

D_MODEL = 2048
BATCH = 4
SEQ = 4096
TOKENS = BATCH * SEQ
CHUNK = 64
PLE_DIM = 256
EPS = 1e-6
A_HEADS = 8
A_HEAD_DIM = 128
A_WIDTH = A_HEADS * A_HEAD_DIM
IDX_HEADS = 16
IDX_DIM = 64
TOPK = min(256, SEQ // 4)
R_HEADS = 8
R_KEY_DIM = 128
R_VAL_DIM = 128
R_WIDTH = R_HEADS * R_VAL_DIM
ROPE_BASE = 10000.0
N_BUCKETS = 32
MAX_DISTANCE = 128

COL_IQ = 4 * A_WIDTH
COL_IK = COL_IQ + IDX_HEADS * IDX_DIM
COL_IW = COL_IK + IDX_DIM
COL_RQ = COL_IW + IDX_HEADS
COL_RV = COL_RQ + 2 * R_WIDTH
IN_WIDTH = COL_RV + 2 * R_WIDTH + 2 * D_MODEL

PA_Q, PA_K, PA_AZ = 0, 1, 2
PA_WIDTH = 3 * A_WIDTH
PB_RV, PB_RZ, PB_GA, PB_GB = 0, 1, 2, 4
PB_WIDTH = 2 * R_WIDTH + 2 * D_MODEL

LANES = 128
QB = 256
KB = 256
NQB = SEQ // QB
NKB = SEQ // KB
VT_ROWS = A_HEAD_DIM + 16
RB = 512
NEG = -1e30
INT_MIN = -(2 ** 31)
LOG2E = math.log2(math.e)
VMEM_LIMIT = 56 * 1024 * 1024

F32 = jnp.float32
BF16 = jnp.bfloat16
NT_DIMS = (((1,), (1,)), ((), ()))


def _params(n_axes):
    return pltpu.CompilerParams(dimension_semantics=("arbitrary",) * n_axes,
                                vmem_limit_bytes=VMEM_LIMIT)


def _rmsnorm_kernel(x_ref, g_ref, o_ref):
    x = x_ref[...]
    ms = jnp.mean(x * x, axis=-1, keepdims=True)
    o_ref[...] = (x * lax.rsqrt(ms + EPS) * g_ref[...]).astype(o_ref.dtype)


def _rmsnorm(x2d, gain, tm=512):
    return pl.pallas_call(
        _rmsnorm_kernel,
        grid=(TOKENS // tm,),
        in_specs=[pl.BlockSpec((tm, D_MODEL), lambda i: (i, 0)),
                  pl.BlockSpec((1, D_MODEL), lambda i: (0, 0))],
        out_specs=pl.BlockSpec((tm, D_MODEL), lambda i: (i, 0)),
        out_shape=jax.ShapeDtypeStruct((TOKENS, D_MODEL), BF16),
        compiler_params=_params(1),
        name="rmsnorm",
    )(x2d, gain.reshape(1, D_MODEL))


def _wt_block(rows, row_of):
    assert rows % 8 == 0
    return pl.BlockSpec((pl.Element(rows), pl.Element(D_MODEL)),
                        lambda *g: (pl.multiple_of(row_of(*g), 8), 0))


def _proj_kernel(h_ref, wt_ref, *rest, scale):
    acc = lax.dot_general(h_ref[...], wt_ref[...].astype(BF16), NT_DIMS,
                          preferred_element_type=F32)
    if scale == "row":
        acc = acc * rest[0][...]
    elif scale is not None:
        acc = acc * scale
    rest[-1][...] = acc.astype(rest[-1].dtype)


def _proj(h, wt, name, row0, ncols, scale=None, skip=None, tm=2048, tn=512):
    def row_of(i, j):
        if skip is not None:
            j = jnp.where(j < skip[0] // tn, j, j + (skip[1] - skip[0]) // tn)
        return row0 + j * tn

    in_specs = [pl.BlockSpec((tm, D_MODEL), lambda i, j: (i, 0)), _wt_block(tn, row_of)]
    args = [h, wt]
    if scale is not None and not isinstance(scale, float):
        in_specs.append(pl.BlockSpec((1, tn), lambda i, j: (0, j)))
        args.append(scale)
        scale = "row"
    return pl.pallas_call(
        functools.partial(_proj_kernel, scale=scale),
        grid=(TOKENS // tm, ncols // tn),
        in_specs=in_specs,
        out_specs=pl.BlockSpec((tm, tn), lambda i, j: (i, j)),
        out_shape=jax.ShapeDtypeStruct((TOKENS, ncols), BF16),
        compiler_params=_params(2),
        name=name,
    )(*args)


def _proj_vt_kernel(h_ref, wt_ref, o_ref):
    acc_t = lax.dot_general(wt_ref[...].astype(BF16), h_ref[...], NT_DIMS,
                            preferred_element_type=F32)
    ones = jnp.ones((VT_ROWS - A_HEAD_DIM, KB), o_ref.dtype)
    for t in range(o_ref.shape[0]):
        for hh in range(acc_t.shape[0] // A_HEAD_DIM):
            r0 = hh * VT_ROWS
            o_ref[t, r0:r0 + A_HEAD_DIM] = acc_t[hh * A_HEAD_DIM:(hh + 1) * A_HEAD_DIM,
                                                 t * KB:(t + 1) * KB].astype(o_ref.dtype)
            o_ref[t, r0 + A_HEAD_DIM:r0 + VT_ROWS] = ones


def _proj_vt(h, wt, row0, tm=2048, tn=512):
    per_batch = SEQ // tm
    rows = tn // A_HEAD_DIM * VT_ROWS
    return pl.pallas_call(
        _proj_vt_kernel,
        grid=(TOKENS // tm, A_WIDTH // tn),
        in_specs=[pl.BlockSpec((tm, D_MODEL), lambda i, j: (i, 0)),
                  _wt_block(tn, lambda i, j: row0 + j * tn)],
        out_specs=pl.BlockSpec((None, tm // KB, rows, KB),
                               lambda i, j: (i // per_batch, i % per_batch, j, 0)),
        out_shape=jax.ShapeDtypeStruct((BATCH, NKB, A_HEADS * VT_ROWS, KB), BF16),
        compiler_params=_params(2),
        name="proj_vt",
    )(h, wt)


def _proj_idx_kernel(h_ref, wik_ref, wiw_ref, ik_ref, iwt_ref):
    h = h_ref[...]
    wik = wik_ref[...].astype(BF16)
    ik = lax.dot_general(h, jnp.concatenate([wik, wik], axis=0), NT_DIMS,
                         preferred_element_type=F32)
    mu = jnp.mean(ik, axis=-1, keepdims=True)
    var = jnp.mean((ik - mu) ** 2, axis=-1, keepdims=True)
    ik_ref[...] = ((ik - mu) * lax.rsqrt(var + EPS)).astype(ik_ref.dtype)
    iwt = lax.dot_general(wiw_ref[...].astype(BF16), h, NT_DIMS, preferred_element_type=F32)
    iwt_ref[...] = iwt * (IDX_HEADS ** -0.5)


def _proj_idx(h, wt, tm=2048):
    return pl.pallas_call(
        _proj_idx_kernel,
        grid=(TOKENS // tm,),
        in_specs=[pl.BlockSpec((tm, D_MODEL), lambda i: (i, 0)),
                  _wt_block(IDX_DIM, lambda i: COL_IK),
                  _wt_block(IDX_HEADS, lambda i: COL_IW)],
        out_specs=[pl.BlockSpec((tm, LANES), lambda i: (i, 0)),
                   pl.BlockSpec((IDX_HEADS, tm), lambda i: (0, i))],
        out_shape=[jax.ShapeDtypeStruct((TOKENS, LANES), BF16),
                   jax.ShapeDtypeStruct((IDX_HEADS, TOKENS), F32)],
        compiler_params=_params(1),
        name="proj_idx",
    )(h, wt, wt)


def _proj_rope_kernel(h_ref, w_ref, pos_ref, freq_ref, o_ref, cos_scr, sin_scr, *,
                      tn, k_tile0, k_scale):
    @pl.when(pl.program_id(1) == 0)
    def _():
        hm = pos_ref.shape[0]
        ang = pos_ref[...].astype(F32) * freq_ref[...]
        c = jnp.cos(ang)
        s = jnp.sin(ang)
        cr = pltpu.roll(c, LANES // 2, 1)
        sr = pltpu.roll(s, LANES // 2, 1)
        low = lax.broadcasted_iota(jnp.int32, c.shape, 1) < LANES // 2
        cos_scr[:hm] = jnp.where(low, c, cr)
        cos_scr[hm:] = jnp.where(low, cr, c)
        sin_scr[:hm] = jnp.where(low, -s, sr)
        sin_scr[hm:] = jnp.where(low, -sr, s)

    acc = lax.dot_general(h_ref[...], w_ref[...].astype(BF16), NT_DIMS,
                          preferred_element_type=F32)
    cos = cos_scr[...]
    sin = sin_scr[...]
    scale = jnp.where(pl.program_id(1) >= k_tile0, k_scale, 1.0).astype(F32)
    for g in range(tn // LANES):
        xg = acc[:, g * LANES:(g + 1) * LANES]
        rot = xg * cos + pltpu.roll(xg, LANES // 2, 1) * sin
        o_ref[:, g * LANES:(g + 1) * LANES] = (rot * scale).astype(o_ref.dtype)


def _proj_rope(h, wt, positions, tm=2048, tn=512):
    ncols = 2 * R_WIDTH
    half = R_KEY_DIM // 2
    inv_freq = ROPE_BASE ** (-jnp.arange(half, dtype=F32) / half)
    freq2 = jnp.concatenate([inv_freq, inv_freq]).reshape(1, R_KEY_DIM)
    pos_t = positions.reshape(TOKENS // tm, 2, tm // 2, 1)
    pos2 = jnp.concatenate([jnp.broadcast_to(pos_t[:, 0], (TOKENS // tm, tm // 2, half)),
                            jnp.broadcast_to(pos_t[:, 1], (TOKENS // tm, tm // 2, half))], axis=-1)
    pos2 = pos2.reshape(TOKENS // 2, R_KEY_DIM)
    kern = functools.partial(_proj_rope_kernel, tn=tn, k_tile0=R_WIDTH // tn,
                             k_scale=R_KEY_DIM ** -0.5)
    return pl.pallas_call(
        kern,
        grid=(TOKENS // tm, ncols // tn),
        in_specs=[pl.BlockSpec((tm, D_MODEL), lambda i, j: (i, 0)),
                  _wt_block(tn, lambda i, j: COL_RQ + j * tn),
                  pl.BlockSpec((tm // 2, R_KEY_DIM), lambda i, j: (i, 0)),
                  pl.BlockSpec((1, R_KEY_DIM), lambda i, j: (0, 0))],
        out_specs=pl.BlockSpec((tm, tn), lambda i, j: (i, j)),
        out_shape=jax.ShapeDtypeStruct((TOKENS, ncols), BF16),
        scratch_shapes=[pltpu.VMEM((tm, R_KEY_DIM), F32), pltpu.VMEM((tm, R_KEY_DIM), F32)],
        compiler_params=_params(2),
        name="proj_rope",
    )(h, wt, pos2, freq2)


def _to_key(a):
    bits = pltpu.bitcast(a, jnp.int32)
    return jnp.where(bits < 0, bits ^ jnp.int32(0x7FFFFFFF), bits)


def _dsa_kernel(pmin_ref, pmax_ref,
                q_ref, k_ref, vt_ref, iq_ref, ikd_ref, iwt_ref, posq_ref, posk_ref, az_ref,
                tab_ref, far_ref, o_ref,
                idx_scr, idxb_scr, off_scr, dig_scr, iqm_scr, acc_scr, m_scr):
    b = pl.program_id(0)
    qi = pl.program_id(1)
    nkb = qi + 1

    lane = lax.broadcasted_iota(jnp.int32, (QB, LANES), 1)
    for p in range(IDX_HEADS // 2):
        pair = iq_ref[:, p * LANES:(p + 1) * LANES].astype(F32)
        iqm_scr[2 * p] = jnp.where(lane < IDX_DIM, pair, 0.0).astype(BF16)
        iqm_scr[2 * p + 1] = jnp.where(lane >= IDX_DIM, pair, 0.0).astype(BF16)
    iwt = iwt_ref[...]

    rr = lax.broadcasted_iota(jnp.int32, (KB, QB), 0)
    cc = lax.broadcasted_iota(jnp.int32, (KB, QB), 1)

    def idx_scores(kb):
        s0 = pl.multiple_of(kb * KB, KB)
        kid = ikd_ref[pl.ds(s0, KB), :]
        acc = jnp.zeros((KB, QB), F32)
        for h in range(IDX_HEADS):
            sc = lax.dot_general(kid, iqm_scr[h], NT_DIMS, preferred_element_type=F32)
            acc = acc + jnp.maximum(sc, 0.0) * iwt[h:h + 1, :]
        return acc

    def idx_store(kb, acc, diagonal):
        if diagonal:
            admissible = (rr // CHUNK) <= (cc // CHUNK)
            acc = jnp.where(admissible, acc, -jnp.inf)
        idx_scr[kb] = acc
        idxb_scr[kb] = acc.astype(BF16)

    def idx_tiles(kbs, diagonal_last):
        accs = [idx_scores(kb) for kb in kbs]
        for t, kb in enumerate(kbs):
            idx_store(kb, accs[t], diagonal_last and t == len(kbs) - 1)

    def idx_pair_body(j, c):
        idx_tiles([2 * j, 2 * j + 1], False)
        return c

    lax.fori_loop(0, lax.shift_right_logical(qi, 1), idx_pair_body, 0)
    qi_odd = (qi & 1) == 1
    pl.when(qi_odd)(lambda: idx_tiles([qi - 1, qi], True))
    pl.when(jnp.logical_not(qi_odd))(lambda: idx_tiles([qi], True))

    nkb_sel = jnp.where(qi > 0, nkb, 0)
    npair_sel = lax.shift_right_logical(nkb_sel + 1, 1)
    sign_bit = jnp.int32(INT_MIN)

    @pl.when((nkb & 1) == 1)
    def _():
        idx_scr[nkb] = jnp.full((KB, QB), -jnp.inf, F32)
        idxb_scr[nkb] = jnp.full((KB, QB), -jnp.inf, BF16)
        dig_scr[nkb] = jnp.full((KB, QB), -300.0, BF16)

    def pattern_to_float(u):
        s = u ^ sign_bit
        return pltpu.bitcast(jnp.where(s < 0, s ^ jnp.int32(0x7FFFFFFF), s), F32)

    def count_ge(plane_ref, cand, group):
        dt = plane_ref.dtype
        one = jnp.ones((), dt)
        zero = jnp.zeros((), dt)

        def body(j, acc):
            slabs = []
            for kb in (2 * j, 2 * j + 1):
                part = jnp.where(plane_ref[kb] >= cand, one, zero)
                slabs += [part[group * g:group * (g + 1)] for g in range(KB // group)]
            while len(slabs) > 1:
                slabs = [slabs[i] + slabs[i + 1] for i in range(0, len(slabs), 2)]
            return acc + slabs[0]

        acc = lax.fori_loop(0, npair_sel, body, jnp.zeros((group, QB), dt))
        return jnp.sum(acc.astype(F32), axis=0, keepdims=True)

    def bisect(n_bits, count_fn):
        def bit_body(it, ans):
            cand = ans | jnp.left_shift(jnp.int32(1), n_bits - 1 - it)
            return jnp.where(count_fn(cand) >= TOPK, cand, ans)

        return lax.fori_loop(0, jnp.where(qi > 0, n_bits, 0), bit_body,
                             jnp.zeros((1, QB), jnp.int32))

    def bf16_candidate(c16):
        s16 = c16 - 32768
        b16 = jnp.where(s16 < 0, s16 ^ 0x7FFF, s16)
        return pltpu.bitcast(b16 << 16, F32).astype(BF16)

    c16 = bisect(16, lambda c: count_ge(idxb_scr, bf16_candidate(c), 16))
    s16 = c16 - 32768
    base_bits = jnp.where(s16 < 0, s16 ^ 0x7FFF, s16) << 16
    base = pltpu.bitcast(base_bits, F32)
    expo = (base_bits >> 23) & 0xFF
    scale = pltpu.bitcast(jnp.clip(278 - expo, 1, 254) << 23, F32)
    unit = pltpu.bitcast(jnp.clip(expo - 24, 1, 254) << 23, F32)
    span = 65536.0
    radix = 512.0
    hi_shift = 192.0
    lo_shift = 256.0

    def hi_body(kb, c):
        q = jnp.clip((idx_scr[kb] - base) * scale + span, -radix, 3.0 * span + radix - 1.0)
        off_scr[kb] = q
        dig_scr[kb] = (jnp.floor(q * (1.0 / radix)) - hi_shift).astype(BF16)
        return c

    lax.fori_loop(0, nkb_sel, hi_body, 0)
    c_hi = bisect(9, lambda c: count_ge(dig_scr, (c - 192).astype(F32).astype(BF16), 16))
    c_hi = c_hi.astype(F32)

    def lo_body(kb, c):
        q = off_scr[kb]
        hi = jnp.floor(q * (1.0 / radix))
        lo = jnp.where(hi > c_hi, lo_shift,
                       jnp.where(hi == c_hi, q - radix * c_hi - lo_shift, -lo_shift - 2.0))
        dig_scr[kb] = lo.astype(BF16)
        return c

    lax.fori_loop(0, nkb_sel, lo_body, 0)
    c_lo = bisect(9, lambda c: count_ge(dig_scr, (c - 256).astype(F32).astype(BF16), 16))
    thr = base + (radix * c_hi + c_lo.astype(F32) - span) * unit
    thr = jnp.where(qi > 0, thr, jnp.finfo(F32).min)

    def count_where(pred_fn):
        def body(kb, acc):
            part = jnp.where(pred_fn(kb, idx_scr[kb]), 1.0, 0.0)
            return acc + jnp.sum(part.reshape(KB // 8, 8, QB), axis=0)
        acc = lax.fori_loop(0, nkb, body, jnp.zeros((8, QB), F32))
        return jnp.sum(acc, axis=0, keepdims=True)

    cnt_ge_thr = count_ge(idx_scr, thr, 8)
    has_tie = jnp.logical_and(qi > 0, jnp.max(cnt_ge_thr) > TOPK)

    @pl.when(has_tie)
    def _():
        cnt_gt = count_where(lambda kb, kk: kk > thr)
        need = TOPK - cnt_gt

        def key_index(kb):
            return kb * KB + rr

        def jb_body(it, j0):
            cand = j0 | jnp.left_shift(jnp.int32(1), 11 - it)
            f = count_where(lambda kb, kk: jnp.logical_and(kk == thr, key_index(kb) < cand))
            return jnp.where(f < need, cand, j0)

        j0 = lax.fori_loop(0, 12, jb_body, jnp.zeros((1, QB), jnp.int32))
        jstar = j0 + 1

        def fix_body(kb, c):
            kk = idx_scr[kb]
            drop = jnp.logical_and(kk == thr, key_index(kb) >= jstar)
            idx_scr[kb] = jnp.where(drop, -jnp.inf, kk)
            return c

        lax.fori_loop(0, nkb, fix_body, 0)

    m_scr[...] = jnp.full(m_scr.shape, NEG, F32)
    acc_scr[...] = jnp.zeros(acc_scr.shape, F32)
    posq = posq_ref[...]
    pmin_q = pmin_ref[b, qi]
    half = N_BUCKETS // 2
    max_exact = half // 2

    def att_tiles(kbs, near):
        scores = {}
        for t, kb in enumerate(kbs):
            s0 = pl.multiple_of(kb * KB, KB)
            for h in range(A_HEADS):
                hs = slice(h * A_HEAD_DIM, (h + 1) * A_HEAD_DIM)
                scores[t, h] = lax.dot_general(k_ref[pl.ds(s0, KB), hs], q_ref[:, hs], NT_DIMS,
                                               preferred_element_type=F32)
        for t, kb in enumerate(kbs):
            s0 = pl.multiple_of(kb * KB, KB)
            madd = jnp.where(idx_scr[kb] >= thr, 0.0, NEG)
            if near:
                pk = posk_ref[pl.ds(s0, KB), :]
                rel = jnp.concatenate([pk] * (QB // LANES), axis=1) - posq
                n = jnp.abs(rel)
                nf = jnp.maximum(n, 1).astype(F32)
                large = max_exact + (jnp.log(nf / max_exact) / math.log(MAX_DISTANCE / max_exact)
                                     * (half - max_exact)).astype(jnp.int32)
                large = jnp.minimum(large, half - 1)
                bucket = jnp.where(rel > 0, half, 0) + jnp.where(n < max_exact, n, large)
            probs, alphas = [], []
            for h in range(A_HEADS):
                if near:
                    row = jnp.broadcast_to(tab_ref[h:h + 1, :], (KB, LANES))
                    bias = jnp.concatenate(
                        [jnp.take_along_axis(row, bucket[:, g * LANES:(g + 1) * LANES], axis=1)
                         for g in range(QB // LANES)], axis=1)
                    s = scores[t, h] + ((bias - far_ref[h]) * LOG2E + madd)
                else:
                    s = scores[t, h] + madd
                m_old = m_scr[h]
                m_new = jnp.maximum(m_old, jnp.max(s, axis=0, keepdims=True))
                alphas.append(jnp.exp2(m_old - m_new))
                probs.append(jnp.exp2(s - m_new).astype(BF16))
                m_scr[h] = m_new
            for h in range(A_HEADS):
                pv = jnp.dot(vt_ref[kb, h * VT_ROWS:(h + 1) * VT_ROWS, :], probs[h],
                             preferred_element_type=F32)
                acc_scr[h] = alphas[h] * acc_scr[h] + pv

    def is_far(kb):
        return (pmin_q - pmax_ref[b, kb]) >= MAX_DISTANCE

    def att_single(kb, c):
        far = is_far(kb)
        pl.when(far)(lambda: att_tiles([kb], False))
        pl.when(jnp.logical_not(far))(lambda: att_tiles([kb], True))
        return c

    def att_pair_body(j, c):
        kb0 = 2 * j
        kb1 = kb0 + 1
        both_far = jnp.logical_and(kb1 < nkb, jnp.logical_and(is_far(kb0), is_far(kb1)))
        pl.when(both_far)(lambda: att_tiles([kb0, kb1], False))

        @pl.when(jnp.logical_not(both_far))
        def _():
            lax.fori_loop(kb0, jnp.minimum(kb1 + 1, nkb), att_single, 0)

        return c

    lax.fori_loop(0, lax.shift_right_logical(nkb + 1, 1), att_pair_body, 0)

    for h in range(A_HEADS):
        hs = slice(h * A_HEAD_DIM, (h + 1) * A_HEAD_DIM)
        z = az_ref[:, hs].astype(F32)
        out_t = acc_scr[h, :A_HEAD_DIM] / acc_scr[h, A_HEAD_DIM:A_HEAD_DIM + 1]
        o_ref[:, hs] = (out_t.T * (z * jax.nn.sigmoid(z))).astype(o_ref.dtype)


def _dsa(pa, vt, iq, ikd, iwt, positions, rel_bias):
    pa3 = pa.reshape(BATCH, SEQ, PA_WIDTH)
    iq3 = iq.reshape(BATCH, SEQ, A_WIDTH)
    ikd3 = ikd.reshape(BATCH, SEQ, LANES)
    posq = positions.reshape(BATCH, 1, SEQ)
    posk = jnp.broadcast_to(positions[:, :, None], (BATCH, SEQ, LANES))
    pblk = positions.reshape(BATCH, NKB, KB)
    pmin = jnp.min(pblk, axis=-1)
    pmax = jnp.max(pblk, axis=-1)
    tab = jnp.zeros((A_HEADS, LANES), F32).at[:, :N_BUCKETS].set(rel_bias.astype(F32).T)
    far = rel_bias[N_BUCKETS // 2 - 1, :].astype(F32)

    grid_spec = pltpu.PrefetchScalarGridSpec(
        num_scalar_prefetch=2,
        grid=(BATCH, NQB),
        in_specs=[
            pl.BlockSpec((None, QB, A_WIDTH), lambda b, i, *_: (b, i, PA_Q)),
            pl.BlockSpec((None, SEQ, A_WIDTH), lambda b, i, *_: (b, 0, PA_K),
                         pipeline_mode=pl.Buffered(1)),
            pl.BlockSpec((None, NKB, A_HEADS * VT_ROWS, KB), lambda b, i, *_: (b, 0, 0, 0),
                         pipeline_mode=pl.Buffered(1)),
            pl.BlockSpec((None, QB, A_WIDTH), lambda b, i, *_: (b, i, 0)),
            pl.BlockSpec((None, SEQ, LANES), lambda b, i, *_: (b, 0, 0)),
            pl.BlockSpec((IDX_HEADS, QB), lambda b, i, *_: (0, b * NQB + i)),
            pl.BlockSpec((None, 1, QB), lambda b, i, *_: (b, 0, i)),
            pl.BlockSpec((None, SEQ, LANES), lambda b, i, *_: (b, 0, 0)),
            pl.BlockSpec((None, QB, A_WIDTH), lambda b, i, *_: (b, i, PA_AZ)),
            pl.BlockSpec((A_HEADS, LANES), lambda b, i, *_: (0, 0)),
            pl.BlockSpec(memory_space=pltpu.SMEM),
        ],
        out_specs=pl.BlockSpec((None, QB, A_WIDTH), lambda b, i, *_: (b, i, 0)),
        scratch_shapes=[
            pltpu.VMEM((NKB, KB, QB), F32),
            pltpu.VMEM((NKB, KB, QB), BF16),
            pltpu.VMEM((NKB, KB, QB), F32),
            pltpu.VMEM((NKB, KB, QB), BF16),
            pltpu.VMEM((IDX_HEADS, QB, LANES), BF16),
            pltpu.VMEM((A_HEADS, VT_ROWS, QB), F32),
            pltpu.VMEM((A_HEADS, 1, QB), F32),
        ],
    )
    out = pl.pallas_call(
        _dsa_kernel,
        grid_spec=grid_spec,
        out_shape=jax.ShapeDtypeStruct((BATCH, SEQ, A_WIDTH), BF16),
        compiler_params=_params(2),
        name="dsa",
    )(pmin, pmax, pa3, pa3, vt, iq3, ikd3, iwt, posq, posk, pa3, tab, far)
    return out.reshape(TOKENS, A_WIDTH)


def _ret_kernel(cdec_ref, q_ref, k_ref, v_ref, z_ref, gain_ref, dec_ref, te_ref, fs_ref, o_ref,
                state_scr):
    @pl.when(pl.program_id(1) == 0)
    def _():
        state_scr[...] = jnp.zeros(state_scr.shape, F32)

    heads = [slice(h * R_KEY_DIM, (h + 1) * R_KEY_DIM) for h in range(R_HEADS)]

    group = 4

    def chunk_group_body(g, carry):
        rows = [pl.ds(pl.multiple_of((g * group + t) * CHUNK, CHUNK), CHUNK) for t in range(group)]
        scores, kv = [], []
        for t in range(group):
            for h, hs in enumerate(heads):
                k = k_ref[rows[t], hs]
                scores.append(lax.dot_general(q_ref[rows[t], hs], k, NT_DIMS,
                                              preferred_element_type=F32))
                ke = (k.astype(F32) * te_ref[h]).T.astype(BF16)
                kv.append(jnp.dot(ke, v_ref[rows[t], hs], preferred_element_type=F32))
        cross = []
        states = [state_scr[h] for h in range(R_HEADS)]
        for t in range(group):
            for h, hs in enumerate(heads):
                qs = (q_ref[rows[t], hs].astype(F32) * fs_ref[h]).astype(BF16)
                cross.append(jnp.dot(qs, states[h].astype(BF16), preferred_element_type=F32))
                states[h] = states[h] * cdec_ref[h] + kv[t * R_HEADS + h]
        for h in range(R_HEADS):
            state_scr[h] = states[h]
        for t in range(group):
            for h, hs in enumerate(heads):
                i = t * R_HEADS + h
                sc = (scores[i] * dec_ref[h]).astype(BF16)
                y = jnp.dot(sc, v_ref[rows[t], hs], preferred_element_type=F32) + cross[i]
                mu = jnp.mean(y, axis=-1, keepdims=True)
                var = jnp.mean((y - mu) ** 2, axis=-1, keepdims=True)
                yn = (y - mu) * lax.rsqrt(var + EPS) * gain_ref[:, hs]
                z = z_ref[rows[t], hs].astype(F32)
                o_ref[rows[t], hs] = (yn * (z * jax.nn.sigmoid(z))).astype(o_ref.dtype)
        return carry

    lax.fori_loop(0, RB // (CHUNK * group), chunk_group_body, 0)


def _retention(rqk, pb, gn_gain):
    log_g = jnp.log(1.0 - 2.0 ** (-5.0 - jnp.arange(R_HEADS, dtype=F32)))
    pos = jnp.arange(CHUNK, dtype=F32)
    dist = jnp.abs(pos[:, None] - pos[None, :])
    intra_decay = jnp.exp(log_g[:, None, None] * dist)
    to_end = jnp.exp(log_g[:, None] * (CHUNK - 1.0 - pos)[None, :])
    from_start = jnp.exp(log_g[:, None] * (pos + 1.0)[None, :])
    chunk_decay = jnp.exp(log_g * CHUNK)
    te = jnp.broadcast_to(to_end[:, :, None], (R_HEADS, CHUNK, R_KEY_DIM))
    fs = jnp.broadcast_to(from_start[:, :, None], (R_HEADS, CHUNK, R_KEY_DIM))

    rqk3 = rqk.reshape(BATCH, SEQ, 2 * R_WIDTH)
    pb3 = pb.reshape(BATCH, SEQ, PB_WIDTH)
    out = pl.pallas_call(
        _ret_kernel,
        grid=(BATCH, SEQ // RB),
        in_specs=[
            pl.BlockSpec(memory_space=pltpu.SMEM),
            pl.BlockSpec((None, RB, R_WIDTH), lambda b, i: (b, i, 0)),
            pl.BlockSpec((None, RB, R_WIDTH), lambda b, i: (b, i, 1)),
            pl.BlockSpec((None, RB, R_WIDTH), lambda b, i: (b, i, PB_RV)),
            pl.BlockSpec((None, RB, R_WIDTH), lambda b, i: (b, i, PB_RZ)),
            pl.BlockSpec((1, R_WIDTH), lambda b, i: (0, 0)),
            pl.BlockSpec((R_HEADS, CHUNK, CHUNK), lambda b, i: (0, 0, 0)),
            pl.BlockSpec((R_HEADS, CHUNK, R_KEY_DIM), lambda b, i: (0, 0, 0)),
            pl.BlockSpec((R_HEADS, CHUNK, R_KEY_DIM), lambda b, i: (0, 0, 0)),
        ],
        out_specs=pl.BlockSpec((None, RB, R_WIDTH), lambda b, i: (b, i, 0)),
        out_shape=jax.ShapeDtypeStruct((BATCH, SEQ, R_WIDTH), BF16),
        scratch_shapes=[pltpu.VMEM((R_HEADS, R_KEY_DIM, R_VAL_DIM), F32)],
        compiler_params=_params(2),
        name="retention",
    )(chunk_decay, rqk3, rqk3, pb3, pb3, gn_gain.reshape(1, R_WIDTH), intra_decay, te, fs)
    return out.reshape(TOKENS, R_WIDTH)


def _out_kernel(a_ref, b_ref, ga_ref, gb_ref, x_ref, p_ref, wa_ref, wb_ref, wo_ref, wp_ref, wg_ref,
                fg_ref, o_ref):
    ta = jnp.dot(a_ref[...], wa_ref[...], preferred_element_type=F32)
    tb = jnp.dot(b_ref[...], wb_ref[...], preferred_element_type=F32)
    merged = (jax.nn.sigmoid(ga_ref[...].astype(F32)) * ta
              + jax.nn.sigmoid(gb_ref[...].astype(F32)) * tb)
    r = x_ref[...] + jnp.dot(merged.astype(BF16), wo_ref[...], preferred_element_type=F32)
    u = jnp.dot(p_ref[...].astype(BF16), wp_ref[...], preferred_element_type=F32)
    g = jnp.dot(r.astype(BF16), wg_ref[...], preferred_element_type=F32)
    y = r + u * jax.nn.sigmoid(g)
    ms = jnp.mean(y * y, axis=-1, keepdims=True)
    o_ref[...] = y * lax.rsqrt(ms + EPS) * fg_ref[...]


def _output(a_out, b_out, pb, x2d, p2d, wa, wb, wo, wp, wg, final_gain, tm=256):
    def resident(shape):
        return pl.BlockSpec(shape, lambda i: (0, 0), pipeline_mode=pl.Buffered(1))

    return pl.pallas_call(
        _out_kernel,
        grid=(TOKENS // tm,),
        in_specs=[pl.BlockSpec((tm, A_WIDTH), lambda i: (i, 0)),
                  pl.BlockSpec((tm, R_WIDTH), lambda i: (i, 0)),
                  pl.BlockSpec((tm, D_MODEL), lambda i: (i, PB_GA * R_WIDTH // D_MODEL)),
                  pl.BlockSpec((tm, D_MODEL), lambda i: (i, PB_GB * R_WIDTH // D_MODEL)),
                  pl.BlockSpec((tm, D_MODEL), lambda i: (i, 0)),
                  pl.BlockSpec((tm, PLE_DIM), lambda i: (i, 0)),
                  resident((A_WIDTH, D_MODEL)),
                  resident((R_WIDTH, D_MODEL)),
                  resident((D_MODEL, D_MODEL)),
                  resident((PLE_DIM, D_MODEL)),
                  resident((D_MODEL, D_MODEL)),
                  pl.BlockSpec((1, D_MODEL), lambda i: (0, 0))],
        out_specs=pl.BlockSpec((tm, D_MODEL), lambda i: (i, 0)),
        out_shape=jax.ShapeDtypeStruct((TOKENS, D_MODEL), F32),
        compiler_params=_params(1),
        name="output",
    )(a_out, b_out, pb, pb, x2d, p2d, wa, wb, wo, wp, wg, final_gain.reshape(1, D_MODEL))


def kernel(x, p, positions, w_in, norm_gain, w_a_out, w_b_out, w_o, ret_gn_gain, w_ple, w_ple_gate,
           rel_bias, final_gain):
    assert x.shape == (BATCH, SEQ, D_MODEL) and w_in.shape == (1, D_MODEL, IN_WIDTH)
    x2d = x.reshape(TOKENS, D_MODEL)
    p2d = p[0].reshape(TOKENS, PLE_DIM)
    wt = jnp.swapaxes(w_in[0], 0, 1)
    q_scale = jnp.where(jnp.arange(PA_WIDTH) < A_WIDTH, A_HEAD_DIM ** -0.5 * LOG2E, 1.0)
    q_scale = q_scale.astype(F32).reshape(1, PA_WIDTH)

    h = _rmsnorm(x2d, norm_gain[0])
    pa = _proj(h, wt, "proj_att", 0, PA_WIDTH, scale=q_scale,
               skip=(2 * A_WIDTH, 3 * A_WIDTH))
    vt = _proj_vt(h, wt, 2 * A_WIDTH)
    iq = _proj(h, wt, "proj_iq", COL_IQ, IDX_HEADS * IDX_DIM, scale=IDX_DIM ** -0.5)
    pb = _proj(h, wt, "proj_ret", COL_RV, PB_WIDTH)
    rqk = _proj_rope(h, wt, positions)
    ikd, iwt = _proj_idx(h, wt)

    a_out = _dsa(pa, vt, iq, ikd, iwt, positions, rel_bias)
    b_out = _retention(rqk, pb, ret_gn_gain[0])

    out = _output(a_out, b_out, pb, x2d, p2d, w_a_out[0].astype(BF16), w_b_out[0].astype(BF16),
                  w_o[0].astype(BF16), w_ple[0].astype(BF16), w_ple_gate[0].astype(BF16), final_gain)
    return out.reshape(BATCH, SEQ, D_MODEL)
```

```python
import functools
import math

import jax
import jax.numpy as jnp
from jax import lax
from jax.experimental import pallas as pl
from jax.experimental.pallas import tpu as pltpu

D_MODEL = 2048
BATCH = 4
SEQ = 4096
TOKENS = BATCH * SEQ
CHUNK = 64
PLE_DIM = 256
EPS = 1e-6
A_HEADS = 8
A_HEAD_DIM = 128
A_WIDTH = A_HEADS * A_HEAD_DIM
IDX_HEADS = 16
IDX_DIM = 64
TOPK = min(256, SEQ // 4)
R_HEADS = 8
R_KEY_DIM = 128
R_VAL_DIM = 128
R_WIDTH = R_HEADS * R_VAL_DIM
ROPE_BASE = 10000.0
N_BUCKETS = 32
MAX_DISTANCE = 128

COL_IQ = 4 * A_WIDTH
COL_IK = COL_IQ + IDX_HEADS * IDX_DIM
COL_IW = COL_IK + IDX_DIM
COL_RQ = COL_IW + IDX_HEADS
COL_RV = COL_RQ + 2 * R_WIDTH
IN_WIDTH = COL_RV + 2 * R_WIDTH + 2 * D_MODEL

PA_Q, PA_K, PA_AZ = 0, 1, 2
PA_WIDTH = 3 * A_WIDTH
PB_RV, PB_RZ, PB_GA, PB_GB = 0, 1, 2, 4
PB_WIDTH = 2 * R_WIDTH + 2 * D_MODEL

LANES = 128
QB = 256
KB = 256
NQB = SEQ // QB
NKB = SEQ // KB
VT_ROWS = A_HEAD_DIM + 16
RB = 512
NEG = -1e30
LOG2E = math.log2(math.e)
VMEM_LIMIT = 56 * 1024 * 1024

F32 = jnp.float32
BF16 = jnp.bfloat16
NT_DIMS = (((1,), (1,)), ((), ()))


def _params(n_axes):
    return pltpu.CompilerParams(dimension_semantics=("arbitrary",) * n_axes,
                                vmem_limit_bytes=VMEM_LIMIT)


def _rmsnorm_kernel(x_ref, g_ref, o_ref):
    x = x_ref[...]
    ms = jnp.mean(x * x, axis=-1, keepdims=True)
    o_ref[...] = (x * lax.rsqrt(ms + EPS) * g_ref[...]).astype(o_ref.dtype)


def _rmsnorm(x2d, gain, tm=512):
    return pl.pallas_call(
        _rmsnorm_kernel,
        grid=(TOKENS // tm,),
        in_specs=[pl.BlockSpec((tm, D_MODEL), lambda i: (i, 0)),
                  pl.BlockSpec((1, D_MODEL), lambda i: (0, 0))],
        out_specs=pl.BlockSpec((tm, D_MODEL), lambda i: (i, 0)),
        out_shape=jax.ShapeDtypeStruct((TOKENS, D_MODEL), BF16),
        compiler_params=_params(1),
        name="rmsnorm",
    )(x2d, gain.reshape(1, D_MODEL))


def _wt_block(rows, row_of):
    assert rows % 8 == 0
    return pl.BlockSpec((pl.Element(rows), pl.Element(D_MODEL)),
                        lambda *g: (pl.multiple_of(row_of(*g), 8), 0))


def _proj_kernel(h_ref, wt_ref, *rest, scale):
    acc = lax.dot_general(h_ref[...], wt_ref[...].astype(BF16), NT_DIMS,
                          preferred_element_type=F32)
    if scale == "row":
        acc = acc * rest[0][...]
    elif scale is not None:
        acc = acc * scale
    rest[-1][...] = acc.astype(rest[-1].dtype)


def _proj(h, wt, name, row0, ncols, scale=None, skip=None, tm=2048, tn=512):
    def row_of(i, j):
        if skip is not None:
            j = jnp.where(j < skip[0] // tn, j, j + (skip[1] - skip[0]) // tn)
        return row0 + j * tn

    in_specs = [pl.BlockSpec((tm, D_MODEL), lambda i, j: (i, 0)), _wt_block(tn, row_of)]
    args = [h, wt]
    if scale is not None and not isinstance(scale, float):
        in_specs.append(pl.BlockSpec((1, tn), lambda i, j: (0, j)))
        args.append(scale)
        scale = "row"
    return pl.pallas_call(
        functools.partial(_proj_kernel, scale=scale),
        grid=(TOKENS // tm, ncols // tn),
        in_specs=in_specs,
        out_specs=pl.BlockSpec((tm, tn), lambda i, j: (i, j)),
        out_shape=jax.ShapeDtypeStruct((TOKENS, ncols), BF16),
        compiler_params=_params(2),
        name=name,
    )(*args)


def _proj_vt_kernel(h_ref, wt_ref, o_ref):
    acc_t = lax.dot_general(wt_ref[...].astype(BF16), h_ref[...], NT_DIMS,
                            preferred_element_type=F32)
    ones = jnp.ones((VT_ROWS - A_HEAD_DIM, KB), o_ref.dtype)
    for t in range(o_ref.shape[0]):
        for hh in range(acc_t.shape[0] // A_HEAD_DIM):
            r0 = hh * VT_ROWS
            o_ref[t, r0:r0 + A_HEAD_DIM] = acc_t[hh * A_HEAD_DIM:(hh + 1) * A_HEAD_DIM,
                                                 t * KB:(t + 1) * KB].astype(o_ref.dtype)
            o_ref[t, r0 + A_HEAD_DIM:r0 + VT_ROWS] = ones


def _proj_vt(h, wt, row0, tm=2048, tn=512):
    per_batch = SEQ // tm
    rows = tn // A_HEAD_DIM * VT_ROWS
    return pl.pallas_call(
        _proj_vt_kernel,
        grid=(TOKENS // tm, A_WIDTH // tn),
        in_specs=[pl.BlockSpec((tm, D_MODEL), lambda i, j: (i, 0)),
                  _wt_block(tn, lambda i, j: row0 + j * tn)],
        out_specs=pl.BlockSpec((None, tm // KB, rows, KB),
                               lambda i, j: (i // per_batch, i % per_batch, j, 0)),
        out_shape=jax.ShapeDtypeStruct((BATCH, NKB, A_HEADS * VT_ROWS, KB), BF16),
        compiler_params=_params(2),
        name="proj_vt",
    )(h, wt)


def _proj_idx_kernel(h_ref, wik_ref, wiw_ref, ik_ref, iwt_ref):
    h = h_ref[...]
    wik = wik_ref[...].astype(BF16)
    ik = lax.dot_general(h, jnp.concatenate([wik, wik], axis=0), NT_DIMS,
                         preferred_element_type=F32)
    mu = jnp.mean(ik, axis=-1, keepdims=True)
    var = jnp.mean((ik - mu) ** 2, axis=-1, keepdims=True)
    ik_ref[...] = ((ik - mu) * lax.rsqrt(var + EPS)).astype(ik_ref.dtype)
    iwt = lax.dot_general(wiw_ref[...].astype(BF16), h, NT_DIMS, preferred_element_type=F32)
    iwt_ref[...] = iwt * (IDX_HEADS ** -0.5)


def _proj_idx(h, wt, tm=2048):
    return pl.pallas_call(
        _proj_idx_kernel,
        grid=(TOKENS // tm,),
        in_specs=[pl.BlockSpec((tm, D_MODEL), lambda i: (i, 0)),
                  _wt_block(IDX_DIM, lambda i: COL_IK),
                  _wt_block(IDX_HEADS, lambda i: COL_IW)],
        out_specs=[pl.BlockSpec((tm, LANES), lambda i: (i, 0)),
                   pl.BlockSpec((IDX_HEADS, tm), lambda i: (0, i))],
        out_shape=[jax.ShapeDtypeStruct((TOKENS, LANES), BF16),
                   jax.ShapeDtypeStruct((IDX_HEADS, TOKENS), F32)],
        compiler_params=_params(1),
        name="proj_idx",
    )(h, wt, wt)


def _proj_rope_kernel(h_ref, w_ref, pos_ref, freq_ref, o_ref, cos_scr, sin_scr, *,
                      tn, k_tile0, k_scale):
    @pl.when(pl.program_id(1) == 0)
    def _():
        hm = pos_ref.shape[0]
        ang = pos_ref[...].astype(F32) * freq_ref[...]
        c = jnp.cos(ang)
        s = jnp.sin(ang)
        cr = pltpu.roll(c, LANES // 2, 1)
        sr = pltpu.roll(s, LANES // 2, 1)
        low = lax.broadcasted_iota(jnp.int32, c.shape, 1) < LANES // 2
        cos_scr[:hm] = jnp.where(low, c, cr)
        cos_scr[hm:] = jnp.where(low, cr, c)
        sin_scr[:hm] = jnp.where(low, -s, sr)
        sin_scr[hm:] = jnp.where(low, -sr, s)

    acc = lax.dot_general(h_ref[...], w_ref[...].astype(BF16), NT_DIMS,
                          preferred_element_type=F32)
    cos = cos_scr[...]
    sin = sin_scr[...]
    scale = jnp.where(pl.program_id(1) >= k_tile0, k_scale, 1.0).astype(F32)
    for g in range(tn // LANES):
        xg = acc[:, g * LANES:(g + 1) * LANES]
        rot = xg * cos + pltpu.roll(xg, LANES // 2, 1) * sin
        o_ref[:, g * LANES:(g + 1) * LANES] = (rot * scale).astype(o_ref.dtype)


def _proj_rope(h, wt, positions, tm=2048, tn=512):
    ncols = 2 * R_WIDTH
    half = R_KEY_DIM // 2
    inv_freq = ROPE_BASE ** (-jnp.arange(half, dtype=F32) / half)
    freq2 = jnp.concatenate([inv_freq, inv_freq]).reshape(1, R_KEY_DIM)
    pos_t = positions.reshape(TOKENS // tm, 2, tm // 2, 1)
    pos2 = jnp.concatenate([jnp.broadcast_to(pos_t[:, 0], (TOKENS // tm, tm // 2, half)),
                            jnp.broadcast_to(pos_t[:, 1], (TOKENS // tm, tm // 2, half))], axis=-1)
    pos2 = pos2.reshape(TOKENS // 2, R_KEY_DIM)
    kern = functools.partial(_proj_rope_kernel, tn=tn, k_tile0=R_WIDTH // tn,
                             k_scale=R_KEY_DIM ** -0.5)
    return pl.pallas_call(
        kern,
        grid=(TOKENS // tm, ncols // tn),
        in_specs=[pl.BlockSpec((tm, D_MODEL), lambda i, j: (i, 0)),
                  _wt_block(tn, lambda i, j: COL_RQ + j * tn),
                  pl.BlockSpec((tm // 2, R_KEY_DIM), lambda i, j: (i, 0)),
                  pl.BlockSpec((1, R_KEY_DIM), lambda i, j: (0, 0))],
        out_specs=pl.BlockSpec((tm, tn), lambda i, j: (i, j)),
        out_shape=jax.ShapeDtypeStruct((TOKENS, ncols), BF16),
        scratch_shapes=[pltpu.VMEM((tm, R_KEY_DIM), F32), pltpu.VMEM((tm, R_KEY_DIM), F32)],
        compiler_params=_params(2),
        name="proj_rope",
    )(h, wt, pos2, freq2)


def _dsa_kernel(pmin_ref, pmax_ref,
                q_ref, k_ref, vt_ref, iq_ref, ikd_ref, iwt_ref, posq_ref, posk_ref, az_ref,
                tab_ref, far_ref, o_ref,
                idx_scr, idxb_scr, off_scr, dig_scr, iqm_scr, acc_scr, m_scr):
    b = pl.program_id(0)
    qi = pl.program_id(1)
    nkb = qi + 1

    lane =lax.broadcasted_iota(jnp.int32, (QB, LANES), 1)
    for p in range(IDX_HEADS // 2):
        pair = iq_ref[:, p * LANES:(p + 1) * LANES].astype(F32)
        iqm_scr[2 * p] = jnp.where(lane < IDX_DIM, pair, 0.0).astype(BF16)
        iqm_scr[2 * p + 1] = jnp.where(lane >= IDX_DIM, pair, 0.0).astype(BF16)
    iwt = iwt_ref[...]

    rr = lax.broadcasted_iota(jnp.int32, (KB, QB), 0)
    cc = lax.broadcasted_iota(jnp.int32, (KB, QB), 1)

    def idx_scores(kb):
        s0 = pl.multiple_of(kb * KB, KB)
        kid = ikd_ref[pl.ds(s0, KB), :]
        acc = jnp.zeros((KB, QB), F32)
        for h in range(IDX_HEADS):
            sc = lax.dot_general(kid, iqm_scr[h], NT_DIMS, preferred_element_type=F32)
            acc = acc + jnp.maximum(sc, 0.0) * iwt[h:h + 1, :]
        return acc

    def idx_store(kb, acc, diagonal):
        if diagonal:
            admissible = (rr // CHUNK) <= (cc // CHUNK)
            acc = jnp.where(admissible, acc, -jnp.inf)
        idx_scr[kb] = acc
        idxb_scr[kb] = acc.astype(BF16)

    def idx_tiles(kbs, diagonal_last):
        accs = [idx_scores(kb) for kb in kbs]
        for t, kb in enumerate(kbs):
            idx_store(kb, accs[t], diagonal_last and t == len(kbs) - 1)

    def idx_pair_body(j, c):
        idx_tiles([2 * j, 2 * j + 1], False)
        return c

    lax.fori_loop(0, lax.shift_right_logical(qi, 1), idx_pair_body, 0)
    qi_odd = (qi & 1) == 1
    pl.when(qi_odd)(lambda: idx_tiles([qi - 1, qi], True))
    pl.when(jnp.logical_not(qi_odd))(lambda: idx_tiles([qi], True))

    nkb_sel = jnp.where(qi > 0, nkb, 0)
    npair_sel = lax.shift_right_logical(nkb_sel + 1, 1)

    @pl.when((nkb & 1) == 1)
    def _():
        idxb_scr[nkb] = jnp.full((KB, QB), -jnp.inf, BF16)
        dig_scr[nkb] = jnp.full((KB, QB), -300.0, BF16)

    def count_ge(plane_ref, cand, group):
        dt = plane_ref.dtype
        one = jnp.ones((), dt)
        zero = jnp.zeros((), dt)

        def body(j, acc):
            slabs = []
            for kb in (2 * j, 2 * j + 1):
                part = jnp.where(plane_ref[kb] >= cand, one, zero)
                slabs += [part[group * g:group * (g + 1)] for g in range(KB // group)]
            while len(slabs) > 1:
                slabs = [slabs[i] + slabs[i + 1] for i in range(0, len(slabs), 2)]
            return acc + slabs[0]

        acc = lax.fori_loop(0, npair_sel, body, jnp.zeros((group, QB), dt))
        return jnp.sum(acc.astype(F32), axis=0, keepdims=True)

    def bisect(n_bits, count_fn):
        def bit_body(it, ans):
            cand = ans | jnp.left_shift(jnp.int32(1), n_bits - 1 - it)
            return jnp.where(count_fn(cand) >= TOPK, cand, ans)

        return lax.fori_loop(0, jnp.where(qi > 0, n_bits, 0), bit_body,
                             jnp.zeros((1, QB), jnp.int32))

    def bf16_candidate(c16):
        s16 = c16 - 32768
        b16 = jnp.where(s16 < 0, s16 ^ 0x7FFF, s16)
        return pltpu.bitcast(b16 << 16, F32).astype(BF16)

    c16 = bisect(16, lambda c: count_ge(idxb_scr, bf16_candidate(c), 16))
    s16 = c16 - 32768
    base_bits = jnp.where(s16 < 0, s16 ^ 0x7FFF, s16) << 16
    base = pltpu.bitcast(base_bits, F32)
    expo = (base_bits >> 23) & 0xFF
    scale = pltpu.bitcast(jnp.clip(278 - expo, 1, 254) << 23, F32)
    unit = pltpu.bitcast(jnp.clip(expo - 24, 1, 254) << 23, F32)
    span = 65536.0
    radix = 512.0
    hi_shift = 192.0
    lo_shift = 256.0

    def hi_body(kb, c):
        q = jnp.clip((idx_scr[kb] - base) * scale + span, -radix, 3.0 * span + radix - 1.0)
        off_scr[kb] = q
        dig_scr[kb] = (jnp.floor(q * (1.0 / radix)) - hi_shift).astype(BF16)
        return c

    lax.fori_loop(0, nkb_sel, hi_body, 0)
    c_hi = bisect(9, lambda c: count_ge(dig_scr, (c - 192).astype(F32).astype(BF16), 16))
    c_hi = c_hi.astype(F32)

    def lo_body(kb, c):
        q = off_scr[kb]
        hi = jnp.floor(q * (1.0 / radix))
        lo = jnp.where(hi > c_hi, lo_shift,
                       jnp.where(hi == c_hi, q - radix * c_hi - lo_shift, -lo_shift - 2.0))
        dig_scr[kb] = lo.astype(BF16)
        return c

    lax.fori_loop(0, nkb_sel, lo_body, 0)
    c_lo = bisect(9, lambda c: count_ge(dig_scr, (c - 256).astype(F32).astype(BF16), 16))
    thr = base + (radix * c_hi + c_lo.astype(F32) - span) * unit
    thr = jnp.where(qi > 0, thr, jnp.finfo(F32).min)

    def count_where(pred_fn):
        def body(kb, acc):
            part = jnp.where(pred_fn(kb, idx_scr[kb]), 1.0, 0.0)
            return acc + jnp.sum(part.reshape(KB // 8, 8, QB), axis=0)
        acc = lax.fori_loop(0, nkb, body, jnp.zeros((8, QB), F32))
        return jnp.sum(acc, axis=0, keepdims=True)

    cnt_ge_thr = count_ge(dig_scr, (c_lo - 256).astype(F32).astype(BF16), 16)
    has_tie = jnp.logical_and(qi > 0, jnp.max(cnt_ge_thr) > TOPK)

    @pl.when(has_tie)
    def _():
        cnt_gt = count_where(lambda kb, kk: kk > thr)
        need = TOPK - cnt_gt

        def key_index(kb):
            return kb * KB + rr

        def jb_body(it, j0):
            cand = j0 | jnp.left_shift(jnp.int32(1), 11 - it)
            f = count_where(lambda kb, kk: jnp.logical_and(kk == thr, key_index(kb) < cand))
            return jnp.where(f < need, cand, j0)

        j0 = lax.fori_loop(0, 12, jb_body, jnp.zeros((1, QB), jnp.int32))
        jstar = j0 + 1

        def fix_body(kb, c):
            kk = idx_scr[kb]
            drop = jnp.logical_and(kk == thr, key_index(kb) >= jstar)
            idx_scr[kb] = jnp.where(drop, -jnp.inf, kk)
            return c

        lax.fori_loop(0, nkb, fix_body, 0)

    m_scr[...] = jnp.full(m_scr.shape, NEG, F32)
    acc_scr[...] = jnp.zeros(acc_scr.shape, F32)
    posq = posq_ref[...]
    pmin_q = pmin_ref[b, qi]
    half = N_BUCKETS // 2
    max_exact = half // 2

    def att_tiles(kbs, near):
        scores = {}
        for t, kb in enumerate(kbs):
            s0 = pl.multiple_of(kb * KB, KB)
            for h in range(A_HEADS):
                hs = slice(h * A_HEAD_DIM, (h + 1) * A_HEAD_DIM)
                scores[t, h] = lax.dot_general(k_ref[pl.ds(s0, KB), hs], q_ref[:, hs], NT_DIMS,
                                               preferred_element_type=F32)
        for t, kb in enumerate(kbs):
            s0 = pl.multiple_of(kb * KB, KB)
            madd = jnp.where(idx_scr[kb] >= thr, 0.0, NEG)
            if near:
                pk = posk_ref[pl.ds(s0, KB), :]
                rel = jnp.concatenate([pk] * (QB // LANES), axis=1) - posq
                n = jnp.abs(rel)
                nf = jnp.maximum(n, 1).astype(F32)
                large = max_exact + (jnp.log(nf / max_exact) / math.log(MAX_DISTANCE / max_exact)
                                     * (half - max_exact)).astype(jnp.int32)
                large = jnp.minimum(large, half - 1)
                bucket = jnp.where(rel > 0, half, 0) + jnp.where(n < max_exact, n, large)
            probs, alphas = [], []
            for h in range(A_HEADS):
                if near:
                    row = jnp.broadcast_to(tab_ref[h:h + 1, :], (KB, LANES))
                    bias = jnp.concatenate(
                        [jnp.take_along_axis(row, bucket[:, g * LANES:(g + 1) * LANES], axis=1)
                         for g in range(QB // LANES)], axis=1)
                    s = scores[t, h] + ((bias - far_ref[h]) * LOG2E + madd)
                else:
                    s = scores[t, h] + madd
                m_old = m_scr[h]
                m_new = jnp.maximum(m_old, jnp.max(s, axis=0, keepdims=True))
                alphas.append(jnp.exp2(m_old - m_new))
                probs.append(jnp.exp2(s - m_new).astype(BF16))
                m_scr[h] = m_new
            for h in range(A_HEADS):
                pv = jnp.dot(vt_ref[kb, h * VT_ROWS:(h + 1) * VT_ROWS, :], probs[h],
                             preferred_element_type=F32)
                acc_scr[h] = alphas[h] * acc_scr[h] + pv

    def is_far(kb):
        return (pmin_q - pmax_ref[b, kb]) >= MAX_DISTANCE

    def att_single(kb, c):
        far = is_far(kb)
        pl.when(far)(lambda: att_tiles([kb], False))
        pl.when(jnp.logical_not(far))(lambda: att_tiles([kb], True))
        return c

    def att_pair_body(j, c):
        kb0 = 2 * j
        kb1 = kb0 + 1
        both_far = jnp.logical_and(kb1 < nkb, jnp.logical_and(is_far(kb0), is_far(kb1)))
        pl.when(both_far)(lambda: att_tiles([kb0, kb1], False))

        @pl.when(jnp.logical_not(both_far))
        def _():
            lax.fori_loop(kb0, jnp.minimum(kb1 + 1, nkb), att_single, 0)

        return c

    lax.fori_loop(0, lax.shift_right_logical(nkb + 1, 1), att_pair_body, 0)

    for h in range(A_HEADS):
        hs = slice(h * A_HEAD_DIM, (h + 1) * A_HEAD_DIM)
        z = az_ref[:, hs].astype(F32)
        out_t = acc_scr[h, :A_HEAD_DIM] / acc_scr[h, A_HEAD_DIM:A_HEAD_DIM + 1]
        o_ref[:, hs] = (out_t.T * (z * jax.nn.sigmoid(z))).astype(o_ref.dtype)


def _dsa(pa, vt, iq, ikd, iwt, positions, rel_bias):
    pa3 = pa.reshape(BATCH, SEQ, PA_WIDTH)
    iq3 = iq.reshape(BATCH, SEQ, A_WIDTH)
    ikd3 = ikd.reshape(BATCH, SEQ, LANES)
    posq = positions.reshape(BATCH, 1, SEQ)
    posk = jnp.broadcast_to(positions[:, :, None], (BATCH, SEQ, LANES))
    pblk = positions.reshape(BATCH, NKB, KB)
    pmin = jnp.min(pblk, axis=-1)
    pmax = jnp.max(pblk, axis=-1)
    tab = jnp.zeros((A_HEADS, LANES), F32).at[:, :N_BUCKETS].set(rel_bias.astype(F32).T)
    far = rel_bias[N_BUCKETS // 2 - 1, :].astype(F32)

    grid_spec = pltpu.PrefetchScalarGridSpec(
        num_scalar_prefetch=2,
        grid=(BATCH, NQB),
        in_specs=[
            pl.BlockSpec((None, QB, A_WIDTH), lambda b, i, *_: (b, i, PA_Q)),
            pl.BlockSpec((None, SEQ, A_WIDTH), lambda b, i, *_: (b, 0, PA_K),
                         pipeline_mode=pl.Buffered(1)),
            pl.BlockSpec((None, NKB, A_HEADS * VT_ROWS, KB), lambda b, i, *_: (b, 0, 0, 0),
                         pipeline_mode=pl.Buffered(1)),
            pl.BlockSpec((None, QB, A_WIDTH), lambda b, i, *_: (b, i, 0)),
            pl.BlockSpec((None, SEQ, LANES), lambda b, i, *_: (b, 0, 0)),
            pl.BlockSpec((IDX_HEADS, QB), lambda b, i, *_: (0, b * NQB + i)),
            pl.BlockSpec((None, 1, QB), lambda b, i, *_: (b, 0, i)),
            pl.BlockSpec((None, SEQ, LANES), lambda b, i, *_: (b, 0, 0)),
            pl.BlockSpec((None, QB, A_WIDTH), lambda b, i, *_: (b, i, PA_AZ)),
            pl.BlockSpec((A_HEADS, LANES), lambda b, i, *_: (0, 0)),
            pl.BlockSpec(memory_space=pltpu.SMEM),
        ],
        out_specs=pl.BlockSpec((None, QB, A_WIDTH), lambda b, i, *_: (b, i, 0)),
        scratch_shapes=[
            pltpu.VMEM((NKB, KB, QB), F32),
            pltpu.VMEM((NKB, KB, QB), BF16),
            pltpu.VMEM((NKB, KB, QB), F32),
            pltpu.VMEM((NKB, KB, QB), BF16),
            pltpu.VMEM((IDX_HEADS, QB, LANES), BF16),
            pltpu.VMEM((A_HEADS, VT_ROWS, QB), F32),
            pltpu.VMEM((A_HEADS, 1, QB), F32),
        ],
    )
    out = pl.pallas_call(
        _dsa_kernel,
        grid_spec=grid_spec,
        out_shape=jax.ShapeDtypeStruct((BATCH, SEQ, A_WIDTH), BF16),
        compiler_params=_params(2),
        name="dsa",
    )(pmin, pmax, pa3, pa3, vt, iq3, ikd3, iwt, posq, posk, pa3, tab, far)
    return out.reshape(TOKENS, A_WIDTH)


def _ret_kernel(cdec_ref, q_ref, k_ref, v_ref, z_ref, gain_ref, dec_ref, te_ref, fs_ref, o_ref,
                state_scr):
    @pl.when(pl.program_id(1) == 0)
    def _():
        state_scr[...] = jnp.zeros(state_scr.shape, F32)

    heads = [slice(h * R_KEY_DIM, (h + 1) * R_KEY_DIM) for h in range(R_HEADS)]

    group = 4

    def chunk_group_body(g, carry):
        rows = [pl.ds(pl.multiple_of((g * group + t) * CHUNK, CHUNK), CHUNK) for t in range(group)]
        scores, kv = [], []
        for t in range(group):
            for h, hs in enumerate(heads):
                k = k_ref[rows[t], hs]
                scores.append(lax.dot_general(q_ref[rows[t], hs], k, NT_DIMS,
                                              preferred_element_type=F32))
                ke = (k.astype(F32) * te_ref[h]).T.astype(BF16)
                kv.append(jnp.dot(ke, v_ref[rows[t], hs], preferred_element_type=F32))
        cross = []
        states = [state_scr[h] for h in range(R_HEADS)]
        for t in range(group):
            for h, hs in enumerate(heads):
                qs = (q_ref[rows[t], hs].astype(F32) * fs_ref[h]).astype(BF16)
                cross.append(jnp.dot(qs, states[h].astype(BF16), preferred_element_type=F32))
                states[h] = states[h] * cdec_ref[h] + kv[t * R_HEADS + h]
        for h in range(R_HEADS):
            state_scr[h] = states[h]
        for t in range(group):
            for h, hs in enumerate(heads):
                i = t * R_HEADS + h
                sc = (scores[i] * dec_ref[h]).astype(BF16)
                y = jnp.dot(sc, v_ref[rows[t], hs], preferred_element_type=F32) + cross[i]
                mu = jnp.mean(y, axis=-1, keepdims=True)
                var = jnp.mean((y - mu) ** 2, axis=-1, keepdims=True)
                yn = (y - mu) * lax.rsqrt(var + EPS) * gain_ref[:, hs]
                z = z_ref[rows[t], hs].astype(F32)
                o_ref[rows[t], hs] = (yn * (z * jax.nn.sigmoid(z))).astype(o_ref.dtype)
        return carry

    lax.fori_loop(0, RB // (CHUNK * group), chunk_group_body, 0)


def _retention(rqk, pb, gn_gain):
    log_g = jnp.log(1.0 - 2.0 ** (-5.0 - jnp.arange(R_HEADS, dtype=F32)))
    pos = jnp.arange(CHUNK, dtype=F32)
    dist = jnp.abs(pos[:, None] - pos[None, :])
    intra_decay = jnp.exp(log_g[:, None, None] * dist)
    to_end = jnp.exp(log_g[:, None] * (CHUNK - 1.0 - pos)[None, :])
    from_start = jnp.exp(log_g[:, None] * (pos + 1.0)[None, :])
    chunk_decay = jnp.exp(log_g * CHUNK)
    te = jnp.broadcast_to(to_end[:, :, None], (R_HEADS, CHUNK, R_KEY_DIM))
    fs = jnp.broadcast_to(from_start[:, :, None], (R_HEADS, CHUNK, R_KEY_DIM))

    rqk3 = rqk.reshape(BATCH, SEQ, 2 * R_WIDTH)
    pb3 = pb.reshape(BATCH, SEQ, PB_WIDTH)
    out = pl.pallas_call(
        _ret_kernel,
        grid=(BATCH, SEQ // RB),
        in_specs=[
            pl.BlockSpec(memory_space=pltpu.SMEM),
            pl.BlockSpec((None, RB, R_WIDTH), lambda b, i: (b, i, 0)),
            pl.BlockSpec((None, RB, R_WIDTH), lambda b, i: (b, i, 1)),
            pl.BlockSpec((None, RB, R_WIDTH), lambda b, i: (b, i, PB_RV)),
            pl.BlockSpec((None, RB, R_WIDTH), lambda b, i: (b, i, PB_RZ)),
            pl.BlockSpec((1, R_WIDTH), lambda b, i: (0, 0)),
            pl.BlockSpec((R_HEADS, CHUNK, CHUNK), lambda b, i: (0, 0, 0)),
            pl.BlockSpec((R_HEADS, CHUNK, R_KEY_DIM), lambda b, i: (0, 0, 0)),
            pl.BlockSpec((R_HEADS, CHUNK, R_KEY_DIM), lambda b, i: (0, 0, 0)),
        ],
        out_specs=pl.BlockSpec((None, RB, R_WIDTH), lambda b, i: (b, i, 0)),
        out_shape=jax.ShapeDtypeStruct((BATCH, SEQ, R_WIDTH), BF16),
        scratch_shapes=[pltpu.VMEM((R_HEADS, R_KEY_DIM, R_VAL_DIM), F32)],
        compiler_params=_params(2),
        name="retention",
    )(chunk_decay, rqk3, rqk3, pb3, pb3, gn_gain.reshape(1, R_WIDTH), intra_decay, te, fs)
    return out.reshape(TOKENS, R_WIDTH)


def _out_kernel(a_ref, b_ref, ga_ref, gb_ref, x_ref, p_ref, wa_ref, wb_ref, wo_ref, wp_ref, wg_ref,
                fg_ref, o_ref):
    ta = jnp.dot(a_ref[...], wa_ref[...], preferred_element_type=F32)
    tb = jnp.dot(b_ref[...], wb_ref[...], preferred_element_type=F32)
    merged = (jax.nn.sigmoid(ga_ref[...].astype(F32)) * ta
              + jax.nn.sigmoid(gb_ref[...].astype(F32)) * tb)
    r = x_ref[...] + jnp.dot(merged.astype(BF16), wo_ref[...], preferred_element_type=F32)
    u = jnp.dot(p_ref[...].astype(BF16), wp_ref[...], preferred_element_type=F32)
    g = jnp.dot(r.astype(BF16), wg_ref[...], preferred_element_type=F32)
    y = r + u * jax.nn.sigmoid(g)
    ms = jnp.mean(y * y, axis=-1, keepdims=True)
    o_ref[...] = y * lax.rsqrt(ms + EPS) * fg_ref[...]


def _output(a_out, b_out, pb, x2d, p2d, wa, wb, wo, wp, wg, final_gain, tm=256):
    def resident(shape):
        return pl.BlockSpec(shape, lambda i: (0, 0), pipeline_mode=pl.Buffered(1))

    return pl.pallas_call(
        _out_kernel,
        grid=(TOKENS // tm,),
        in_specs=[pl.BlockSpec((tm, A_WIDTH), lambda i: (i, 0)),
                  pl.BlockSpec((tm, R_WIDTH), lambda i: (i, 0)),
                  pl.BlockSpec((tm, D_MODEL), lambda i: (i, PB_GA * R_WIDTH // D_MODEL)),
                  pl.BlockSpec((tm, D_MODEL), lambda i: (i, PB_GB * R_WIDTH // D_MODEL)),
                  pl.BlockSpec((tm, D_MODEL), lambda i: (i, 0)),
                  pl.BlockSpec((tm, PLE_DIM), lambda i: (i, 0)),
                  resident((A_WIDTH, D_MODEL)),
                  resident((R_WIDTH, D_MODEL)),
                  resident((D_MODEL, D_MODEL)),
                  resident((PLE_DIM, D_MODEL)),
                  resident((D_MODEL, D_MODEL)),
                  pl.BlockSpec((1, D_MODEL), lambda i: (0, 0))],
        out_specs=pl.BlockSpec((tm, D_MODEL), lambda i: (i, 0)),
        out_shape=jax.ShapeDtypeStruct((TOKENS, D_MODEL), F32),
        compiler_params=_params(1),
        name="output",
    )(a_out, b_out, pb, pb, x2d, p2d, wa, wb, wo, wp, wg, final_gain.reshape(1, D_MODEL))


def kernel(x, p, positions, w_in, norm_gain, w_a_out, w_b_out, w_o, ret_gn_gain, w_ple, w_ple_gate,
           rel_bias, final_gain):
    assert x.shape == (BATCH, SEQ, D_MODEL) and w_in.shape == (1, D_MODEL, IN_WIDTH)
    x2d = x.reshape(TOKENS, D_MODEL)
    p2d = p[0].reshape(TOKENS, PLE_DIM)
    wt = jnp.swapaxes(w_in[0], 0, 1)
    q_scale = jnp.where(jnp.arange(PA_WIDTH) < A_WIDTH, A_HEAD_DIM ** -0.5 * LOG2E, 1.0)
    q_scale = q_scale.astype(F32).reshape(1, PA_WIDTH)

    h = _rmsnorm(x2d, norm_gain[0])
    pa = _proj(h, wt, "proj_att", 0, PA_WIDTH, scale=q_scale,
               skip=(2 * A_WIDTH, 3 * A_WIDTH))
    vt = _proj_vt(h, wt, 2 * A_WIDTH)
    iq = _proj(h, wt, "proj_iq", COL_IQ, IDX_HEADS * IDX_DIM, scale=IDX_DIM ** -0.5)
    pb = _proj(h, wt, "proj_ret", COL_RV, PB_WIDTH)
    rqk = _proj_rope(h, wt, positions)
    ikd, iwt = _proj_idx(h, wt)

    a_out = _dsa(pa, vt, iq, ikd, iwt, positions, rel_bias)
    b_out = _retention(rqk, pb, ret_gn_gain[0])

    out = _output(a_out, b_out, pb, x2d, p2d, w_a_out[0].astype(BF16), w_b_out[0].astype(BF16),
                  w_o[0].astype(BF16), w_ple[0].astype(BF16), w_ple_gate[0].astype(BF16), final_gain)
    return out.reshape(BATCH, SEQ, D_MODEL)
```

```python
import functools
import math

import jax
import jax.numpy as jnp
from jax import lax
from jax.experimental import pallas as pl
from jax.experimental.pallas import tpu as pltpu

D_MODEL = 2048
BATCH = 4
SEQ = 4096
TOKENS = BATCH * SEQ
CHUNK = 64
PLE_DIM = 256
EPS = 1e-6
A_HEADS = 8
A_HEAD_DIM = 128
A_WIDTH = A_HEADS * A_HEAD_DIM
IDX_HEADS = 16
IDX_DIM = 64
TOPK = min(256, SEQ // 4)
R_HEADS = 8
R_KEY_DIM = 128
R_VAL_DIM = 128
R_WIDTH = R_HEADS * R_VAL_DIM
ROPE_BASE = 10000.0
N_BUCKETS = 32
MAX_DISTANCE = 128

COL_IQ = 4 * A_WIDTH
COL_IK = COL_IQ + IDX_HEADS * IDX_DIM
COL_IW = COL_IK + IDX_DIM
COL_RQ = COL_IW + IDX_HEADS
COL_RV = COL_RQ + 2 * R_WIDTH
IN_WIDTH = COL_RV + 2 * R_WIDTH + 2 * D_MODEL

PA_Q, PA_K, PA_AZ, PA_IQ = 0, 1, 2, 3
PA_WIDTH = 3 * A_WIDTH + IDX_HEADS * IDX_DIM
PB_RV, PB_RZ, PB_GA, PB_GB = 0, 1, 2, 4
PB_WIDTH = 2 * R_WIDTH + 2 * D_MODEL

LANES = 128
QB = 256
KB = 256
NQB = SEQ // QB
NKB = SEQ // KB
VT_ROWS = A_HEAD_DIM + 16
RB = 512
NEG = -1e30
LOG2E = math.log2(math.e)
VMEM_LIMIT = 56 * 1024 * 1024

F32 = jnp.float32
BF16 = jnp.bfloat16
NT_DIMS = (((1,), (1,)), ((), ()))


def _params(n_axes):
    return pltpu.CompilerParams(dimension_semantics=("arbitrary",) * n_axes,
                                vmem_limit_bytes=VMEM_LIMIT)


def _rmsnorm_kernel(x_ref, g_ref, o_ref):
    x = x_ref[...]
    ms = jnp.mean(x * x, axis=-1, keepdims=True)
    o_ref[...] = (x * lax.rsqrt(ms + EPS) * g_ref[...]).astype(o_ref.dtype)


def _rmsnorm(x2d, gain, tm=512):
    return pl.pallas_call(
        _rmsnorm_kernel,
        grid=(TOKENS // tm,),
        in_specs=[pl.BlockSpec((tm, D_MODEL), lambda i: (i, 0)),
                  pl.BlockSpec((1, D_MODEL), lambda i: (0, 0))],
        out_specs=pl.BlockSpec((tm, D_MODEL), lambda i: (i, 0)),
        out_shape=jax.ShapeDtypeStruct((TOKENS, D_MODEL), BF16),
        compiler_params=_params(1),
        name="rmsnorm",
    )(x2d, gain.reshape(1, D_MODEL))


def _wt_block(rows, row_of):
    assert rows % 8 == 0
    return pl.BlockSpec((pl.Element(rows), pl.Element(D_MODEL)),
                        lambda *g: (pl.multiple_of(row_of(*g), 8), 0))


def _proj_kernel(h_ref, wt_ref, *rest, scale):
    acc = lax.dot_general(h_ref[...], wt_ref[...].astype(BF16), NT_DIMS,
                          preferred_element_type=F32)
    if scale == "row":
        acc = acc * rest[0][...]
    elif scale is not None:
        acc = acc * scale
    rest[-1][...] = acc.astype(rest[-1].dtype)


def _proj(h, wt, name, row0, ncols, scale=None, skip=None, tm=2048, tn=512):
    def row_of(i, j):
        if skip is not None:
            j = jnp.where(j < skip[0] // tn, j, j + (skip[1] - skip[0]) // tn)
        return row0 + j * tn

    in_specs = [pl.BlockSpec((tm, D_MODEL), lambda i, j: (i, 0)), _wt_block(tn, row_of)]
    args = [h, wt]
    if scale is not None and not isinstance(scale, float):
        in_specs.append(pl.BlockSpec((1, tn), lambda i, j: (0, j)))
        args.append(scale)
        scale = "row"
    return pl.pallas_call(
        functools.partial(_proj_kernel, scale=scale),
        grid=(TOKENS // tm, ncols // tn),
        in_specs=in_specs,
        out_specs=pl.BlockSpec((tm, tn), lambda i, j: (i, j)),
        out_shape=jax.ShapeDtypeStruct((TOKENS, ncols), BF16),
        compiler_params=_params(2),
        name=name,
    )(*args)


def _proj_vt_kernel(h_ref, wt_ref, o_ref):
    acc_t = lax.dot_general(wt_ref[...].astype(BF16), h_ref[...], NT_DIMS,
                            preferred_element_type=F32)
    ones = jnp.ones((VT_ROWS - A_HEAD_DIM, KB), o_ref.dtype)
    for t in range(o_ref.shape[0]):
        for hh in range(acc_t.shape[0] // A_HEAD_DIM):
            r0 = hh * VT_ROWS
            o_ref[t, r0:r0 + A_HEAD_DIM] = acc_t[hh * A_HEAD_DIM:(hh + 1) * A_HEAD_DIM,
                                                 t * KB:(t + 1) * KB].astype(o_ref.dtype)
            o_ref[t, r0 + A_HEAD_DIM:r0 + VT_ROWS] = ones


def _proj_vt(h, wt, row0, tm=2048, tn=512):
    per_batch = SEQ // tm
    rows = tn // A_HEAD_DIM * VT_ROWS
    return pl.pallas_call(
        _proj_vt_kernel,
        grid=(TOKENS // tm, A_WIDTH // tn),
        in_specs=[pl.BlockSpec((tm, D_MODEL), lambda i, j: (i, 0)),
                  _wt_block(tn, lambda i, j: row0 + j * tn)],
        out_specs=pl.BlockSpec((None, tm // KB, rows, KB),
                               lambda i, j: (i // per_batch, i % per_batch, j, 0)),
        out_shape=jax.ShapeDtypeStruct((BATCH, NKB, A_HEADS * VT_ROWS, KB), BF16),
        compiler_params=_params(2),
        name="proj_vt",
    )(h, wt)


def _proj_idx_kernel(h_ref, wik_ref, wiw_ref, ik_ref, iwt_ref):
    h = h_ref[...]
    wik = wik_ref[...].astype(BF16)
    ik = lax.dot_general(h, jnp.concatenate([wik, wik], axis=0), NT_DIMS,
                         preferred_element_type=F32)
    mu = jnp.mean(ik, axis=-1, keepdims=True)
    var = jnp.mean((ik - mu) ** 2, axis=-1, keepdims=True)
    ik_ref[...] = ((ik - mu) * lax.rsqrt(var + EPS)).astype(ik_ref.dtype)
    iwt = lax.dot_general(wiw_ref[...].astype(BF16), h, NT_DIMS, preferred_element_type=F32)
    iwt_ref[...] = iwt * (IDX_HEADS ** -0.5)


def _proj_idx(h, wt, tm=2048):
    return pl.pallas_call(
        _proj_idx_kernel,
        grid=(TOKENS // tm,),
        in_specs=[pl.BlockSpec((tm, D_MODEL), lambda i: (i, 0)),
                  _wt_block(IDX_DIM, lambda i: COL_IK),
                  _wt_block(IDX_HEADS, lambda i: COL_IW)],
        out_specs=[pl.BlockSpec((tm, LANES), lambda i: (i, 0)),
                   pl.BlockSpec((IDX_HEADS, tm), lambda i: (0, i))],
        out_shape=[jax.ShapeDtypeStruct((TOKENS, LANES), BF16),
                   jax.ShapeDtypeStruct((IDX_HEADS, TOKENS), F32)],
        compiler_params=_params(1),
        name="proj_idx",
    )(h, wt, wt)


def _proj_rope_kernel(h_ref, w_ref, pos_ref, freq_ref, o_ref, cos_scr, sin_scr, *,
                      tn, k_tile0, k_scale):
    @pl.when(pl.program_id(1) == 0)
    def _():
        hm = pos_ref.shape[0]
        ang = pos_ref[...].astype(F32) * freq_ref[...]
        c = jnp.cos(ang)
        s = jnp.sin(ang)
        cr = pltpu.roll(c, LANES // 2, 1)
        sr = pltpu.roll(s, LANES // 2, 1)
        low = lax.broadcasted_iota(jnp.int32, c.shape, 1) < LANES // 2
        cos_scr[:hm] = jnp.where(low, c, cr)
        cos_scr[hm:] = jnp.where(low, cr, c)
        sin_scr[:hm] = jnp.where(low, -s, sr)
        sin_scr[hm:] = jnp.where(low, -sr, s)

    acc = lax.dot_general(h_ref[...], w_ref[...].astype(BF16), NT_DIMS,
                          preferred_element_type=F32)
    cos = cos_scr[...]
    sin = sin_scr[...]
    scale = jnp.where(pl.program_id(1) >= k_tile0, k_scale, 1.0).astype(F32)
    for g in range(tn // LANES):
        xg = acc[:, g * LANES:(g + 1) * LANES]
        rot = xg * cos + pltpu.roll(xg, LANES // 2, 1) * sin
        o_ref[:, g * LANES:(g + 1) * LANES] = (rot * scale).astype(o_ref.dtype)


def _proj_rope(h, wt, positions, tm=2048, tn=512):
    ncols = 2 * R_WIDTH
    half = R_KEY_DIM // 2
    inv_freq = ROPE_BASE ** (-jnp.arange(half, dtype=F32) / half)
    freq2 = jnp.concatenate([inv_freq, inv_freq]).reshape(1, R_KEY_DIM)
    pos_t = positions.reshape(TOKENS // tm, 2, tm // 2, 1)
    pos2 = jnp.concatenate([jnp.broadcast_to(pos_t[:, 0], (TOKENS // tm, tm // 2, half)),
                            jnp.broadcast_to(pos_t[:, 1], (TOKENS // tm, tm // 2, half))], axis=-1)
    pos2 = pos2.reshape(TOKENS // 2, R_KEY_DIM)
    kern = functools.partial(_proj_rope_kernel, tn=tn, k_tile0=R_WIDTH // tn,
                             k_scale=R_KEY_DIM ** -0.5)
    return pl.pallas_call(
        kern,
        grid=(TOKENS // tm, ncols // tn),
        in_specs=[pl.BlockSpec((tm, D_MODEL), lambda i, j: (i, 0)),
                  _wt_block(tn, lambda i, j: COL_RQ + j * tn),
                  pl.BlockSpec((tm // 2, R_KEY_DIM), lambda i, j: (i, 0)),
                  pl.BlockSpec((1, R_KEY_DIM), lambda i, j: (0, 0))],
        out_specs=pl.BlockSpec((tm, tn), lambda i, j: (i, j)),
        out_shape=jax.ShapeDtypeStruct((TOKENS, ncols), BF16),
        scratch_shapes=[pltpu.VMEM((tm, R_KEY_DIM), F32), pltpu.VMEM((tm, R_KEY_DIM), F32)],
        compiler_params=_params(2),
        name="proj_rope",
    )(h, wt, pos2, freq2)


def _dsa_kernel(pmin_ref, pmax_ref, pminh_ref, pmaxh_ref,
                q_ref, k_ref, vt_ref, iq_ref, ikd_ref, iwt_ref, posq_ref, posk_ref, az_ref,
                tab_ref, far_ref, o_ref,
                idx_scr, idxb_scr, off_scr, dig_scr, iqm_scr, acc_scr, m_scr):
    b = pl.program_id(0)
    qi = pl.program_id(1)
    nkb = qi + 1

    lane =lax.broadcasted_iota(jnp.int32, (QB, LANES), 1)
    for p in range(IDX_HEADS // 2):
        pair = iq_ref[:, p * LANES:(p + 1) * LANES].astype(F32)
        iqm_scr[2 * p] = jnp.where(lane < IDX_DIM, pair, 0.0).astype(BF16)
        iqm_scr[2 * p + 1] = jnp.where(lane >= IDX_DIM, pair, 0.0).astype(BF16)
    iwt = iwt_ref[...]

    rr = lax.broadcasted_iota(jnp.int32, (KB, QB), 0)
    cc = lax.broadcasted_iota(jnp.int32, (KB, QB), 1)

    def idx_scores(kb):
        s0 = pl.multiple_of(kb * KB, KB)
        kid = ikd_ref[pl.ds(s0, KB), :]
        acc = jnp.zeros((KB, QB), F32)
        for h in range(IDX_HEADS):
            sc = lax.dot_general(kid, iqm_scr[h], NT_DIMS, preferred_element_type=F32)
            acc = acc + jnp.maximum(sc, 0.0) * iwt[h:h + 1, :]
        return acc

    def idx_store(kb, acc, diagonal):
        if diagonal:
            admissible = (rr // CHUNK) <= (cc // CHUNK)
            acc = jnp.where(admissible, acc, -jnp.inf)
        idx_scr[kb] = acc
        idxb_scr[kb] = acc.astype(BF16)

    def idx_tiles(kbs, diagonal_last):
        accs = [idx_scores(kb) for kb in kbs]
        for t, kb in enumerate(kbs):
            idx_store(kb, accs[t], diagonal_last and t == len(kbs) - 1)

    def idx_pair_body(j, c):
        idx_tiles([2 * j, 2 * j + 1], False)
        return c

    lax.fori_loop(0, lax.shift_right_logical(qi, 1), idx_pair_body, 0)
    qi_odd = (qi & 1) == 1
    pl.when(qi_odd)(lambda: idx_tiles([qi - 1, qi], True))
    pl.when(jnp.logical_not(qi_odd))(lambda: idx_tiles([qi], True))

    nkb_sel = jnp.where(qi > 0, nkb, 0)
    npair_sel = lax.shift_right_logical(nkb_sel + 1, 1)

    @pl.when((nkb & 1) == 1)
    def _():
        idxb_scr[nkb] = jnp.full((KB, QB), -jnp.inf, BF16)
        dig_scr[nkb] = jnp.full((KB, QB), -300.0, BF16)

    def count_ge(plane_ref, cand, group):
        dt = plane_ref.dtype
        one = jnp.ones((), dt)
        zero = jnp.zeros((), dt)

        def body(j, acc):
            slabs = []
            for kb in (2 * j, 2 * j + 1):
                part = jnp.where(plane_ref[kb] >= cand, one, zero)
                slabs += [part[group * g:group * (g + 1)] for g in range(KB // group)]
            while len(slabs) > 1:
                slabs = [slabs[i] + slabs[i + 1] for i in range(0, len(slabs), 2)]
            return acc + slabs[0]

        acc = lax.fori_loop(0, npair_sel, body, jnp.zeros((group, QB), dt))
        return jnp.sum(acc.astype(F32), axis=0, keepdims=True)

    def bisect(n_bits, count_fn):
        def bit_body(it, ans):
            cand = ans | jnp.left_shift(jnp.int32(1), n_bits - 1 - it)
            return jnp.where(count_fn(cand) >= TOPK, cand, ans)

        return lax.fori_loop(0, jnp.where(qi > 0, n_bits, 0), bit_body,
                             jnp.zeros((1, QB), jnp.int32))

    def bf16_candidate(c16):
        s16 = c16 - 32768
        b16 = jnp.where(s16 < 0, s16 ^ 0x7FFF, s16)
        return pltpu.bitcast(b16 << 16, F32).astype(BF16)

    c16 = bisect(16, lambda c: count_ge(idxb_scr, bf16_candidate(c), 16))
    s16 = c16 - 32768
    base_bits = jnp.where(s16 < 0, s16 ^ 0x7FFF, s16) << 16
    base = pltpu.bitcast(base_bits, F32)
    expo = (base_bits >> 23) & 0xFF
    scale = pltpu.bitcast(jnp.clip(278 - expo, 1, 254) << 23, F32)
    unit = pltpu.bitcast(jnp.clip(expo - 24, 1, 254) << 23, F32)
    span = 65536.0
    radix = 512.0
    hi_shift = 192.0
    lo_shift = 256.0

    def hi_body(kb, c):
        q = jnp.clip((idx_scr[kb] - base) * scale + span, -radix, 3.0 * span + radix - 1.0)
        off_scr[kb] = q
        dig_scr[kb] = (jnp.floor(q * (1.0 / radix)) - hi_shift).astype(BF16)
        return c

    lax.fori_loop(0, nkb_sel, hi_body, 0)
    c_hi = bisect(9, lambda c: count_ge(dig_scr, (c - 192).astype(F32).astype(BF16), 16))
    c_hi = c_hi.astype(F32)

    def lo_body(kb, c):
        q = off_scr[kb]
        hi = jnp.floor(q * (1.0 / radix))
        lo = jnp.where(hi > c_hi, lo_shift,
                       jnp.where(hi == c_hi, q - radix * c_hi - lo_shift, -lo_shift - 2.0))
        dig_scr[kb] = lo.astype(BF16)
        return c

    lax.fori_loop(0, nkb_sel, lo_body, 0)
    c_lo = bisect(9, lambda c: count_ge(dig_scr, (c - 256).astype(F32).astype(BF16), 16))
    thr = base + (radix * c_hi + c_lo.astype(F32) - span) * unit
    thr = jnp.where(qi > 0, thr, jnp.finfo(F32).min)

    def count_where(pred_fn):
        def body(kb, acc):
            part = jnp.where(pred_fn(kb, idx_scr[kb]), 1.0, 0.0)
            return acc + jnp.sum(part.reshape(KB // 8, 8, QB), axis=0)
        acc = lax.fori_loop(0, nkb, body, jnp.zeros((8, QB), F32))
        return jnp.sum(acc, axis=0, keepdims=True)

    cnt_ge_thr = count_ge(dig_scr, (c_lo - 256).astype(F32).astype(BF16), 16)
    has_tie = jnp.logical_and(qi > 0, jnp.max(cnt_ge_thr) > TOPK)

    @pl.when(has_tie)
    def _():
        cnt_gt = count_where(lambda kb, kk: kk > thr)
        need = TOPK - cnt_gt

        def key_index(kb):
            return kb * KB + rr

        def jb_body(it, j0):
            cand = j0 | jnp.left_shift(jnp.int32(1), 11 - it)
            f = count_where(lambda kb, kk: jnp.logical_and(kk == thr, key_index(kb) < cand))
            return jnp.where(f < need, cand, j0)

        j0 = lax.fori_loop(0, 12, jb_body, jnp.zeros((1, QB), jnp.int32))
        jstar = j0 + 1

        def fix_body(kb, c):
            kk = idx_scr[kb]
            drop = jnp.logical_and(kk == thr, key_index(kb) >= jstar)
            idx_scr[kb] = jnp.where(drop, -jnp.inf, kk)
            return c

        lax.fori_loop(0, nkb, fix_body, 0)

    m_scr[...] = jnp.full(m_scr.shape, NEG, F32)
    acc_scr[...] = jnp.zeros(acc_scr.shape, F32)
    posq = posq_ref[...]
    pmin_q = pmin_ref[b, qi]
    half = N_BUCKETS // 2
    max_exact = half // 2

    def t5_bucket(pk, pq):
        rel = jnp.concatenate([pk] * (pq.shape[1] // LANES), axis=1) - pq
        n = jnp.abs(rel)
        nf = jnp.maximum(n, 1).astype(F32)
        large = max_exact + (jnp.log(nf / max_exact) / math.log(MAX_DISTANCE / max_exact)
                             * (half - max_exact)).astype(jnp.int32)
        large = jnp.minimum(large, half - 1)
        return jnp.where(rel > 0, half, 0) + jnp.where(n < max_exact, n, large)

    def shifted_bias(h, bucket):
        row = jnp.broadcast_to(tab_ref[h:h + 1, :], (bucket.shape[0], LANES))
        bias = jnp.concatenate(
            [jnp.take_along_axis(row, bucket[:, g * LANES:(g + 1) * LANES], axis=1)
             for g in range(bucket.shape[1] // LANES)], axis=1)
        return (bias - far_ref[h]) * LOG2E

    def att_tiles(kbs, near):
        hk, hq = KB // 2, QB // 2
        scores = {}
        for t, kb in enumerate(kbs):
            s0 = pl.multiple_of(kb * KB, KB)
            for h in range(A_HEADS):
                hs = slice(h * A_HEAD_DIM, (h + 1) * A_HEAD_DIM)
                scores[t, h] = lax.dot_general(k_ref[pl.ds(s0, KB), hs], q_ref[:, hs], NT_DIMS,
                                               preferred_element_type=F32)
        for t, kb in enumerate(kbs):
            s0 = pl.multiple_of(kb * KB, KB)
            madd = jnp.where(idx_scr[kb] >= thr, 0.0, NEG)
            if near == "all":
                bucket = t5_bucket(posk_ref[pl.ds(s0, KB), :], posq)
            elif near == "corner":
                bucket = t5_bucket(posk_ref[pl.ds(s0 + hk, hk), :], posq[:, :hq])
            probs, alphas = [], []
            for h in range(A_HEADS):
                if near == "all":
                    s = scores[t, h] + (shifted_bias(h, bucket) + madd)
                elif near == "corner":
                    raw = scores[t, h]
                    low_left = raw[hk:, :hq] + (shifted_bias(h, bucket) + madd[hk:, :hq])
                    low = jnp.concatenate([low_left, raw[hk:, hq:] + madd[hk:, hq:]], axis=1)
                    s = jnp.concatenate([raw[:hk] + madd[:hk], low], axis=0)
                else:
                    s = scores[t, h] + madd
                m_old = m_scr[h]
                m_new = jnp.maximum(m_old, jnp.max(s, axis=0, keepdims=True))
                alphas.append(jnp.exp2(m_old - m_new))
                probs.append(jnp.exp2(s - m_new).astype(BF16))
                m_scr[h] = m_new
            for h in range(A_HEADS):
                pv = jnp.dot(vt_ref[kb, h * VT_ROWS:(h + 1) * VT_ROWS, :], probs[h],
                             preferred_element_type=F32)
                acc_scr[h] = alphas[h] * acc_scr[h] + pv

    def is_far(kb):
        return (pmin_q - pmax_ref[b, kb]) >= MAX_DISTANCE

    def att_single(kb, c):
        far = is_far(kb)
        early_far = (pmin_q - pmaxh_ref[b, 2 * kb]) >= MAX_DISTANCE
        late_far = (pminh_ref[b, 2 * qi + 1] - pmaxh_ref[b, 2 * kb + 1]) >= MAX_DISTANCE
        corner = jnp.logical_and(jnp.logical_not(far), jnp.logical_and(early_far, late_far))
        rest = jnp.logical_not(jnp.logical_or(far, corner))
        pl.when(far)(lambda: att_tiles([kb], False))
        pl.when(corner)(lambda: att_tiles([kb], "corner"))
        pl.when(rest)(lambda: att_tiles([kb], "all"))
        return c

    def att_pair_body(j, c):
        kb0 = 2 * j
        kb1 = kb0 + 1
        both_far = jnp.logical_and(kb1 < nkb, jnp.logical_and(is_far(kb0), is_far(kb1)))
        pl.when(both_far)(lambda: att_tiles([kb0, kb1], False))

        @pl.when(jnp.logical_not(both_far))
        def _():
            lax.fori_loop(kb0, jnp.minimum(kb1 + 1, nkb), att_single, 0)

        return c

    lax.fori_loop(0, lax.shift_right_logical(nkb + 1, 1), att_pair_body, 0)

    for h in range(A_HEADS):
        hs = slice(h * A_HEAD_DIM, (h + 1) * A_HEAD_DIM)
        z = az_ref[:, hs].astype(F32)
        out_t = acc_scr[h, :A_HEAD_DIM] / acc_scr[h, A_HEAD_DIM:A_HEAD_DIM + 1]
        o_ref[:, hs] = (out_t.T * (z * jax.nn.sigmoid(z))).astype(o_ref.dtype)


def _dsa(pa, vt, ikd, iwt, positions, rel_bias):
    pa3 = pa.reshape(BATCH, SEQ, PA_WIDTH)
    ikd3 = ikd.reshape(BATCH, SEQ, LANES)
    posq = positions.reshape(BATCH, 1, SEQ)
    posk = jnp.broadcast_to(positions[:, :, None], (BATCH, SEQ, LANES))
    phalf = positions.reshape(BATCH, 2 * NKB, KB // 2)
    pminh = jnp.min(phalf, axis=-1)
    pmaxh = jnp.max(phalf, axis=-1)
    pmin = jnp.min(pminh.reshape(BATCH, NKB, 2), axis=-1)
    pmax = jnp.max(pmaxh.reshape(BATCH, NKB, 2), axis=-1)
    tab = jnp.zeros((A_HEADS, LANES), F32).at[:, :N_BUCKETS].set(rel_bias.astype(F32).T)
    far = rel_bias[N_BUCKETS // 2 - 1, :].astype(F32)

    grid_spec = pltpu.PrefetchScalarGridSpec(
        num_scalar_prefetch=4,
        grid=(BATCH, NQB),
        in_specs=[
            pl.BlockSpec((None, QB, A_WIDTH), lambda b, i, *_: (b, i, PA_Q)),
            pl.BlockSpec((None, SEQ, A_WIDTH), lambda b, i, *_: (b, 0, PA_K),
                         pipeline_mode=pl.Buffered(1)),
            pl.BlockSpec((None, NKB, A_HEADS * VT_ROWS, KB), lambda b, i, *_: (b, 0, 0, 0),
                         pipeline_mode=pl.Buffered(1)),
            pl.BlockSpec((None, QB, A_WIDTH), lambda b, i, *_: (b, i, PA_IQ)),
            pl.BlockSpec((None, SEQ, LANES), lambda b, i, *_: (b, 0, 0)),
            pl.BlockSpec((IDX_HEADS, QB), lambda b, i, *_: (0, b * NQB + i)),
            pl.BlockSpec((None, 1, QB), lambda b, i, *_: (b, 0, i)),
            pl.BlockSpec((None, SEQ, LANES), lambda b, i, *_: (b, 0, 0)),
            pl.BlockSpec((None, QB, A_WIDTH), lambda b, i, *_: (b, i, PA_AZ)),
            pl.BlockSpec((A_HEADS, LANES), lambda b, i, *_: (0, 0)),
            pl.BlockSpec(memory_space=pltpu.SMEM),
        ],
        out_specs=pl.BlockSpec((None, QB, A_WIDTH), lambda b, i, *_: (b, i, 0)),
        scratch_shapes=[
            pltpu.VMEM((NKB, KB, QB), F32),
            pltpu.VMEM((NKB, KB, QB), BF16),
            pltpu.VMEM((NKB, KB, QB), F32),
            pltpu.VMEM((NKB, KB, QB), BF16),
            pltpu.VMEM((IDX_HEADS, QB, LANES), BF16),
            pltpu.VMEM((A_HEADS, VT_ROWS, QB), F32),
            pltpu.VMEM((A_HEADS, 1, QB), F32),
        ],
    )
    out = pl.pallas_call(
        _dsa_kernel,
        grid_spec=grid_spec,
        out_shape=jax.ShapeDtypeStruct((BATCH, SEQ, A_WIDTH), BF16),
        compiler_params=_params(2),
        name="dsa",
    )(pmin, pmax, pminh, pmaxh, pa3, pa3, vt, pa3, ikd3, iwt, posq, posk, pa3, tab, far)
    return out.reshape(TOKENS, A_WIDTH)


def _ret_kernel(cdec_ref, q_ref, k_ref, v_ref, z_ref, gain_ref, dec_ref, te_ref, fs_ref, o_ref,
                state_scr):
    @pl.when(pl.program_id(1) == 0)
    def _():
        state_scr[...] = jnp.zeros(state_scr.shape, F32)

    heads = [slice(h * R_KEY_DIM, (h + 1) * R_KEY_DIM) for h in range(R_HEADS)]

    group = 4

    def chunk_group_body(g, carry):
        rows = [pl.ds(pl.multiple_of((g * group + t) * CHUNK, CHUNK), CHUNK) for t in range(group)]
        scores, kv = [], []
        for t in range(group):
            for h, hs in enumerate(heads):
                k = k_ref[rows[t], hs]
                scores.append(lax.dot_general(q_ref[rows[t], hs], k, NT_DIMS,
                                              preferred_element_type=F32))
                ke = (k.astype(F32) * te_ref[h]).T.astype(BF16)
                kv.append(jnp.dot(ke, v_ref[rows[t], hs], preferred_element_type=F32))
        cross = []
        states = [state_scr[h] for h in range(R_HEADS)]
        for t in range(group):
            for h, hs in enumerate(heads):
                qs = (q_ref[rows[t], hs].astype(F32) * fs_ref[h]).astype(BF16)
                cross.append(jnp.dot(qs, states[h].astype(BF16), preferred_element_type=F32))
                states[h] = states[h] * cdec_ref[h] + kv[t * R_HEADS + h]
        for h in range(R_HEADS):
            state_scr[h] = states[h]
        for t in range(group):
            for h, hs in enumerate(heads):
                i = t * R_HEADS + h
                sc = (scores[i] * dec_ref[h]).astype(BF16)
                y = jnp.dot(sc, v_ref[rows[t], hs], preferred_element_type=F32) + cross[i]
                mu = jnp.mean(y, axis=-1, keepdims=True)
                var = jnp.mean((y - mu) ** 2, axis=-1, keepdims=True)
                yn = (y - mu) * lax.rsqrt(var + EPS) * gain_ref[:, hs]
                z = z_ref[rows[t], hs].astype(F32)
                o_ref[rows[t], hs] = (yn * (z * jax.nn.sigmoid(z))).astype(o_ref.dtype)
        return carry

    lax.fori_loop(0, RB // (CHUNK * group), chunk_group_body, 0)


def _retention(rqk, pb, gn_gain):
    log_g = jnp.log(1.0 - 2.0 ** (-5.0 - jnp.arange(R_HEADS, dtype=F32)))
    pos = jnp.arange(CHUNK, dtype=F32)
    dist = jnp.abs(pos[:, None] - pos[None, :])
    intra_decay = jnp.exp(log_g[:, None, None] * dist)
    to_end = jnp.exp(log_g[:, None] * (CHUNK - 1.0 - pos)[None, :])
    from_start = jnp.exp(log_g[:, None] * (pos + 1.0)[None, :])
    chunk_decay = jnp.exp(log_g * CHUNK)
    te = jnp.broadcast_to(to_end[:, :, None], (R_HEADS, CHUNK, R_KEY_DIM))
    fs = jnp.broadcast_to(from_start[:, :, None], (R_HEADS, CHUNK, R_KEY_DIM))

    rqk3 = rqk.reshape(BATCH, SEQ, 2 * R_WIDTH)
    pb3 = pb.reshape(BATCH, SEQ, PB_WIDTH)
    out = pl.pallas_call(
        _ret_kernel,
        grid=(BATCH, SEQ // RB),
        in_specs=[
            pl.BlockSpec(memory_space=pltpu.SMEM),
            pl.BlockSpec((None, RB, R_WIDTH), lambda b, i: (b, i, 0)),
            pl.BlockSpec((None, RB, R_WIDTH), lambda b, i: (b, i, 1)),
            pl.BlockSpec((None, RB, R_WIDTH), lambda b, i: (b, i, PB_RV)),
            pl.BlockSpec((None, RB, R_WIDTH), lambda b, i: (b, i, PB_RZ)),
            pl.BlockSpec((1, R_WIDTH), lambda b, i: (0, 0)),
            pl.BlockSpec((R_HEADS, CHUNK, CHUNK), lambda b, i: (0, 0, 0)),
            pl.BlockSpec((R_HEADS, CHUNK, R_KEY_DIM), lambda b, i: (0, 0, 0)),
            pl.BlockSpec((R_HEADS, CHUNK, R_KEY_DIM), lambda b, i: (0, 0, 0)),
        ],
        out_specs=pl.BlockSpec((None, RB, R_WIDTH), lambda b, i: (b, i, 0)),
        out_shape=jax.ShapeDtypeStruct((BATCH, SEQ, R_WIDTH), BF16),
        scratch_shapes=[pltpu.VMEM((R_HEADS, R_KEY_DIM, R_VAL_DIM), F32)],
        compiler_params=_params(2),
        name="retention",
    )(chunk_decay, rqk3, rqk3, pb3, pb3, gn_gain.reshape(1, R_WIDTH), intra_decay, te, fs)
    return out.reshape(TOKENS, R_WIDTH)


def _out_kernel(a_ref, b_ref, ga_ref, gb_ref, x_ref, p_ref, wa_ref, wb_ref, wo_ref, wp_ref, wg_ref,
                fg_ref, o_ref):
    ta = jnp.dot(a_ref[...], wa_ref[...], preferred_element_type=F32)
    tb = jnp.dot(b_ref[...], wb_ref[...], preferred_element_type=F32)
    merged = (jax.nn.sigmoid(ga_ref[...].astype(F32)) * ta
              + jax.nn.sigmoid(gb_ref[...].astype(F32)) * tb)
    r = x_ref[...] + jnp.dot(merged.astype(BF16), wo_ref[...], preferred_element_type=F32)
    u = jnp.dot(p_ref[...].astype(BF16), wp_ref[...], preferred_element_type=F32)
    g = jnp.dot(r.astype(BF16), wg_ref[...], preferred_element_type=F32)
    y = r + u * jax.nn.sigmoid(g)
    ms = jnp.mean(y * y, axis=-1, keepdims=True)
    o_ref[...] = y * lax.rsqrt(ms + EPS) * fg_ref[...]


def _output(a_out, b_out, pb, x2d, p2d, wa, wb, wo, wp, wg, final_gain, tm=256):
    def resident(shape):
        return pl.BlockSpec(shape, lambda i: (0, 0), pipeline_mode=pl.Buffered(1))

    return pl.pallas_call(
        _out_kernel,
        grid=(TOKENS // tm,),
        in_specs=[pl.BlockSpec((tm, A_WIDTH), lambda i: (i, 0)),
                  pl.BlockSpec((tm, R_WIDTH), lambda i: (i, 0)),
                  pl.BlockSpec((tm, D_MODEL), lambda i: (i, PB_GA * R_WIDTH // D_MODEL)),
                  pl.BlockSpec((tm, D_MODEL), lambda i: (i, PB_GB * R_WIDTH // D_MODEL)),
                  pl.BlockSpec((tm, D_MODEL), lambda i: (i, 0)),
                  pl.BlockSpec((tm, PLE_DIM), lambda i: (i, 0)),
                  resident((A_WIDTH, D_MODEL)),
                  resident((R_WIDTH, D_MODEL)),
                  resident((D_MODEL, D_MODEL)),
                  resident((PLE_DIM, D_MODEL)),
                  resident((D_MODEL, D_MODEL)),
                  pl.BlockSpec((1, D_MODEL), lambda i: (0, 0))],
        out_specs=pl.BlockSpec((tm, D_MODEL), lambda i: (i, 0)),
        out_shape=jax.ShapeDtypeStruct((TOKENS, D_MODEL), F32),
        compiler_params=_params(1),
        name="output",
    )(a_out, b_out, pb, pb, x2d, p2d, wa, wb, wo, wp, wg, final_gain.reshape(1, D_MODEL))


def kernel(x, p, positions, w_in, norm_gain, w_a_out, w_b_out, w_o, ret_gn_gain, w_ple, w_ple_gate,
           rel_bias, final_gain):
    assert x.shape == (BATCH, SEQ, D_MODEL) and w_in.shape == (1, D_MODEL, IN_WIDTH)
    x2d = x.reshape(TOKENS, D_MODEL)
    p2d = p[0].reshape(TOKENS, PLE_DIM)
    wt = jnp.swapaxes(w_in[0], 0, 1)
    col = jnp.arange(PA_WIDTH)
    pa_scale = jnp.where(col < A_WIDTH, A_HEAD_DIM ** -0.5 * LOG2E,
                         jnp.where(col < PA_IQ * A_WIDTH, 1.0, IDX_DIM ** -0.5))
    pa_scale = pa_scale.astype(F32).reshape(1, PA_WIDTH)

    h = _rmsnorm(x2d, norm_gain[0])
    pa = _proj(h, wt, "proj_att", 0, PA_WIDTH, scale=pa_scale,
               skip=(2 * A_WIDTH, 3 * A_WIDTH))
    vt = _proj_vt(h, wt, 2 * A_WIDTH)
    pb = _proj(h, wt, "proj_ret", COL_RV, PB_WIDTH)
    rqk = _proj_rope(h, wt, positions)
    ikd, iwt = _proj_idx(h, wt)

    a_out = _dsa(pa, vt, ikd, iwt, positions, rel_bias)
    b_out = _retention(rqk, pb, ret_gn_gain[0])

    out = _output(a_out, b_out, pb, x2d, p2d, w_a_out[0].astype(BF16), w_b_out[0].astype(BF16),
                  w_o[0].astype(BF16), w_ple[0].astype(BF16), w_ple_gate[0].astype(BF16), final_gain)
    return out.reshape(BATCH, SEQ, D_MODEL)
```

```python
import functools
import math

import jax
import jax.numpy as jnp
from jax import lax
from jax.experimental import pallas as pl
from jax.experimental.pallas import tpu as pltpu

D_MODEL = 2048
BATCH = 4
SEQ = 4096
TOKENS = BATCH * SEQ
CHUNK = 64
PLE_DIM = 256
EPS = 1e-6
A_HEADS = 8
A_HEAD_DIM = 128
A_WIDTH = A_HEADS * A_HEAD_DIM
IDX_HEADS = 16
IDX_DIM = 64
TOPK = min(256, SEQ // 4)
R_HEADS = 8
R_KEY_DIM = 128
R_VAL_DIM = 128
R_WIDTH = R_HEADS * R_VAL_DIM
ROPE_BASE = 10000.0
N_BUCKETS = 32
MAX_DISTANCE = 128

COL_IQ = 4 * A_WIDTH
COL_IK = COL_IQ + IDX_HEADS * IDX_DIM
COL_IW = COL_IK + IDX_DIM
COL_RQ = COL_IW + IDX_HEADS
COL_RV = COL_RQ + 2 * R_WIDTH
IN_WIDTH = COL_RV + 2 * R_WIDTH + 2 * D_MODEL

PA_Q, PA_K, PA_AZ, PA_IQ = 0, 1, 2, 3
PA_WIDTH = 3 * A_WIDTH + IDX_HEADS * IDX_DIM
PB_RV, PB_RZ, PB_GA, PB_GB = 0, 1, 2, 4
PB_WIDTH = 2 * R_WIDTH + 2 * D_MODEL

LANES = 128
QB = 256
KB = 256
NQB = SEQ // QB
NKB = SEQ // KB
VT_ROWS = A_HEAD_DIM + 16
RB = 512
NEG = -1e30
LOG2E = math.log2(math.e)
VMEM_LIMIT = 56 * 1024 * 1024

F32 = jnp.float32
BF16 = jnp.bfloat16
NT_DIMS = (((1,), (1,)), ((), ()))


def _params(n_axes):
    return pltpu.CompilerParams(dimension_semantics=("arbitrary",) * n_axes,
                                vmem_limit_bytes=VMEM_LIMIT)


def _rmsnorm_kernel(x_ref, g_ref, o_ref):
    x = x_ref[...]
    ms = jnp.mean(x * x, axis=-1, keepdims=True)
    o_ref[...] = (x * lax.rsqrt(ms + EPS) * g_ref[...]).astype(o_ref.dtype)


def _rmsnorm(x2d, gain, tm=512):
    return pl.pallas_call(
        _rmsnorm_kernel,
        grid=(TOKENS // tm,),
        in_specs=[pl.BlockSpec((tm, D_MODEL), lambda i: (i, 0)),
                  pl.BlockSpec((1, D_MODEL), lambda i: (0, 0))],
        out_specs=pl.BlockSpec((tm, D_MODEL), lambda i: (i, 0)),
        out_shape=jax.ShapeDtypeStruct((TOKENS, D_MODEL), BF16),
        compiler_params=_params(1),
        name="rmsnorm",
    )(x2d, gain.reshape(1, D_MODEL))


def _wt_block(rows, row_of):
    assert rows % 8 == 0
    return pl.BlockSpec((pl.Element(rows), pl.Element(D_MODEL)),
                        lambda *g: (pl.multiple_of(row_of(*g), 8), 0))


def _proj_kernel(h_ref, wt_ref, *rest, scale):
    acc = lax.dot_general(h_ref[...], wt_ref[...].astype(BF16), NT_DIMS,
                          preferred_element_type=F32)
    if scale == "row":
        acc = acc * rest[0][...]
    elif scale is not None:
        acc = acc * scale
    rest[-1][...] = acc.astype(rest[-1].dtype)


def _proj(h, wt, name, row0, ncols, scale=None, skip=None, tm=2048, tn=512):
    def row_of(i, j):
        if skip is not None:
            j = jnp.where(j < skip[0] // tn, j, j + (skip[1] - skip[0]) // tn)
        return row0 + j * tn

    in_specs = [pl.BlockSpec((tm, D_MODEL), lambda i, j: (i, 0)), _wt_block(tn, row_of)]
    args = [h, wt]
    if scale is not None and not isinstance(scale, float):
        in_specs.append(pl.BlockSpec((1, tn), lambda i, j: (0, j)))
        args.append(scale)
        scale = "row"
    return pl.pallas_call(
        functools.partial(_proj_kernel, scale=scale),
        grid=(TOKENS // tm, ncols // tn),
        in_specs=in_specs,
        out_specs=pl.BlockSpec((tm, tn), lambda i, j: (i, j)),
        out_shape=jax.ShapeDtypeStruct((TOKENS, ncols), BF16),
        compiler_params=_params(2),
        name=name,
    )(*args)


def _proj_vt_kernel(h_ref, wt_ref, o_ref):
    acc_t = lax.dot_general(wt_ref[...].astype(BF16), h_ref[...], NT_DIMS,
                            preferred_element_type=F32)
    ones = jnp.ones((VT_ROWS - A_HEAD_DIM, KB), o_ref.dtype)
    for t in range(o_ref.shape[0]):
        for hh in range(acc_t.shape[0] // A_HEAD_DIM):
            r0 = hh * VT_ROWS
            o_ref[t, r0:r0 + A_HEAD_DIM] = acc_t[hh * A_HEAD_DIM:(hh + 1) * A_HEAD_DIM,
                                                 t * KB:(t + 1) * KB].astype(o_ref.dtype)
            o_ref[t, r0 + A_HEAD_DIM:r0 + VT_ROWS] = ones


def _proj_vt(h, wt, row0, tm=2048, tn=512):
    per_batch = SEQ // tm
    rows = tn // A_HEAD_DIM * VT_ROWS
    return pl.pallas_call(
        _proj_vt_kernel,
        grid=(TOKENS // tm, A_WIDTH // tn),
        in_specs=[pl.BlockSpec((tm, D_MODEL), lambda i, j: (i, 0)),
                  _wt_block(tn, lambda i, j: row0 + j * tn)],
        out_specs=pl.BlockSpec((None, tm // KB, rows, KB),
                               lambda i, j: (i // per_batch, i % per_batch, j, 0)),
        out_shape=jax.ShapeDtypeStruct((BATCH, NKB, A_HEADS * VT_ROWS, KB), BF16),
        compiler_params=_params(2),
        name="proj_vt",
    )(h, wt)


def _proj_idx_kernel(h_ref, wik_ref, wiw_ref, ik_ref, iwt_ref):
    h = h_ref[...]
    wik = wik_ref[...].astype(BF16)
    ik = lax.dot_general(h, jnp.concatenate([wik, wik], axis=0), NT_DIMS,
                         preferred_element_type=F32)
    mu = jnp.mean(ik, axis=-1, keepdims=True)
    var = jnp.mean((ik - mu) ** 2, axis=-1, keepdims=True)
    ik_ref[...] = ((ik - mu) * lax.rsqrt(var + EPS)).astype(ik_ref.dtype)
    iwt = lax.dot_general(wiw_ref[...].astype(BF16), h, NT_DIMS, preferred_element_type=F32)
    iwt_ref[...] = iwt * (IDX_HEADS ** -0.5)


def _proj_idx(h, wt, tm=2048):
    return pl.pallas_call(
        _proj_idx_kernel,
        grid=(TOKENS // tm,),
        in_specs=[pl.BlockSpec((tm, D_MODEL), lambda i: (i, 0)),
                  _wt_block(IDX_DIM, lambda i: COL_IK),
                  _wt_block(IDX_HEADS, lambda i: COL_IW)],
        out_specs=[pl.BlockSpec((tm, LANES), lambda i: (i, 0)),
                   pl.BlockSpec((IDX_HEADS, tm), lambda i: (0, i))],
        out_shape=[jax.ShapeDtypeStruct((TOKENS, LANES), BF16),
                   jax.ShapeDtypeStruct((IDX_HEADS, TOKENS), F32)],
        compiler_params=_params(1),
        name="proj_idx",
    )(h, wt, wt)


def _proj_rope_kernel(h_ref, w_ref, pos_ref, freq_ref, o_ref, cos_scr, sin_scr, *,
                      tn, k_tile0, k_scale):
    @pl.when(pl.program_id(1) == 0)
    def _():
        hm = pos_ref.shape[0]
        ang = pos_ref[...].astype(F32) * freq_ref[...]
        c = jnp.cos(ang)
        s = jnp.sin(ang)
        cr = pltpu.roll(c, LANES // 2, 1)
        sr = pltpu.roll(s, LANES // 2, 1)
        low = lax.broadcasted_iota(jnp.int32, c.shape, 1) < LANES // 2
        cos_scr[:hm] = jnp.where(low, c, cr)
        cos_scr[hm:] = jnp.where(low, cr, c)
        sin_scr[:hm] = jnp.where(low, -s, sr)
        sin_scr[hm:] = jnp.where(low, -sr, s)

    acc = lax.dot_general(h_ref[...], w_ref[...].astype(BF16), NT_DIMS,
                          preferred_element_type=F32)
    cos = cos_scr[...]
    sin = sin_scr[...]
    scale = jnp.where(pl.program_id(1) >= k_tile0, k_scale, 1.0).astype(F32)
    for g in range(tn // LANES):
        xg = acc[:, g * LANES:(g + 1) * LANES]
        rot = xg * cos + pltpu.roll(xg, LANES // 2, 1) * sin
        o_ref[:, g * LANES:(g + 1) * LANES] = (rot * scale).astype(o_ref.dtype)


def _proj_rope(h, wt, positions, tm=2048, tn=512):
    ncols = 2 * R_WIDTH
    half = R_KEY_DIM // 2
    inv_freq = ROPE_BASE ** (-jnp.arange(half, dtype=F32) / half)
    freq2 = jnp.concatenate([inv_freq, inv_freq]).reshape(1, R_KEY_DIM)
    pos_t = positions.reshape(TOKENS // tm, 2, tm // 2, 1)
    pos2 = jnp.concatenate([jnp.broadcast_to(pos_t[:, 0], (TOKENS // tm, tm // 2, half)),
                            jnp.broadcast_to(pos_t[:, 1], (TOKENS // tm, tm // 2, half))], axis=-1)
    pos2 = pos2.reshape(TOKENS // 2, R_KEY_DIM)
    kern = functools.partial(_proj_rope_kernel, tn=tn, k_tile0=R_WIDTH // tn,
                             k_scale=R_KEY_DIM ** -0.5)
    return pl.pallas_call(
        kern,
        grid=(TOKENS // tm, ncols // tn),
        in_specs=[pl.BlockSpec((tm, D_MODEL), lambda i, j: (i, 0)),
                  _wt_block(tn, lambda i, j: COL_RQ + j * tn),
                  pl.BlockSpec((tm // 2, R_KEY_DIM), lambda i, j: (i, 0)),
                  pl.BlockSpec((1, R_KEY_DIM), lambda i, j: (0, 0))],
        out_specs=pl.BlockSpec((tm, tn), lambda i, j: (i, j)),
        out_shape=jax.ShapeDtypeStruct((TOKENS, ncols), BF16),
        scratch_shapes=[pltpu.VMEM((tm, R_KEY_DIM), F32), pltpu.VMEM((tm, R_KEY_DIM), F32)],
        compiler_params=_params(2),
        name="proj_rope",
    )(h, wt, pos2, freq2)


def _dsa_kernel(pmin_ref, pmax_ref, pminh_ref, pmaxh_ref,
                q_ref, k_ref, vt_ref, iq_ref, ikd_ref, iwt_ref, posq_ref, posk_ref, az_ref,
                tab_ref, far_ref, o_ref,
                idx_scr, idxb_scr, off_scr, dig_scr, iqm_scr, acc_scr, m_scr):
    b = pl.program_id(0)
    qi = pl.program_id(1)
    nkb = qi + 1

    lane =lax.broadcasted_iota(jnp.int32, (QB, LANES), 1)
    for p in range(IDX_HEADS // 2):
        pair = iq_ref[:, p * LANES:(p + 1) * LANES].astype(F32)
        iqm_scr[2 * p] = jnp.where(lane < IDX_DIM, pair, 0.0).astype(BF16)
        iqm_scr[2 * p + 1] = jnp.where(lane >= IDX_DIM, pair, 0.0).astype(BF16)
    iwt = iwt_ref[...]

    rr = lax.broadcasted_iota(jnp.int32, (KB, QB), 0)
    cc = lax.broadcasted_iota(jnp.int32, (KB, QB), 1)

    def idx_scores(kb):
        s0 = pl.multiple_of(kb * KB, KB)
        kid = ikd_ref[pl.ds(s0, KB), :]
        acc = jnp.zeros((KB, QB), F32)
        for h in range(IDX_HEADS):
            sc = lax.dot_general(kid, iqm_scr[h], NT_DIMS, preferred_element_type=F32)
            acc = acc + jnp.maximum(sc, 0.0) * iwt[h:h + 1, :]
        return acc

    def idx_store(kb, acc, diagonal):
        if diagonal:
            admissible = (rr // CHUNK) <= (cc // CHUNK)
            acc = jnp.where(admissible, acc, -jnp.inf)
        idx_scr[kb] = acc
        idxb_scr[kb] = acc.astype(BF16)

    def idx_tiles(kbs, diagonal_last):
        accs = [idx_scores(kb) for kb in kbs]
        for t, kb in enumerate(kbs):
            idx_store(kb, accs[t], diagonal_last and t == len(kbs) - 1)

    def idx_pair_body(j, c):
        idx_tiles([2 * j, 2 * j + 1], False)
        return c

    lax.fori_loop(0, lax.shift_right_logical(qi, 1), idx_pair_body, 0)
    qi_odd = (qi & 1) == 1
    pl.when(qi_odd)(lambda: idx_tiles([qi - 1, qi], True))
    pl.when(jnp.logical_not(qi_odd))(lambda: idx_tiles([qi], True))

    nkb_sel = jnp.where(qi > 0, nkb, 0)
    npair_sel = lax.shift_right_logical(nkb_sel + 1, 1)

    @pl.when((nkb & 1) == 1)
    def _():
        idxb_scr[nkb] = jnp.full((KB, QB), -jnp.inf, BF16)
        dig_scr[nkb] = jnp.full((KB, QB), -300.0, BF16)

    def count_ge(plane_ref, cand, group):
        dt = plane_ref.dtype
        one = jnp.ones((), dt)
        zero = jnp.zeros((), dt)

        def body(j, acc):
            slabs = []
            for kb in (2 * j, 2 * j + 1):
                part = jnp.where(plane_ref[kb] >= cand, one, zero)
                slabs += [part[group * g:group * (g + 1)] for g in range(KB // group)]
            while len(slabs) > 1:
                slabs = [slabs[i] + slabs[i + 1] for i in range(0, len(slabs), 2)]
            return acc + slabs[0]

        acc = lax.fori_loop(0, npair_sel, body, jnp.zeros((group, QB), dt))
        return jnp.sum(acc.astype(F32), axis=0, keepdims=True)

    def bisect(n_bits, count_fn):
        def bit_body(it, ans):
            cand = ans | jnp.left_shift(jnp.int32(1), n_bits - 1 - it)
            return jnp.where(count_fn(cand) >= TOPK, cand, ans)

        return lax.fori_loop(0, jnp.where(qi > 0, n_bits, 0), bit_body,
                             jnp.zeros((1, QB), jnp.int32))

    def bf16_candidate(c16):
        s16 = c16 - 32768
        b16 = jnp.where(s16 < 0, s16 ^ 0x7FFF, s16)
        return pltpu.bitcast(b16 << 16, F32).astype(BF16)

    c16 = bisect(16, lambda c: count_ge(idxb_scr, bf16_candidate(c), 16))
    s16 = c16 - 32768
    base_bits = jnp.where(s16 < 0, s16 ^ 0x7FFF, s16) << 16
    base = pltpu.bitcast(base_bits, F32)
    expo = (base_bits >> 23) & 0xFF
    scale = pltpu.bitcast(jnp.clip(278 - expo, 1, 254) << 23, F32)
    unit = pltpu.bitcast(jnp.clip(expo - 24, 1, 254) << 23, F32)
    span = 65536.0
    radix = 512.0
    hi_shift = 192.0
    lo_shift = 256.0

    def hi_body(kb, c):
        q = jnp.clip((idx_scr[kb] - base) * scale + span, -radix, 3.0 * span + radix - 1.0)
        off_scr[kb] = q
        dig_scr[kb] = (jnp.floor(q * (1.0 / radix)) - hi_shift).astype(BF16)
        return c

    lax.fori_loop(0, nkb_sel, hi_body, 0)
    c_hi = bisect(9, lambda c: count_ge(dig_scr, (c - 192).astype(F32).astype(BF16), 16))
    c_hi = c_hi.astype(F32)

    def lo_body(kb, c):
        q = off_scr[kb]
        hi = jnp.floor(q * (1.0 / radix))
        lo = jnp.where(hi > c_hi, lo_shift,
                       jnp.where(hi == c_hi, q - radix * c_hi - lo_shift, -lo_shift - 2.0))
        dig_scr[kb] = lo.astype(BF16)
        return c

    lax.fori_loop(0, nkb_sel, lo_body, 0)
    c_lo = bisect(9, lambda c: count_ge(dig_scr, (c - 256).astype(F32).astype(BF16), 16))
    thr = base + (radix * c_hi + c_lo.astype(F32) - span) * unit
    thr = jnp.where(qi > 0, thr, jnp.finfo(F32).min)

    def count_where(pred_fn):
        def body(kb, acc):
            part = jnp.where(pred_fn(kb, idx_scr[kb]), 1.0, 0.0)
            return acc + jnp.sum(part.reshape(KB // 8, 8, QB), axis=0)
        acc = lax.fori_loop(0, nkb, body, jnp.zeros((8, QB), F32))
        return jnp.sum(acc, axis=0, keepdims=True)

    cnt_ge_thr = count_ge(dig_scr, (c_lo - 256).astype(F32).astype(BF16), 16)
    has_tie = jnp.logical_and(qi > 0, jnp.max(cnt_ge_thr) > TOPK)

    @pl.when(has_tie)
    def _():
        cnt_gt = count_where(lambda kb, kk: kk > thr)
        need = TOPK - cnt_gt

        def key_index(kb):
            return kb * KB + rr

        def jb_body(it, j0):
            cand = j0 | jnp.left_shift(jnp.int32(1), 11 - it)
            f = count_where(lambda kb, kk: jnp.logical_and(kk == thr, key_index(kb) < cand))
            return jnp.where(f < need, cand, j0)

        j0 = lax.fori_loop(0, 12, jb_body, jnp.zeros((1, QB), jnp.int32))
        jstar = j0 + 1

        def fix_body(kb, c):
            kk = idx_scr[kb]
            drop = jnp.logical_and(kk == thr, key_index(kb) >= jstar)
            idx_scr[kb] = jnp.where(drop, -jnp.inf, kk)
            return c

        lax.fori_loop(0, nkb, fix_body, 0)

    m_scr[...] = jnp.full(m_scr.shape, NEG, F32)
    acc_scr[...] = jnp.zeros(acc_scr.shape, F32)
    posq = posq_ref[...]
    pmin_q = pmin_ref[b, qi]
    half = N_BUCKETS // 2
    max_exact = half // 2

    def t5_bucket(pk, pq):
        rel = jnp.concatenate([pk] * (pq.shape[1] // LANES), axis=1) - pq
        n = jnp.abs(rel)
        nf = jnp.maximum(n, 1).astype(F32)
        large = max_exact + (jnp.log(nf / max_exact) / math.log(MAX_DISTANCE / max_exact)
                             * (half - max_exact)).astype(jnp.int32)
        large = jnp.minimum(large, half - 1)
        return jnp.where(rel > 0, half, 0) + jnp.where(n < max_exact, n, large)

    def shifted_bias(h, bucket):
        row = jnp.broadcast_to(tab_ref[h:h + 1, :], (bucket.shape[0], LANES))
        bias = jnp.concatenate(
            [jnp.take_along_axis(row, bucket[:, g * LANES:(g + 1) * LANES], axis=1)
             for g in range(bucket.shape[1] // LANES)], axis=1)
        return (bias - far_ref[h]) * LOG2E

    def att_tiles(kbs, near):
        hk, hq = KB // 2, QB // 2
        scores = {}
        for t, kb in enumerate(kbs):
            s0 = pl.multiple_of(kb * KB, KB)
            for h in range(A_HEADS):
                hs = slice(h * A_HEAD_DIM, (h + 1) * A_HEAD_DIM)
                scores[t, h] = lax.dot_general(k_ref[pl.ds(s0, KB), hs], q_ref[:, hs], NT_DIMS,
                                               preferred_element_type=F32)
        for t, kb in enumerate(kbs):
            s0 = pl.multiple_of(kb * KB, KB)
            madd = jnp.where(idx_scr[kb] >= thr, 0.0, NEG)
            if near == "all":
                bucket = t5_bucket(posk_ref[pl.ds(s0, KB), :], posq)
            elif near == "corner":
                bucket = t5_bucket(posk_ref[pl.ds(s0 + hk, hk), :], posq[:, :hq])
            elif near == "diagonal":
                bucket = t5_bucket(posk_ref[pl.ds(s0, hk), :], posq)
                bucket_lr = t5_bucket(posk_ref[pl.ds(s0 + hk, hk), :], posq[:, hq:])
            probs, alphas = [], []
            for h in range(A_HEADS):
                if near == "all":
                    s = scores[t, h] + (shifted_bias(h, bucket) + madd)
                elif near == "corner":
                    raw = scores[t, h]
                    low_left = raw[hk:, :hq] + (shifted_bias(h, bucket) + madd[hk:, :hq])
                    low = jnp.concatenate([low_left, raw[hk:, hq:] + madd[hk:, hq:]], axis=1)
                    s = jnp.concatenate([raw[:hk] + madd[:hk], low], axis=0)
                elif near == "diagonal":
                    raw = scores[t, h]
                    top = raw[:hk] + (shifted_bias(h, bucket) + madd[:hk])
                    low_right = raw[hk:, hq:] + (shifted_bias(h, bucket_lr) + madd[hk:, hq:])
                    low = jnp.concatenate([raw[hk:, :hq] + madd[hk:, :hq], low_right], axis=1)
                    s = jnp.concatenate([top, low], axis=0)
                else:
                    s = scores[t, h] + madd
                m_old = m_scr[h]
                m_new = jnp.maximum(m_old, jnp.max(s, axis=0, keepdims=True))
                alphas.append(jnp.exp2(m_old - m_new))
                probs.append(jnp.exp2(s - m_new).astype(BF16))
                m_scr[h] = m_new
            for h in range(A_HEADS):
                pv = jnp.dot(vt_ref[kb, h * VT_ROWS:(h + 1) * VT_ROWS, :], probs[h],
                             preferred_element_type=F32)
                acc_scr[h] = alphas[h] * acc_scr[h] + pv

    def is_far(kb):
        return (pmin_q - pmax_ref[b, kb]) >= MAX_DISTANCE

    def att_single(kb, c):
        far = is_far(kb)
        early_far = (pmin_q - pmaxh_ref[b, 2 * kb]) >= MAX_DISTANCE
        late_far = (pminh_ref[b, 2 * qi + 1] - pmaxh_ref[b, 2 * kb + 1]) >= MAX_DISTANCE
        corner = jnp.logical_and(jnp.logical_not(far), jnp.logical_and(early_far, late_far))
        rest = jnp.logical_not(jnp.logical_or(far, corner))
        diagonal = jnp.logical_and(rest, kb == qi)
        pl.when(far)(lambda: att_tiles([kb], False))
        pl.when(corner)(lambda: att_tiles([kb], "corner"))
        pl.when(diagonal)(lambda: att_tiles([kb], "diagonal"))
        pl.when(jnp.logical_and(rest, kb != qi))(lambda: att_tiles([kb], "all"))
        return c

    def att_pair_body(j, c):
        kb0 = 2 * j
        kb1 = kb0 + 1
        both_far = jnp.logical_and(kb1 < nkb, jnp.logical_and(is_far(kb0), is_far(kb1)))
        pl.when(both_far)(lambda: att_tiles([kb0, kb1], False))

        @pl.when(jnp.logical_not(both_far))
        def _():
            lax.fori_loop(kb0, jnp.minimum(kb1 + 1, nkb), att_single, 0)

        return c

    lax.fori_loop(0, lax.shift_right_logical(nkb + 1, 1), att_pair_body, 0)

    for h in range(A_HEADS):
        hs = slice(h * A_HEAD_DIM, (h + 1) * A_HEAD_DIM)
        z = az_ref[:, hs].astype(F32)
        out_t = acc_scr[h, :A_HEAD_DIM] / acc_scr[h, A_HEAD_DIM:A_HEAD_DIM + 1]
        o_ref[:, hs] = (out_t.T * (z * jax.nn.sigmoid(z))).astype(o_ref.dtype)


def _dsa(pa, vt, ikd, iwt, positions, rel_bias):
    pa3 = pa.reshape(BATCH, SEQ, PA_WIDTH)
    ikd3 = ikd.reshape(BATCH, SEQ, LANES)
    posq = positions.reshape(BATCH, 1, SEQ)
    posk = jnp.broadcast_to(positions[:, :, None], (BATCH, SEQ, LANES))
    phalf = positions.reshape(BATCH, 2 * NKB, KB // 2)
    pminh = jnp.min(phalf, axis=-1)
    pmaxh = jnp.max(phalf, axis=-1)
    pmin = jnp.min(pminh.reshape(BATCH, NKB, 2), axis=-1)
    pmax = jnp.max(pmaxh.reshape(BATCH, NKB, 2), axis=-1)
    tab = jnp.zeros((A_HEADS, LANES), F32).at[:, :N_BUCKETS].set(rel_bias.astype(F32).T)
    far = rel_bias[N_BUCKETS // 2 - 1, :].astype(F32)

    grid_spec = pltpu.PrefetchScalarGridSpec(
        num_scalar_prefetch=4,
        grid=(BATCH, NQB),
        in_specs=[
            pl.BlockSpec((None, QB, A_WIDTH), lambda b, i, *_: (b, i, PA_Q)),
            pl.BlockSpec((None, SEQ, A_WIDTH), lambda b, i, *_: (b, 0, PA_K),
                         pipeline_mode=pl.Buffered(1)),
            pl.BlockSpec((None, NKB, A_HEADS * VT_ROWS, KB), lambda b, i, *_: (b, 0, 0, 0),
                         pipeline_mode=pl.Buffered(1)),
            pl.BlockSpec((None, QB, A_WIDTH), lambda b, i, *_: (b, i, PA_IQ)),
            pl.BlockSpec((None, SEQ, LANES), lambda b, i, *_: (b, 0, 0)),
            pl.BlockSpec((IDX_HEADS, QB), lambda b, i, *_: (0, b * NQB + i)),
            pl.BlockSpec((None, 1, QB), lambda b, i, *_: (b, 0, i)),
            pl.BlockSpec((None, SEQ, LANES), lambda b, i, *_: (b, 0, 0)),
            pl.BlockSpec((None, QB, A_WIDTH), lambda b, i, *_: (b, i, PA_AZ)),
            pl.BlockSpec((A_HEADS, LANES), lambda b, i, *_: (0, 0)),
            pl.BlockSpec(memory_space=pltpu.SMEM),
        ],
        out_specs=pl.BlockSpec((None, QB, A_WIDTH), lambda b, i, *_: (b, i, 0)),
        scratch_shapes=[
            pltpu.VMEM((NKB, KB, QB), F32),
            pltpu.VMEM((NKB, KB, QB), BF16),
            pltpu.VMEM((NKB, KB, QB), F32),
            pltpu.VMEM((NKB, KB, QB), BF16),
            pltpu.VMEM((IDX_HEADS, QB, LANES), BF16),
            pltpu.VMEM((A_HEADS, VT_ROWS, QB), F32),
            pltpu.VMEM((A_HEADS, 1, QB), F32),
        ],
    )
    out = pl.pallas_call(
        _dsa_kernel,
        grid_spec=grid_spec,
        out_shape=jax.ShapeDtypeStruct((BATCH, SEQ, A_WIDTH), BF16),
        compiler_params=_params(2),
        name="dsa",
    )(pmin, pmax, pminh, pmaxh, pa3, pa3, vt, pa3, ikd3, iwt, posq, posk, pa3, tab, far)
    return out.reshape(TOKENS, A_WIDTH)


def _ret_kernel(cdec_ref, q_ref, k_ref, v_ref, z_ref, gain_ref, dec_ref, te_ref, fs_ref, o_ref,
                state_scr):
    @pl.when(pl.program_id(1) == 0)
    def _():
        state_scr[...] = jnp.zeros(state_scr.shape, F32)

    heads = [slice(h * R_KEY_DIM, (h + 1) * R_KEY_DIM) for h in range(R_HEADS)]

    group = 4

    def chunk_group_body(g, carry):
        rows = [pl.ds(pl.multiple_of((g * group + t) * CHUNK, CHUNK), CHUNK) for t in range(group)]
        scores, kv = [], []
        for t in range(group):
            for h, hs in enumerate(heads):
                k = k_ref[rows[t], hs]
                scores.append(lax.dot_general(q_ref[rows[t], hs], k, NT_DIMS,
                                              preferred_element_type=F32))
                ke = (k.astype(F32) * te_ref[h]).T.astype(BF16)
                kv.append(jnp.dot(ke, v_ref[rows[t], hs], preferred_element_type=F32))
        cross = []
        states = [state_scr[h] for h in range(R_HEADS)]
        for t in range(group):
            for h, hs in enumerate(heads):
                qs = (q_ref[rows[t], hs].astype(F32) * fs_ref[h]).astype(BF16)
                cross.append(jnp.dot(qs, states[h].astype(BF16), preferred_element_type=F32))
                states[h] = states[h] * cdec_ref[h] + kv[t * R_HEADS + h]
        for h in range(R_HEADS):
            state_scr[h] = states[h]
        for t in range(group):
            for h, hs in enumerate(heads):
                i = t * R_HEADS + h
                sc = (scores[i] * dec_ref[h]).astype(BF16)
                y = jnp.dot(sc, v_ref[rows[t], hs], preferred_element_type=F32) + cross[i]
                mu = jnp.mean(y, axis=-1, keepdims=True)
                var = jnp.mean((y - mu) ** 2, axis=-1, keepdims=True)
                yn = (y - mu) * lax.rsqrt(var + EPS) * gain_ref[:, hs]
                z = z_ref[rows[t], hs].astype(F32)
                o_ref[rows[t], hs] = (yn * (z * jax.nn.sigmoid(z))).astype(o_ref.dtype)
        return carry

    lax.fori_loop(0, RB // (CHUNK * group), chunk_group_body, 0)


def _retention(rqk, pb, gn_gain):
    log_g = jnp.log(1.0 - 2.0 ** (-5.0 - jnp.arange(R_HEADS, dtype=F32)))
    pos = jnp.arange(CHUNK, dtype=F32)
    dist = jnp.abs(pos[:, None] - pos[None, :])
    intra_decay = jnp.exp(log_g[:, None, None] * dist)
    to_end = jnp.exp(log_g[:, None] * (CHUNK - 1.0 - pos)[None, :])
    from_start = jnp.exp(log_g[:, None] * (pos + 1.0)[None, :])
    chunk_decay = jnp.exp(log_g * CHUNK)
    te = jnp.broadcast_to(to_end[:, :, None], (R_HEADS, CHUNK, R_KEY_DIM))
    fs = jnp.broadcast_to(from_start[:, :, None], (R_HEADS, CHUNK, R_KEY_DIM))

    rqk3 = rqk.reshape(BATCH, SEQ, 2 * R_WIDTH)
    pb3 = pb.reshape(BATCH, SEQ, PB_WIDTH)
    out = pl.pallas_call(
        _ret_kernel,
        grid=(BATCH, SEQ // RB),
        in_specs=[
            pl.BlockSpec(memory_space=pltpu.SMEM),
            pl.BlockSpec((None, RB, R_WIDTH), lambda b, i: (b, i, 0)),
            pl.BlockSpec((None, RB, R_WIDTH), lambda b, i: (b, i, 1)),
            pl.BlockSpec((None, RB, R_WIDTH), lambda b, i: (b, i, PB_RV)),
            pl.BlockSpec((None, RB, R_WIDTH), lambda b, i: (b, i, PB_RZ)),
            pl.BlockSpec((1, R_WIDTH), lambda b, i: (0, 0)),
            pl.BlockSpec((R_HEADS, CHUNK, CHUNK), lambda b, i: (0, 0, 0)),
            pl.BlockSpec((R_HEADS, CHUNK, R_KEY_DIM), lambda b, i: (0, 0, 0)),
            pl.BlockSpec((R_HEADS, CHUNK, R_KEY_DIM), lambda b, i: (0, 0, 0)),
        ],
        out_specs=pl.BlockSpec((None, RB, R_WIDTH), lambda b, i: (b, i, 0)),
        out_shape=jax.ShapeDtypeStruct((BATCH, SEQ, R_WIDTH), BF16),
        scratch_shapes=[pltpu.VMEM((R_HEADS, R_KEY_DIM, R_VAL_DIM), F32)],
        compiler_params=_params(2),
        name="retention",
    )(chunk_decay, rqk3, rqk3, pb3, pb3, gn_gain.reshape(1, R_WIDTH), intra_decay, te, fs)
    return out.reshape(TOKENS, R_WIDTH)


def _out_kernel(a_ref, b_ref, ga_ref, gb_ref, x_ref, p_ref, wa_ref, wb_ref, wo_ref, wp_ref, wg_ref,
                fg_ref, o_ref):
    ta = jnp.dot(a_ref[...], wa_ref[...], preferred_element_type=F32)
    tb = jnp.dot(b_ref[...], wb_ref[...], preferred_element_type=F32)
    merged = (jax.nn.sigmoid(ga_ref[...].astype(F32)) * ta
              + jax.nn.sigmoid(gb_ref[...].astype(F32)) * tb)
    r = x_ref[...] + jnp.dot(merged.astype(BF16), wo_ref[...], preferred_element_type=F32)
    u = jnp.dot(p_ref[...].astype(BF16), wp_ref[...], preferred_element_type=F32)
    g = jnp.dot(r.astype(BF16), wg_ref[...], preferred_element_type=F32)
    y = r + u * jax.nn.sigmoid(g)
    ms = jnp.mean(y * y, axis=-1, keepdims=True)
    o_ref[...] = y * lax.rsqrt(ms + EPS) * fg_ref[...]


def _output(a_out, b_out, pb, x2d, p2d, wa, wb, wo, wp, wg, final_gain, tm=256):
    def resident(shape):
        return pl.BlockSpec(shape, lambda i: (0, 0), pipeline_mode=pl.Buffered(1))

    return pl.pallas_call(
        _out_kernel,
        grid=(TOKENS // tm,),
        in_specs=[pl.BlockSpec((tm, A_WIDTH), lambda i: (i, 0)),
                  pl.BlockSpec((tm, R_WIDTH), lambda i: (i, 0)),
                  pl.BlockSpec((tm, D_MODEL), lambda i: (i, PB_GA * R_WIDTH // D_MODEL)),
                  pl.BlockSpec((tm, D_MODEL), lambda i: (i, PB_GB * R_WIDTH // D_MODEL)),
                  pl.BlockSpec((tm, D_MODEL), lambda i: (i, 0)),
                  pl.BlockSpec((tm, PLE_DIM), lambda i: (i, 0)),
                  resident((A_WIDTH, D_MODEL)),
                  resident((R_WIDTH, D_MODEL)),
                  resident((D_MODEL, D_MODEL)),
                  resident((PLE_DIM, D_MODEL)),
                  resident((D_MODEL, D_MODEL)),
                  pl.BlockSpec((1, D_MODEL), lambda i: (0, 0))],
        out_specs=pl.BlockSpec((tm, D_MODEL), lambda i: (i, 0)),
        out_shape=jax.ShapeDtypeStruct((TOKENS, D_MODEL), F32),
        compiler_params=_params(1),
        name="output",
    )(a_out, b_out, pb, pb, x2d, p2d, wa, wb, wo, wp, wg, final_gain.reshape(1, D_MODEL))


def kernel(x, p, positions, w_in, norm_gain, w_a_out, w_b_out, w_o, ret_gn_gain, w_ple, w_ple_gate,
           rel_bias, final_gain):
    assert x.shape == (BATCH, SEQ, D_MODEL) and w_in.shape == (1, D_MODEL, IN_WIDTH)
    x2d = x.reshape(TOKENS, D_MODEL)
    p2d = p[0].reshape(TOKENS, PLE_DIM)
    wt = jnp.swapaxes(w_in[0], 0, 1)
    col = jnp.arange(PA_WIDTH)
    pa_scale = jnp.where(col < A_WIDTH, A_HEAD_DIM ** -0.5 * LOG2E,
                         jnp.where(col < PA_IQ * A_WIDTH, 1.0, IDX_DIM ** -0.5))
    pa_scale = pa_scale.astype(F32).reshape(1, PA_WIDTH)

    h = _rmsnorm(x2d, norm_gain[0])
    pa = _proj(h, wt, "proj_att", 0, PA_WIDTH, scale=pa_scale,
               skip=(2 * A_WIDTH, 3 * A_WIDTH))
    vt = _proj_vt(h, wt, 2 * A_WIDTH)
    pb = _proj(h, wt, "proj_ret", COL_RV, PB_WIDTH)
    rqk = _proj_rope(h, wt, positions)
    ikd, iwt = _proj_idx(h, wt)

    a_out = _dsa(pa, vt, ikd, iwt, positions, rel_bias)
    b_out = _retention(rqk, pb, ret_gn_gain[0])

    out = _output(a_out, b_out, pb, x2d, p2d, w_a_out[0].astype(BF16), w_b_out[0].astype(BF16),
                  w_o[0].astype(BF16), w_ple[0].astype(BF16), w_ple_gate[0].astype(BF16), final_gain)
    return out.reshape(BATCH, SEQ, D_MODEL)
```

```python
import functools
import math

import jax
import jax.numpy as jnp
from jax import lax
from jax.experimental import pallas as pl
from jax.experimental.pallas import tpu as pltpu

D_MODEL = 2048
BATCH = 4
SEQ = 4096
TOKENS = BATCH * SEQ
CHUNK = 64
PLE_DIM = 256
EPS = 1e-6
A_HEADS = 8
A_HEAD_DIM = 128
A_WIDTH = A_HEADS * A_HEAD_DIM
IDX_HEADS = 16
IDX_DIM = 64
TOPK = min(256, SEQ // 4)
R_HEADS = 8
R_KEY_DIM = 128
R_VAL_DIM = 128
R_WIDTH = R_HEADS * R_VAL_DIM
ROPE_BASE = 10000.0
N_BUCKETS = 32
MAX_DISTANCE = 128

COL_IQ = 4 * A_WIDTH
COL_IK = COL_IQ + IDX_HEADS * IDX_DIM
COL_IW = COL_IK + IDX_DIM
COL_RQ = COL_IW + IDX_HEADS
COL_RV = COL_RQ + 2 * R_WIDTH
IN_WIDTH = COL_RV + 2 * R_WIDTH + 2 * D_MODEL

PA_Q, PA_K, PA_AZ, PA_IQ = 0, 1, 2, 3
PA_WIDTH = 3 * A_WIDTH + IDX_HEADS * IDX_DIM
PB_RV, PB_RZ, PB_GA, PB_GB = 0, 1, 2, 4
PB_WIDTH = 2 * R_WIDTH + 2 * D_MODEL

LANES = 128
QB = 256
KB = 256
NQB = SEQ // QB
NKB = SEQ // KB
VT_ROWS = A_HEAD_DIM + 16
RB = 512
NEG = -1e30
LOG2E = math.log2(math.e)
VMEM_LIMIT = 56 * 1024 * 1024

F32 = jnp.float32
BF16 = jnp.bfloat16
NT_DIMS = (((1,), (1,)), ((), ()))


def _params(n_axes):
    return pltpu.CompilerParams(dimension_semantics=("arbitrary",) * n_axes,
                                vmem_limit_bytes=VMEM_LIMIT)


def _rmsnorm_kernel(x_ref, g_ref, o_ref):
    x = x_ref[...]
    ms = jnp.mean(x * x, axis=-1, keepdims=True)
    o_ref[...] = (x * lax.rsqrt(ms + EPS) * g_ref[...]).astype(o_ref.dtype)


def _rmsnorm(x2d, gain, tm=512):
    return pl.pallas_call(
        _rmsnorm_kernel,
        grid=(TOKENS // tm,),
        in_specs=[pl.BlockSpec((tm, D_MODEL), lambda i: (i, 0)),
                  pl.BlockSpec((1, D_MODEL), lambda i: (0, 0))],
        out_specs=pl.BlockSpec((tm, D_MODEL), lambda i: (i, 0)),
        out_shape=jax.ShapeDtypeStruct((TOKENS, D_MODEL), BF16),
        compiler_params=_params(1),
        name="rmsnorm",
    )(x2d, gain.reshape(1, D_MODEL))


def _wt_block(rows, row_of):
    assert rows % 8 == 0
    return pl.BlockSpec((pl.Element(rows), pl.Element(D_MODEL)),
                        lambda *g: (pl.multiple_of(row_of(*g), 8), 0))


def _proj_kernel(h_ref, wt_ref, *rest, scale):
    acc = lax.dot_general(h_ref[...], wt_ref[...].astype(BF16), NT_DIMS,
                          preferred_element_type=F32)
    if scale == "row":
        acc = acc * rest[0][...]
    elif scale is not None:
        acc = acc * scale
    rest[-1][...] = acc.astype(rest[-1].dtype)


def _proj(h, wt, name, row0, ncols, scale=None, skip=None, tm=2048, tn=512):
    def row_of(i, j):
        if skip is not None:
            j = jnp.where(j < skip[0] // tn, j, j + (skip[1] - skip[0]) // tn)
        return row0 + j * tn

    in_specs = [pl.BlockSpec((tm, D_MODEL), lambda i, j: (i, 0)), _wt_block(tn, row_of)]
    args = [h, wt]
    if scale is not None and not isinstance(scale, float):
        in_specs.append(pl.BlockSpec((1, tn), lambda i, j: (0, j)))
        args.append(scale)
        scale = "row"
    return pl.pallas_call(
        functools.partial(_proj_kernel, scale=scale),
        grid=(TOKENS // tm, ncols // tn),
        in_specs=in_specs,
        out_specs=pl.BlockSpec((tm, tn), lambda i, j: (i, j)),
        out_shape=jax.ShapeDtypeStruct((TOKENS, ncols), BF16),
        compiler_params=_params(2),
        name=name,
    )(*args)


def _proj_vt_kernel(h_ref, wt_ref, o_ref):
    acc_t = lax.dot_general(wt_ref[...].astype(BF16), h_ref[...], NT_DIMS,
                            preferred_element_type=F32)
    ones = jnp.ones((VT_ROWS - A_HEAD_DIM, KB), o_ref.dtype)
    for t in range(o_ref.shape[0]):
        for hh in range(acc_t.shape[0] // A_HEAD_DIM):
            r0 = hh * VT_ROWS
            o_ref[t, r0:r0 + A_HEAD_DIM] = acc_t[hh * A_HEAD_DIM:(hh + 1) * A_HEAD_DIM,
                                                 t * KB:(t + 1) * KB].astype(o_ref.dtype)
            o_ref[t, r0 + A_HEAD_DIM:r0 + VT_ROWS] = ones


def _proj_vt(h, wt, row0, tm=2048, tn=512):
    per_batch = SEQ // tm
    rows = tn // A_HEAD_DIM * VT_ROWS
    return pl.pallas_call(
        _proj_vt_kernel,
        grid=(TOKENS // tm, A_WIDTH // tn),
        in_specs=[pl.BlockSpec((tm, D_MODEL), lambda i, j: (i, 0)),
                  _wt_block(tn, lambda i, j: row0 + j * tn)],
        out_specs=pl.BlockSpec((None, tm // KB, rows, KB),
                               lambda i, j: (i // per_batch, i % per_batch, j, 0)),
        out_shape=jax.ShapeDtypeStruct((BATCH, NKB, A_HEADS * VT_ROWS, KB), BF16),
        compiler_params=_params(2),
        name="proj_vt",
    )(h, wt)


def _proj_idx_kernel(h_ref, wik_ref, wiw_ref, ik_ref, iwt_ref):
    h = h_ref[...]
    wik = wik_ref[...].astype(BF16)
    ik = lax.dot_general(h, jnp.concatenate([wik, wik], axis=0), NT_DIMS,
                         preferred_element_type=F32)
    mu = jnp.mean(ik, axis=-1, keepdims=True)
    var = jnp.mean((ik - mu) ** 2, axis=-1, keepdims=True)
    ik_ref[...] = ((ik - mu) * lax.rsqrt(var + EPS)).astype(ik_ref.dtype)
    iwt = lax.dot_general(wiw_ref[...].astype(BF16), h, NT_DIMS, preferred_element_type=F32)
    iwt_ref[...] = iwt * (IDX_HEADS ** -0.5)


def _proj_idx(h, wt, tm=2048):
    return pl.pallas_call(
        _proj_idx_kernel,
        grid=(TOKENS // tm,),
        in_specs=[pl.BlockSpec((tm, D_MODEL), lambda i: (i, 0)),
                  _wt_block(IDX_DIM, lambda i: COL_IK),
                  _wt_block(IDX_HEADS, lambda i: COL_IW)],
        out_specs=[pl.BlockSpec((tm, LANES), lambda i: (i, 0)),
                   pl.BlockSpec((IDX_HEADS, tm), lambda i: (0, i))],
        out_shape=[jax.ShapeDtypeStruct((TOKENS, LANES), BF16),
                   jax.ShapeDtypeStruct((IDX_HEADS, TOKENS), F32)],
        compiler_params=_params(1),
        name="proj_idx",
    )(h, wt, wt)


def _proj_rope_kernel(h_ref, w_ref, pos_ref, freq_ref, o_ref, cos_scr, sin_scr, *,
                      tn, k_tile0, k_scale):
    @pl.when(pl.program_id(1) == 0)
    def _():
        hm = pos_ref.shape[0]
        ang = pos_ref[...].astype(F32) * freq_ref[...]
        c = jnp.cos(ang)
        s = jnp.sin(ang)
        cr = pltpu.roll(c, LANES // 2, 1)
        sr = pltpu.roll(s, LANES // 2, 1)
        low = lax.broadcasted_iota(jnp.int32, c.shape, 1) < LANES // 2
        cos_scr[:hm] = jnp.where(low, c, cr)
        cos_scr[hm:] = jnp.where(low, cr, c)
        sin_scr[:hm] = jnp.where(low, -s, sr)
        sin_scr[hm:] = jnp.where(low, -sr, s)

    acc = lax.dot_general(h_ref[...], w_ref[...].astype(BF16), NT_DIMS,
                          preferred_element_type=F32)
    cos = cos_scr[...]
    sin = sin_scr[...]
    scale = jnp.where(pl.program_id(1) >= k_tile0, k_scale, 1.0).astype(F32)
    for g in range(tn // LANES):
        xg = acc[:, g * LANES:(g + 1) * LANES]
        rot = xg * cos + pltpu.roll(xg, LANES // 2, 1) * sin
        o_ref[:, g * LANES:(g + 1) * LANES] = (rot * scale).astype(o_ref.dtype)


def _proj_rope(h, wt, positions, tm=2048, tn=512):
    ncols = 2 * R_WIDTH
    half = R_KEY_DIM // 2
    inv_freq = ROPE_BASE ** (-jnp.arange(half, dtype=F32) / half)
    freq2 = jnp.concatenate([inv_freq, inv_freq]).reshape(1, R_KEY_DIM)
    pos_t = positions.reshape(TOKENS // tm, 2, tm // 2, 1)
    pos2 = jnp.concatenate([jnp.broadcast_to(pos_t[:, 0], (TOKENS // tm, tm // 2, half)),
                            jnp.broadcast_to(pos_t[:, 1], (TOKENS // tm, tm // 2, half))], axis=-1)
    pos2 = pos2.reshape(TOKENS // 2, R_KEY_DIM)
    kern = functools.partial(_proj_rope_kernel, tn=tn, k_tile0=R_WIDTH // tn,
                             k_scale=R_KEY_DIM ** -0.5)
    return pl.pallas_call(
        kern,
        grid=(TOKENS // tm, ncols // tn),
        in_specs=[pl.BlockSpec((tm, D_MODEL), lambda i, j: (i, 0)),
                  _wt_block(tn, lambda i, j: COL_RQ + j * tn),
                  pl.BlockSpec((tm // 2, R_KEY_DIM), lambda i, j: (i, 0)),
                  pl.BlockSpec((1, R_KEY_DIM), lambda i, j: (0, 0))],
        out_specs=pl.BlockSpec((tm, tn), lambda i, j: (i, j)),
        out_shape=jax.ShapeDtypeStruct((TOKENS, ncols), BF16),
        scratch_shapes=[pltpu.VMEM((tm, R_KEY_DIM), F32), pltpu.VMEM((tm, R_KEY_DIM), F32)],
        compiler_params=_params(2),
        name="proj_rope",
    )(h, wt, pos2, freq2)


def _dsa_kernel(pmin_ref, pmax_ref, pminh_ref, pmaxh_ref,
                q_ref, k_ref, vt_ref, iq_ref, ikd_ref, iwt_ref, posq_ref, posk_ref, az_ref,
                tab_ref, far_ref, o_ref,
                idx_scr, idxb_scr, off_scr, dig_scr, iqm_scr, acc_scr, m_scr):
    b = pl.program_id(0)
    qi = pl.program_id(1)
    nkb = qi + 1

    lane =lax.broadcasted_iota(jnp.int32, (QB, LANES), 1)
    for p in range(IDX_HEADS // 2):
        pair = iq_ref[:, p * LANES:(p + 1) * LANES].astype(F32)
        iqm_scr[2 * p] = jnp.where(lane < IDX_DIM, pair, 0.0).astype(BF16)
        iqm_scr[2 * p + 1] = jnp.where(lane >= IDX_DIM, pair, 0.0).astype(BF16)
    iwt = iwt_ref[...]

    rr = lax.broadcasted_iota(jnp.int32, (KB, QB), 0)
    cc = lax.broadcasted_iota(jnp.int32, (KB, QB), 1)

    def idx_scores(kb):
        s0 = pl.multiple_of(kb * KB, KB)
        kid = ikd_ref[pl.ds(s0, KB), :]
        acc = jnp.zeros((KB, QB), F32)
        for h in range(IDX_HEADS):
            sc = lax.dot_general(kid, iqm_scr[h], NT_DIMS, preferred_element_type=F32)
            acc = acc + jnp.maximum(sc, 0.0) * iwt[h:h + 1, :]
        return acc

    def idx_store(kb, acc, diagonal):
        if diagonal:
            admissible = (rr // CHUNK) <= (cc // CHUNK)
            acc = jnp.where(admissible, acc, -jnp.inf)
        idx_scr[kb] = acc
        idxb_scr[kb] = acc.astype(BF16)

    def idx_tiles(kbs, diagonal_last):
        accs = [idx_scores(kb) for kb in kbs]
        for t, kb in enumerate(kbs):
            idx_store(kb, accs[t], diagonal_last and t == len(kbs) - 1)

    def idx_pair_body(j, c):
        idx_tiles([2 * j, 2 * j + 1], False)
        return c

    lax.fori_loop(0, lax.shift_right_logical(qi, 1), idx_pair_body, 0)
    qi_odd = (qi & 1) == 1
    pl.when(qi_odd)(lambda: idx_tiles([qi - 1, qi], True))
    pl.when(jnp.logical_not(qi_odd))(lambda: idx_tiles([qi], True))

    nkb_sel = jnp.where(qi > 0, nkb, 0)
    npair_sel = lax.shift_right_logical(nkb_sel + 1, 1)

    @pl.when((nkb & 1) == 1)
    def _():
        idxb_scr[nkb] = jnp.full((KB, QB), -jnp.inf, BF16)
        dig_scr[nkb] = jnp.full((KB, QB), -300.0, BF16)

    def count_ge(plane_ref, cand, group):
        dt = plane_ref.dtype
        one = jnp.ones((), dt)
        zero = jnp.zeros((), dt)

        def body(j, acc):
            slabs = []
            for kb in (2 * j, 2 * j + 1):
                part = jnp.where(plane_ref[kb] >= cand, one, zero)
                slabs += [part[group * g:group * (g + 1)] for g in range(KB // group)]
            while len(slabs) > 1:
                slabs = [slabs[i] + slabs[i + 1] for i in range(0, len(slabs), 2)]
            return acc + slabs[0]

        acc = lax.fori_loop(0, npair_sel, body, jnp.zeros((group, QB), dt))
        return jnp.sum(acc.astype(F32), axis=0, keepdims=True)

    def bisect(n_bits, count_fn):
        def bit_body(it, ans):
            cand = ans | jnp.left_shift(jnp.int32(1), n_bits - 1 - it)
            return jnp.where(count_fn(cand) >= TOPK, cand, ans)

        return lax.fori_loop(0, jnp.where(qi > 0, n_bits, 0), bit_body,
                             jnp.zeros((1, QB), jnp.int32))

    def bf16_candidate(c16):
        s16 = c16 - 32768
        b16 = jnp.where(s16 < 0, s16 ^ 0x7FFF, s16)
        return pltpu.bitcast(b16 << 16, F32).astype(BF16)

    c16 = bisect(16, lambda c: count_ge(idxb_scr, bf16_candidate(c), 16))
    s16 = c16 - 32768
    base_bits = jnp.where(s16 < 0, s16 ^ 0x7FFF, s16) << 16
    base = pltpu.bitcast(base_bits, F32)
    expo = (base_bits >> 23) & 0xFF
    scale = pltpu.bitcast(jnp.clip(278 - expo, 1, 254) << 23, F32)
    unit = pltpu.bitcast(jnp.clip(expo - 24, 1, 254) << 23, F32)
    span = 65536.0
    radix = 512.0
    hi_shift = 192.0
    lo_shift = 256.0

    def hi_body(kb, c):
        q = jnp.clip((idx_scr[kb] - base) * scale + span, -radix, 3.0 * span + radix - 1.0)
        off_scr[kb] = q
        dig_scr[kb] = (jnp.floor(q * (1.0 / radix)) - hi_shift).astype(BF16)
        return c

    lax.fori_loop(0, nkb_sel, hi_body, 0)
    c_hi = bisect(9, lambda c: count_ge(dig_scr, (c - 192).astype(F32).astype(BF16), 16))
    c_hi = c_hi.astype(F32)

    def lo_body(kb, c):
        q = off_scr[kb]
        hi = jnp.floor(q * (1.0 / radix))
        lo = jnp.where(hi > c_hi, lo_shift,
                       jnp.where(hi == c_hi, q - radix * c_hi - lo_shift, -lo_shift - 2.0))
        dig_scr[kb] = lo.astype(BF16)
        return c

    lax.fori_loop(0, nkb_sel, lo_body, 0)
    c_lo = bisect(9, lambda c: count_ge(dig_scr, (c - 256).astype(F32).astype(BF16), 16))
    thr = base + (radix * c_hi + c_lo.astype(F32) - span) * unit
    thr = jnp.where(qi > 0, thr, jnp.finfo(F32).min)

    def count_where(pred_fn):
        def body(kb, acc):
            part = jnp.where(pred_fn(kb, idx_scr[kb]), 1.0, 0.0)
            return acc + jnp.sum(part.reshape(KB // 8, 8, QB), axis=0)
        acc = lax.fori_loop(0, nkb, body, jnp.zeros((8, QB), F32))
        return jnp.sum(acc, axis=0, keepdims=True)

    cnt_ge_thr = count_ge(dig_scr, (c_lo - 256).astype(F32).astype(BF16), 16)
    has_tie = jnp.logical_and(qi > 0, jnp.max(cnt_ge_thr) > TOPK)

    @pl.when(has_tie)
    def _():
        cnt_gt = count_where(lambda kb, kk: kk > thr)
        need = TOPK - cnt_gt

        def key_index(kb):
            return kb * KB + rr

        def jb_body(it, j0):
            cand = j0 | jnp.left_shift(jnp.int32(1), 11 - it)
            f = count_where(lambda kb, kk: jnp.logical_and(kk == thr, key_index(kb) < cand))
            return jnp.where(f < need, cand, j0)

        j0 = lax.fori_loop(0, 12, jb_body, jnp.zeros((1, QB), jnp.int32))
        jstar = j0 + 1

        def fix_body(kb, c):
            kk = idx_scr[kb]
            drop = jnp.logical_and(kk == thr, key_index(kb) >= jstar)
            idx_scr[kb] = jnp.where(drop, -jnp.inf, kk)
            return c

        lax.fori_loop(0, nkb, fix_body, 0)

    m_scr[...] = jnp.full(m_scr.shape, NEG, F32)
    acc_scr[...] = jnp.zeros(acc_scr.shape, F32)
    posq = posq_ref[...]
    pmin_q = pmin_ref[b, qi]
    half = N_BUCKETS // 2
    max_exact = half // 2

    def t5_bucket(pk, pq):
        rel = jnp.concatenate([pk] * (pq.shape[1] // LANES), axis=1) - pq
        n = jnp.abs(rel)
        nf = jnp.maximum(n, 1).astype(F32)
        large = max_exact + (jnp.log(nf / max_exact) / math.log(MAX_DISTANCE / max_exact)
                             * (half - max_exact)).astype(jnp.int32)
        large = jnp.minimum(large, half - 1)
        return jnp.where(rel > 0, half, 0) + jnp.where(n < max_exact, n, large)

    def shifted_bias(h, bucket):
        row = jnp.broadcast_to(tab_ref[h:h + 1, :], (bucket.shape[0], LANES))
        bias = jnp.concatenate(
            [jnp.take_along_axis(row, bucket[:, g * LANES:(g + 1) * LANES], axis=1)
             for g in range(bucket.shape[1] // LANES)], axis=1)
        return (bias - far_ref[h]) * LOG2E

    def att_tiles(kbs, near):
        hk, hq = KB // 2, QB // 2
        scores = {}
        for t, kb in enumerate(kbs):
            s0 = pl.multiple_of(kb * KB, KB)
            for h in range(A_HEADS):
                hs = slice(h * A_HEAD_DIM, (h + 1) * A_HEAD_DIM)
                scores[t, h] = lax.dot_general(k_ref[pl.ds(s0, KB), hs], q_ref[:, hs], NT_DIMS,
                                               preferred_element_type=F32)
        if len(kbs) > 1:
            assert not near
            madds = [jnp.where(idx_scr[kb] >= thr, 0.0, NEG) for kb in kbs]
            probs, alphas = {}, []
            for h in range(A_HEADS):
                ss = [scores[t, h] + madds[t] for t in range(len(kbs))]
                m_old = m_scr[h]
                m_new = m_old
                for s in ss:
                    m_new = jnp.maximum(m_new, jnp.max(s, axis=0, keepdims=True))
                alphas.append(jnp.exp2(m_old - m_new))
                for t, s in enumerate(ss):
                    probs[t, h] = jnp.exp2(s - m_new).astype(BF16)
                m_scr[h] = m_new
            for h in range(A_HEADS):
                rows = slice(h * VT_ROWS, (h + 1) * VT_ROWS)
                pv = sum(jnp.dot(vt_ref[kb, rows, :], probs[t, h], preferred_element_type=F32)
                         for t, kb in enumerate(kbs))
                acc_scr[h] = alphas[h] * acc_scr[h] + pv
            return
        for t, kb in enumerate(kbs):
            s0 = pl.multiple_of(kb * KB, KB)
            madd = jnp.where(idx_scr[kb] >= thr, 0.0, NEG)
            if near == "all":
                bucket = t5_bucket(posk_ref[pl.ds(s0, KB), :], posq)
            elif near == "corner":
                bucket = t5_bucket(posk_ref[pl.ds(s0 + hk, hk), :], posq[:, :hq])
            elif near == "diagonal":
                bucket = t5_bucket(posk_ref[pl.ds(s0, hk), :], posq)
                bucket_lr = t5_bucket(posk_ref[pl.ds(s0 + hk, hk), :], posq[:, hq:])
            probs, alphas = [], []
            for h in range(A_HEADS):
                if near == "all":
                    s = scores[t, h] + (shifted_bias(h, bucket) + madd)
                elif near == "corner":
                    raw = scores[t, h]
                    low_left = raw[hk:, :hq] + (shifted_bias(h, bucket) + madd[hk:, :hq])
                    low = jnp.concatenate([low_left, raw[hk:, hq:] + madd[hk:, hq:]], axis=1)
                    s = jnp.concatenate([raw[:hk] + madd[:hk], low], axis=0)
                elif near == "diagonal":
                    raw = scores[t, h]
                    top = raw[:hk] + (shifted_bias(h, bucket) + madd[:hk])
                    low_right = raw[hk:, hq:] + (shifted_bias(h, bucket_lr) + madd[hk:, hq:])
                    low = jnp.concatenate([raw[hk:, :hq] + madd[hk:, :hq], low_right], axis=1)
                    s = jnp.concatenate([top, low], axis=0)
                else:
                    s = scores[t, h] + madd
                m_old = m_scr[h]
                m_new = jnp.maximum(m_old, jnp.max(s, axis=0, keepdims=True))
                alphas.append(jnp.exp2(m_old - m_new))
                probs.append(jnp.exp2(s - m_new).astype(BF16))
                m_scr[h] = m_new
            for h in range(A_HEADS):
                pv = jnp.dot(vt_ref[kb, h * VT_ROWS:(h + 1) * VT_ROWS, :], probs[h],
                             preferred_element_type=F32)
                acc_scr[h] = alphas[h] * acc_scr[h] + pv

    def is_far(kb):
        return (pmin_q - pmax_ref[b, kb]) >= MAX_DISTANCE

    def att_single(kb, c):
        far = is_far(kb)
        early_far = (pmin_q - pmaxh_ref[b, 2 * kb]) >= MAX_DISTANCE
        late_far = (pminh_ref[b, 2 * qi + 1] - pmaxh_ref[b, 2 * kb + 1]) >= MAX_DISTANCE
        corner = jnp.logical_and(jnp.logical_not(far), jnp.logical_and(early_far, late_far))
        rest = jnp.logical_not(jnp.logical_or(far, corner))
        diagonal = jnp.logical_and(rest, kb == qi)
        pl.when(far)(lambda: att_tiles([kb], False))
        pl.when(corner)(lambda: att_tiles([kb], "corner"))
        pl.when(diagonal)(lambda: att_tiles([kb], "diagonal"))
        pl.when(jnp.logical_and(rest, kb != qi))(lambda: att_tiles([kb], "all"))
        return c

    def att_pair_body(j, c):
        kb0 = 2 * j
        kb1 = kb0 + 1
        both_far = jnp.logical_and(kb1 < nkb, jnp.logical_and(is_far(kb0), is_far(kb1)))
        pl.when(both_far)(lambda: att_tiles([kb0, kb1], False))

        @pl.when(jnp.logical_not(both_far))
        def _():
            lax.fori_loop(kb0, jnp.minimum(kb1 + 1, nkb), att_single, 0)

        return c

    lax.fori_loop(0, lax.shift_right_logical(nkb + 1, 1), att_pair_body, 0)

    for h in range(A_HEADS):
        hs = slice(h * A_HEAD_DIM, (h + 1) * A_HEAD_DIM)
        z = az_ref[:, hs].astype(F32)
        out_t = acc_scr[h, :A_HEAD_DIM] / acc_scr[h, A_HEAD_DIM:A_HEAD_DIM + 1]
        o_ref[:, hs] = (out_t.T * (z * jax.nn.sigmoid(z))).astype(o_ref.dtype)


def _dsa(pa, vt, ikd, iwt, positions, rel_bias):
    pa3 = pa.reshape(BATCH, SEQ, PA_WIDTH)
    ikd3 = ikd.reshape(BATCH, SEQ, LANES)
    posq = positions.reshape(BATCH, 1, SEQ)
    posk = jnp.broadcast_to(positions[:, :, None], (BATCH, SEQ, LANES))
    phalf = positions.reshape(BATCH, 2 * NKB, KB // 2)
    pminh = jnp.min(phalf, axis=-1)
    pmaxh = jnp.max(phalf, axis=-1)
    pmin = jnp.min(pminh.reshape(BATCH, NKB, 2), axis=-1)
    pmax = jnp.max(pmaxh.reshape(BATCH, NKB, 2), axis=-1)
    tab = jnp.zeros((A_HEADS, LANES), F32).at[:, :N_BUCKETS].set(rel_bias.astype(F32).T)
    far = rel_bias[N_BUCKETS // 2 - 1, :].astype(F32)

    grid_spec = pltpu.PrefetchScalarGridSpec(
        num_scalar_prefetch=4,
        grid=(BATCH, NQB),
        in_specs=[
            pl.BlockSpec((None, QB, A_WIDTH), lambda b, i, *_: (b, i, PA_Q)),
            pl.BlockSpec((None, SEQ, A_WIDTH), lambda b, i, *_: (b, 0, PA_K),
                         pipeline_mode=pl.Buffered(1)),
            pl.BlockSpec((None, NKB, A_HEADS * VT_ROWS, KB), lambda b, i, *_: (b, 0, 0, 0),
                         pipeline_mode=pl.Buffered(1)),
            pl.BlockSpec((None, QB, A_WIDTH), lambda b, i, *_: (b, i, PA_IQ)),
            pl.BlockSpec((None, SEQ, LANES), lambda b, i, *_: (b, 0, 0)),
            pl.BlockSpec((IDX_HEADS, QB), lambda b, i, *_: (0, b * NQB + i)),
            pl.BlockSpec((None, 1, QB), lambda b, i, *_: (b, 0, i)),
            pl.BlockSpec((None, SEQ, LANES), lambda b, i, *_: (b, 0, 0)),
            pl.BlockSpec((None, QB, A_WIDTH), lambda b, i, *_: (b, i, PA_AZ)),
            pl.BlockSpec((A_HEADS, LANES), lambda b, i, *_: (0, 0)),
            pl.BlockSpec(memory_space=pltpu.SMEM),
        ],
        out_specs=pl.BlockSpec((None, QB, A_WIDTH), lambda b, i, *_: (b, i, 0)),
        scratch_shapes=[
            pltpu.VMEM((NKB, KB, QB), F32),
            pltpu.VMEM((NKB, KB, QB), BF16),
            pltpu.VMEM((NKB, KB, QB), F32),
            pltpu.VMEM((NKB, KB, QB), BF16),
            pltpu.VMEM((IDX_HEADS, QB, LANES), BF16),
            pltpu.VMEM((A_HEADS, VT_ROWS, QB), F32),
            pltpu.VMEM((A_HEADS, 1, QB), F32),
        ],
    )
    out = pl.pallas_call(
        _dsa_kernel,
        grid_spec=grid_spec,
        out_shape=jax.ShapeDtypeStruct((BATCH, SEQ, A_WIDTH), BF16),
        compiler_params=_params(2),
        name="dsa",
    )(pmin, pmax, pminh, pmaxh, pa3, pa3, vt, pa3, ikd3, iwt, posq, posk, pa3, tab, far)
    return out.reshape(TOKENS, A_WIDTH)


def _ret_kernel(cdec_ref, q_ref, k_ref, v_ref, z_ref, gain_ref, dec_ref, te_ref, fs_ref, o_ref,
                state_scr):
    @pl.when(pl.program_id(1) == 0)
    def _():
        state_scr[...] = jnp.zeros(state_scr.shape, F32)

    heads = [slice(h * R_KEY_DIM, (h + 1) * R_KEY_DIM) for h in range(R_HEADS)]

    group = 4

    def chunk_group_body(g, carry):
        rows = [pl.ds(pl.multiple_of((g * group + t) * CHUNK, CHUNK), CHUNK) for t in range(group)]
        scores, kv = [], []
        for t in range(group):
            for h, hs in enumerate(heads):
                k = k_ref[rows[t], hs]
                scores.append(lax.dot_general(q_ref[rows[t], hs], k, NT_DIMS,
                                              preferred_element_type=F32))
                ke = (k.astype(F32) * te_ref[h]).T.astype(BF16)
                kv.append(jnp.dot(ke, v_ref[rows[t], hs], preferred_element_type=F32))
        cross = []
        states = [state_scr[h] for h in range(R_HEADS)]
        for t in range(group):
            for h, hs in enumerate(heads):
                qs = (q_ref[rows[t], hs].astype(F32) * fs_ref[h]).astype(BF16)
                cross.append(jnp.dot(qs, states[h].astype(BF16), preferred_element_type=F32))
                states[h] = states[h] * cdec_ref[h] + kv[t * R_HEADS + h]
        for h in range(R_HEADS):
            state_scr[h] = states[h]
        for t in range(group):
            for h, hs in enumerate(heads):
                i = t * R_HEADS + h
                sc = (scores[i] * dec_ref[h]).astype(BF16)
                y = jnp.dot(sc, v_ref[rows[t], hs], preferred_element_type=F32) + cross[i]
                mu = jnp.mean(y, axis=-1, keepdims=True)
                var = jnp.mean((y - mu) ** 2, axis=-1, keepdims=True)
                yn = (y - mu) * lax.rsqrt(var + EPS) * gain_ref[:, hs]
                z = z_ref[rows[t], hs].astype(F32)
                o_ref[rows[t], hs] = (yn * (z * jax.nn.sigmoid(z))).astype(o_ref.dtype)
        return carry

    lax.fori_loop(0, RB // (CHUNK * group), chunk_group_body, 0)


def _retention(rqk, pb, gn_gain):
    log_g = jnp.log(1.0 - 2.0 ** (-5.0 - jnp.arange(R_HEADS, dtype=F32)))
    pos = jnp.arange(CHUNK, dtype=F32)
    dist = jnp.abs(pos[:, None] - pos[None, :])
    intra_decay = jnp.exp(log_g[:, None, None] * dist)
    to_end = jnp.exp(log_g[:, None] * (CHUNK - 1.0 - pos)[None, :])
    from_start = jnp.exp(log_g[:, None] * (pos + 1.0)[None, :])
    chunk_decay = jnp.exp(log_g * CHUNK)
    te = jnp.broadcast_to(to_end[:, :, None], (R_HEADS, CHUNK, R_KEY_DIM))
    fs = jnp.broadcast_to(from_start[:, :, None], (R_HEADS, CHUNK, R_KEY_DIM))

    rqk3 = rqk.reshape(BATCH, SEQ, 2 * R_WIDTH)
    pb3 = pb.reshape(BATCH, SEQ, PB_WIDTH)
    out = pl.pallas_call(
        _ret_kernel,
        grid=(BATCH, SEQ // RB),
        in_specs=[
            pl.BlockSpec(memory_space=pltpu.SMEM),
            pl.BlockSpec((None, RB, R_WIDTH), lambda b, i: (b, i, 0)),
            pl.BlockSpec((None, RB, R_WIDTH), lambda b, i: (b, i, 1)),
            pl.BlockSpec((None, RB, R_WIDTH), lambda b, i: (b, i, PB_RV)),
            pl.BlockSpec((None, RB, R_WIDTH), lambda b, i: (b, i, PB_RZ)),
            pl.BlockSpec((1, R_WIDTH), lambda b, i: (0, 0)),
            pl.BlockSpec((R_HEADS, CHUNK, CHUNK), lambda b, i: (0, 0, 0)),
            pl.BlockSpec((R_HEADS, CHUNK, R_KEY_DIM), lambda b, i: (0, 0, 0)),
            pl.BlockSpec((R_HEADS, CHUNK, R_KEY_DIM), lambda b, i: (0, 0, 0)),
        ],
        out_specs=pl.BlockSpec((None, RB, R_WIDTH), lambda b, i: (b, i, 0)),
        out_shape=jax.ShapeDtypeStruct((BATCH, SEQ, R_WIDTH), BF16),
        scratch_shapes=[pltpu.VMEM((R_HEADS, R_KEY_DIM, R_VAL_DIM), F32)],
        compiler_params=_params(2),
        name="retention",
    )(chunk_decay, rqk3, rqk3, pb3, pb3, gn_gain.reshape(1, R_WIDTH), intra_decay, te, fs)
    return out.reshape(TOKENS, R_WIDTH)


def _out_kernel(a_ref, b_ref, ga_ref, gb_ref, x_ref, p_ref, wa_ref, wb_ref, wo_ref, wp_ref, wg_ref,
                fg_ref, o_ref):
    ta = jnp.dot(a_ref[...], wa_ref[...], preferred_element_type=F32)
    tb = jnp.dot(b_ref[...], wb_ref[...], preferred_element_type=F32)
    merged = (jax.nn.sigmoid(ga_ref[...].astype(F32)) * ta
              + jax.nn.sigmoid(gb_ref[...].astype(F32)) * tb)
    r = x_ref[...] + jnp.dot(merged.astype(BF16), wo_ref[...], preferred_element_type=F32)
    u = jnp.dot(p_ref[...].astype(BF16), wp_ref[...], preferred_element_type=F32)
    g = jnp.dot(r.astype(BF16), wg_ref[...], preferred_element_type=F32)
    y = r + u * jax.nn.sigmoid(g)
    ms = jnp.mean(y * y, axis=-1, keepdims=True)
    o_ref[...] = y * lax.rsqrt(ms + EPS) * fg_ref[...]


def _output(a_out, b_out, pb, x2d, p2d, wa, wb, wo, wp, wg, final_gain, tm=256):
    def resident(shape):
        return pl.BlockSpec(shape, lambda i: (0, 0), pipeline_mode=pl.Buffered(1))

    return pl.pallas_call(
        _out_kernel,
        grid=(TOKENS // tm,),
        in_specs=[pl.BlockSpec((tm, A_WIDTH), lambda i: (i, 0)),
                  pl.BlockSpec((tm, R_WIDTH), lambda i: (i, 0)),
                  pl.BlockSpec((tm, D_MODEL), lambda i: (i, PB_GA * R_WIDTH // D_MODEL)),
                  pl.BlockSpec((tm, D_MODEL), lambda i: (i, PB_GB * R_WIDTH // D_MODEL)),
                  pl.BlockSpec((tm, D_MODEL), lambda i: (i, 0)),
                  pl.BlockSpec((tm, PLE_DIM), lambda i: (i, 0)),
                  resident((A_WIDTH, D_MODEL)),
                  resident((R_WIDTH, D_MODEL)),
                  resident((D_MODEL, D_MODEL)),
                  resident((PLE_DIM, D_MODEL)),
                  resident((D_MODEL, D_MODEL)),
                  pl.BlockSpec((1, D_MODEL), lambda i: (0, 0))],
        out_specs=pl.BlockSpec((tm, D_MODEL), lambda i: (i, 0)),
        out_shape=jax.ShapeDtypeStruct((TOKENS, D_MODEL), F32),
        compiler_params=_params(1),
        name="output",
    )(a_out, b_out, pb, pb, x2d, p2d, wa, wb, wo, wp, wg, final_gain.reshape(1, D_MODEL))


def kernel(x, p, positions, w_in, norm_gain, w_a_out, w_b_out, w_o, ret_gn_gain, w_ple, w_ple_gate,
           rel_bias, final_gain):
    assert x.shape == (BATCH, SEQ, D_MODEL) and w_in.shape == (1, D_MODEL, IN_WIDTH)
    x2d = x.reshape(TOKENS, D_MODEL)
    p2d = p[0].reshape(TOKENS, PLE_DIM)
    wt = jnp.swapaxes(w_in[0], 0, 1)
    col = jnp.arange(PA_WIDTH)
    pa_scale = jnp.where(col < A_WIDTH, A_HEAD_DIM ** -0.5 * LOG2E,
                         jnp.where(col < PA_IQ * A_WIDTH, 1.0, IDX_DIM ** -0.5))
    pa_scale = pa_scale.astype(F32).reshape(1, PA_WIDTH)

    h = _rmsnorm(x2d, norm_gain[0])
    pa = _proj(h, wt, "proj_att", 0, PA_WIDTH, scale=pa_scale,
               skip=(2 * A_WIDTH, 3 * A_WIDTH))
    vt = _proj_vt(h, wt, 2 * A_WIDTH)
    pb = _proj(h, wt, "proj_ret", COL_RV, PB_WIDTH)
    rqk = _proj_rope(h, wt, positions)
    ikd, iwt = _proj_idx(h, wt)

    a_out = _dsa(pa, vt, ikd, iwt, positions, rel_bias)
    b_out = _retention(rqk, pb, ret_gn_gain[0])

    out = _output(a_out, b_out, pb, x2d, p2d, w_a_out[0].astype(BF16), w_b_out[0].astype(BF16),
                  w_o[0].astype(BF16), w_ple[0].astype(BF16), w_ple_gate[0].astype(BF16), final_gain)
    return out.reshape(BATCH, SEQ, D_MODEL)
```
